```python
import jax, jax.numpy as jnp
from jax import lax
import numpy as np

D_MODEL = 1024
BATCH = 4
SEQ = 8192
DEPTH = 1
DEC_BATCH = 8
DEC_SEQ = 16
PAST_LEN = 1024

CHUNK = 64
N_PAST_CHUNKS = 8
BAND = (N_PAST_CHUNKS + 1) * CHUNK
HEAD_DIM = 64
ATTN_HEADS = D_MODEL // HEAD_DIM
ATTN_WIDTH = ATTN_HEADS * HEAD_DIM
MAX_REL = 128
GMLP_CHUNK = 128
GMLP_GROUPS = 4
GMLP_WIDTH = D_MODEL
GMLP_GROUP_DIM = GMLP_WIDTH // GMLP_GROUPS
N_BRANCH = 2
IN_WIDTH = 2 * GMLP_WIDTH + 3 * ATTN_WIDTH + N_BRANCH * D_MODEL
SPLITS = (GMLP_WIDTH, 2 * GMLP_WIDTH, 2 * GMLP_WIDTH + ATTN_WIDTH,
          2 * GMLP_WIDTH + 2 * ATTN_WIDTH, 2 * GMLP_WIDTH + 3 * ATTN_WIDTH,
          2 * GMLP_WIDTH + 3 * ATTN_WIDTH + D_MODEL)
N_GROUPS = 4
EXPERTS_PER_GROUP = 8
N_EXPERTS = N_GROUPS * EXPERTS_PER_GROUP
TOP_K_INNER = 2
D_EXPERT = D_MODEL // 4
PLE_DIM = 256
EPS = 1e-6
NEG = -1e30

kernel_name = 'hybrid_gmlp_bandattn_hmoe_step'


def rms_norm(x, g):
    xf = x.astype(jnp.float32)
    y = xf * lax.rsqrt(jnp.mean(xf * xf, axis=-1, keepdims=True) + EPS)
    return (y * g.astype(jnp.float32)).astype(x.dtype)


def mixer_inputs(x, g_mix, w_in, g_q, g_k):
    b, s = x.shape[0], x.shape[1]
    z = rms_norm(x, g_mix) @ w_in
    ua, va, q, k, v, ga, gb = jnp.split(z, SPLITS, axis=-1)
    q = rms_norm(q.reshape(b, s, ATTN_HEADS, HEAD_DIM), g_q)
    k = rms_norm(k.reshape(b, s, ATTN_HEADS, HEAD_DIM), g_k)
    v = v.reshape(b, s, ATTN_HEADS, HEAD_DIM)
    return ua, va, q, k, v, ga, gb


def gmlp_uv(ua, va, g_v):
    return jax.nn.gelu(ua), rms_norm(jax.nn.gelu(va), g_v)


def spatial_gate(u, v, w_s, b_s):
    b, s, w = v.shape
    L = min(s, GMLP_CHUNK)
    n = s // L
    ws = jnp.tril(w_s[:, :L, :L])
    vr = v.reshape(b, n, L, GMLP_GROUPS, GMLP_GROUP_DIM)
    mix = jnp.einsum('gts,bnsgc->bntgc', ws, vr) + jnp.transpose(b_s[:, :L])[None, None, :, :, None]
    return u * mix.reshape(b, s, w)


def chunk_attend(q, k, v, q_pos, k_pos, rel_bias):
    s = jnp.einsum('bqhd,bkhd->bhqk', q, k).astype(jnp.float32) * (HEAD_DIM ** -0.5)
    rel = jnp.clip(q_pos[:, None] - k_pos[None, :], -MAX_REL, MAX_REL) + MAX_REL
    s = s + rel_bias[:, rel].astype(jnp.float32)[None]
    q_chunk = (q_pos // CHUNK)[:, None]
    k_chunk = (k_pos // CHUNK)[None, :]
    ok = (k_pos[None, :] >= 0) & (k_chunk <= q_chunk) & (k_chunk >= q_chunk - N_PAST_CHUNKS)
    s = jnp.where(ok[None, None], s, NEG)
    p = jax.nn.softmax(s, axis=-1).astype(v.dtype)
    return jnp.einsum('bhqk,bkhd->bqhd', p, v)


def band_attention_prompt(q, k, v, rel_bias):
    b, s, h, dh = q.shape
    n_chunks = s // CHUNK
    pad = N_PAST_CHUNKS * CHUNK
    kpad = jnp.pad(k, ((0, 0), (pad, 0), (0, 0), (0, 0)))
    vpad = jnp.pad(v, ((0, 0), (pad, 0), (0, 0), (0, 0)))
    offs_q = jnp.arange(CHUNK, dtype=jnp.int32)
    offs_k = jnp.arange(BAND, dtype=jnp.int32)

    def one_chunk(c):
        start = c * CHUNK
        qc = lax.dynamic_slice_in_dim(q, start, CHUNK, axis=1)
        kc = lax.dynamic_slice_in_dim(kpad, start, BAND, axis=1)
        vc = lax.dynamic_slice_in_dim(vpad, start, BAND, axis=1)
        return chunk_attend(qc, kc, vc, start + offs_q, start - pad + offs_k, rel_bias)

    o = lax.map(one_chunk, jnp.arange(n_chunks, dtype=jnp.int32))
    return jnp.moveaxis(o, 0, 1).reshape(b, s, h * dh)


def band_attention_sample(q, k, v, cache_k, cache_v, rel_bias):
    b, L, h, dh = q.shape
    lc = cache_k.shape[1]
    q_pos = PAST_LEN + jnp.arange(L, dtype=jnp.int32)
    k_pos = jnp.concatenate([PAST_LEN - lc + jnp.arange(lc, dtype=jnp.int32), q_pos])
    kk = jnp.concatenate([cache_k.astype(k.dtype), k], axis=1)
    vv = jnp.concatenate([cache_v.astype(v.dtype), v], axis=1)
    return chunk_attend(q, kk, vv, q_pos, k_pos, rel_bias).reshape(b, L, h * dh)


def merge_branches(x, o_a, o_b, ga, gb, w_branch, w_out):
    b_a = o_a @ w_branch[0]
    b_b = o_b @ w_branch[1]
    m = jax.nn.sigmoid(ga) * b_a + jax.nn.sigmoid(gb) * b_b
    return x + m @ w_out


def hier_moe(h, w_rg, b_rg, w_re, b_re, w_g, w_u, w_d):
    shape = h.shape
    hf = h.reshape(-1, D_MODEL)
    lg = (hf @ w_rg).astype(jnp.float32) + b_rg.astype(jnp.float32)
    pg = jax.nn.softmax(lg, axis=-1)
    pg_top, g_idx = lax.top_k(pg, 1)
    le = ((hf @ w_re).astype(jnp.float32) + b_re.astype(jnp.float32)).reshape(-1, N_GROUPS, EXPERTS_PER_GROUP)
    le_sel = jnp.einsum('tge,tg->te', le, jax.nn.one_hot(g_idx[:, 0], N_GROUPS, dtype=jnp.float32))
    v2, e_idx = lax.top_k(le_sel, TOP_K_INNER)
    w2 = jax.nn.softmax(v2, axis=-1) * pg_top
    gid = g_idx * EXPERTS_PER_GROUP + e_idx
    comb = jnp.einsum('tk,tke->te', w2, jax.nn.one_hot(gid, N_EXPERTS, dtype=jnp.float32)).astype(h.dtype)
    hid = jax.nn.silu(jnp.einsum('td,edf->tef', hf, w_g)) * jnp.einsum('td,edf->tef', hf, w_u)
    y = jnp.einsum('tef,efd->td', hid * comb[:, :, None], w_d)
    return y.reshape(shape)


def channel_and_ple(x, p, g_ffn, w_rg, b_rg, w_re, b_re, w_g, w_u, w_d, g_ple, w_ple_gate, w_ple_proj):
    x = x + hier_moe(rms_norm(x, g_ffn), w_rg, b_rg, w_re, b_re, w_g, w_u, w_d)
    gate = jax.nn.sigmoid(rms_norm(x, g_ple) @ w_ple_gate)
    return x + gate * (p.astype(x.dtype) @ w_ple_proj)


def setup_inputs(seed: int = 0) -> dict:
    key = jax.random.key(seed)
    ks = jax.random.split(key, 32)
    f32 = jnp.float32

    def nrm(k, shape, scale):
        return jax.random.normal(k, shape, f32) * scale

    lc = min(N_PAST_CHUNKS * CHUNK, PAST_LEN)
    L = DEPTH
    return {
        'x_prompt': nrm(ks[0], (BATCH, SEQ, D_MODEL), 1.0),
        'x_sample': nrm(ks[1], (DEC_BATCH, DEC_SEQ, D_MODEL), 1.0),
        'cache_attn_k': nrm(ks[2], (L, DEC_BATCH, lc, ATTN_HEADS, HEAD_DIM), 1.0),
        'cache_attn_v': nrm(ks[3], (L, DEC_BATCH, lc, ATTN_HEADS, HEAD_DIM), 1.0),
        'p_prompt': nrm(ks[4], (L, BATCH, SEQ, PLE_DIM), 1.0),
        'p_sample': nrm(ks[5], (L, DEC_BATCH, DEC_SEQ, PLE_DIM), 1.0),
        'g_mix': 1.0 + nrm(ks[6], (L, D_MODEL), 0.05),
        'w_in': nrm(ks[7], (L, D_MODEL, IN_WIDTH), D_MODEL ** -0.5),
        'g_gmlp_v': 1.0 + nrm(ks[8], (L, GMLP_WIDTH), 0.05),
        'w_gmlp_s': nrm(ks[9], (L, GMLP_GROUPS, GMLP_CHUNK, GMLP_CHUNK), GMLP_CHUNK ** -0.5),
        'b_gmlp_s': 1.0 + nrm(ks[10], (L, GMLP_GROUPS, GMLP_CHUNK), 0.1),
        'g_q': 1.0 + nrm(ks[11], (L, HEAD_DIM), 0.05),
        'g_k': 1.0 + nrm(ks[12], (L, HEAD_DIM), 0.05),
        'rel_bias': nrm(ks[13], (L, ATTN_HEADS, 2 * MAX_REL + 1), 0.1),
        'w_branch': nrm(ks[14], (L, N_BRANCH, GMLP_WIDTH, D_MODEL), GMLP_WIDTH ** -0.5),
        'w_out': nrm(ks[15], (L, D_MODEL, D_MODEL), D_MODEL ** -0.5),
        'g_ffn': 1.0 + nrm(ks[16], (L, D_MODEL), 0.05),
        'w_router_group': nrm(ks[17], (L, D_MODEL, N_GROUPS), D_MODEL ** -0.5),
        'b_router_group': nrm(ks[18], (L, N_GROUPS), 0.01),
        'w_router_expert': nrm(ks[19], (L, D_MODEL, N_EXPERTS), D_MODEL ** -0.5),
        'b_router_expert': nrm(ks[20], (L, N_EXPERTS), 0.01),
        'w_exp_gate': nrm(ks[21], (L, N_EXPERTS, D_MODEL, D_EXPERT), D_MODEL ** -0.5),
        'w_exp_up': nrm(ks[22], (L, N_EXPERTS, D_MODEL, D_EXPERT), D_MODEL ** -0.5),
        'w_exp_down': nrm(ks[23], (L, N_EXPERTS, D_EXPERT, D_MODEL), D_EXPERT ** -0.5),
        'g_ple': 1.0 + nrm(ks[24], (L, D_MODEL), 0.05),
        'w_ple_gate': nrm(ks[25], (L, D_MODEL, D_MODEL), D_MODEL ** -0.5),
        'w_ple_proj': nrm(ks[26], (L, PLE_DIM, D_MODEL), PLE_DIM ** -0.5),
    }


def reference(x_prompt, x_sample, cache_attn_k, cache_attn_v, p_prompt, p_sample,
              g_mix, w_in, g_gmlp_v, w_gmlp_s, b_gmlp_s, g_q, g_k, rel_bias,
              w_branch, w_out, g_ffn, w_router_group, b_router_group,
              w_router_expert, b_router_expert, w_exp_gate, w_exp_up, w_exp_down,
              g_ple, w_ple_gate, w_ple_proj):
    xp, xs = x_prompt, x_sample
    kp_list, vp_list, ks_list, vs_list, gv_list = [], [], [], [], []
    for l in range(DEPTH):
        ua, va, q, k, v, ga, gb = mixer_inputs(xp, g_mix[l], w_in[l], g_q[l], g_k[l])
        u, vn = gmlp_uv(ua, va, g_gmlp_v[l])
        o_a = spatial_gate(u, vn, w_gmlp_s[l], b_gmlp_s[l])
        o_b = band_attention_prompt(q, k, v, rel_bias[l])
        xp = merge_branches(xp, o_a, o_b, ga, gb, w_branch[l], w_out[l])
        xp = channel_and_ple(xp, p_prompt[l], g_ffn[l], w_router_group[l], b_router_group[l],
                             w_router_expert[l], b_router_expert[l], w_exp_gate[l], w_exp_up[l],
                             w_exp_down[l], g_ple[l], w_ple_gate[l], w_ple_proj[l])
        keep = min(N_PAST_CHUNKS * CHUNK, k.shape[1])
        kp_list.append(k[:, k.shape[1] - keep:])
        vp_list.append(v[:, v.shape[1] - keep:])
        ua, va, q, k, v, ga, gb = mixer_inputs(xs, g_mix[l], w_in[l], g_q[l], g_k[l])
        u, vn = gmlp_uv(ua, va, g_gmlp_v[l])
        o_a = spatial_gate(u, vn, w_gmlp_s[l], b_gmlp_s[l])
        o_b = band_attention_sample(q, k, v, cache_attn_k[l], cache_attn_v[l], rel_bias[l])
        xs = merge_branches(xs, o_a, o_b, ga, gb, w_branch[l], w_out[l])
        xs = channel_and_ple(xs, p_sample[l], g_ffn[l], w_router_group[l], b_router_group[l],
                             w_router_expert[l], b_router_expert[l], w_exp_gate[l], w_exp_up[l],
                             w_exp_down[l], g_ple[l], w_ple_gate[l], w_ple_proj[l])
        ks_list.append(k)
        vs_list.append(v)
        gv_list.append(vn)
    return (xp, xs, jnp.stack(kp_list), jnp.stack(vp_list), jnp.stack(ks_list), jnp.stack(vs_list), jnp.stack(gv_list))
```

```python
import functools

import jax
import jax.numpy as jnp
from jax import lax
from jax.experimental import pallas as pl
from jax.experimental.pallas import tpu as pltpu

F32 = jnp.float32
BF16 = jnp.bfloat16

D_MODEL = 1024
CHUNK = 64
N_PAST_CHUNKS = 8
HEAD_DIM = 64
ATTN_HEADS = D_MODEL // HEAD_DIM
MAX_REL = 128
GMLP_CHUNK = 128
GMLP_GROUPS = 4
GMLP_GROUP_DIM = D_MODEL // GMLP_GROUPS
N_SECTIONS = 7
N_GROUPS = 4
EXPERTS_PER_GROUP = 8
N_EXPERTS = N_GROUPS * EXPERTS_PER_GROUP
D_EXPERT = D_MODEL // 4
PLE_DIM = 256
EPS = 1e-6
NEG = -1e30
PAST_LEN = 1024

LANES = 128
Q_BLOCK = 2 * CHUNK
K_WINDOW = Q_BLOCK + N_PAST_CHUNKS * CHUNK
N_SHIFT = N_PAST_CHUNKS * CHUNK // Q_BLOCK + 1
VMEM_LIMIT = 56 * 1024 * 1024


def _rms(x, gain):
    return x * lax.rsqrt(jnp.mean(x * x, axis=-1, keepdims=True) + EPS) * gain


def _act_dtype(precise):
    return F32 if precise else BF16


def _mm(a, b, precise, dims=None):
    dims = dims or (((a.ndim - 1,), (0,)), ((), ()))
    if precise:
        return lax.dot_general(a.astype(F32), b.astype(F32), dims, precision=lax.Precision.HIGHEST,
                               preferred_element_type=F32)
    return lax.dot_general(a.astype(BF16), b.astype(BF16), dims, preferred_element_type=F32)


_NT = (((1,), (1,)), ((), ()))


def _head_rms(z, gain, seg, segt2, precise):
    ssum = _mm(z * z, seg, precise)
    inv = lax.rsqrt(ssum * (1.0 / HEAD_DIM) + EPS)
    if precise:
        full = _mm(inv, segt2[:LANES], True)
    else:
        hi = inv.astype(BF16)
        lo = (inv - hi.astype(F32)).astype(BF16)
        full = _mm(jnp.concatenate([hi, lo], axis=1), segt2, False)
    return z * full * gain


def _inproj_kernel(x_ref, gmix_ref, w_ref, gv_ref, gq_ref, gk_ref, seg_ref, segt2_ref,
                   u_ref, vn_ref, q_ref, k_ref, v_ref, sga_ref, sgb_ref,
                   kst_ref, vst_ref, *maybe_vnst_ref, tiles_per_batch, precise):
    act = _act_dtype(precise)
    h = _rms(x_ref[...], gmix_ref[...]).astype(act)

    def section(s):
        return _mm(h, w_ref[:, s * D_MODEL:(s + 1) * D_MODEL], precise)

    is_state_tile = (pl.program_id(0) % tiles_per_batch) == tiles_per_batch - 1

    u_ref[...] = jax.nn.gelu(section(0)).astype(act)

    vn = _rms(jax.nn.gelu(section(1)), gv_ref[...])
    vn_ref[...] = vn.astype(act)
    if maybe_vnst_ref:
        maybe_vnst_ref[0][...] = vn

    qn = _head_rms(section(2), gq_ref[...], seg_ref[...], segt2_ref[...], precise)
    q_ref[...] = (qn * (HEAD_DIM ** -0.5)).astype(act)

    kn = _head_rms(section(3), gk_ref[...], seg_ref[...], segt2_ref[...], precise)
    k_ref[...] = kn.astype(act)

    @pl.when(is_state_tile)
    def _():
        kst_ref[...] = kn

    v = section(4)
    v_ref[...] = v.astype(act)

    @pl.when(is_state_tile)
    def _():
        vst_ref[...] = v

    sga_ref[...] = jax.nn.sigmoid(section(5)).astype(act)
    sgb_ref[...] = jax.nn.sigmoid(section(6)).astype(act)


def _inproj(x, gmix, w_in, gv, gq_t, gk_t, seg, segt2, *, tm, tiles_per_batch, emit_vn_state, precise):
    t = x.shape[0]
    n_tiles = t // tm
    n_state = n_tiles // tiles_per_batch
    row = lambda i: (i, 0)
    const = lambda i: (0, 0)
    state = lambda i: (i // tiles_per_batch, 0)
    act = jax.ShapeDtypeStruct((t, D_MODEL), _act_dtype(precise))
    st = jax.ShapeDtypeStruct((n_state * tm, D_MODEL), F32)
    out_shape = [act] * 7 + [st, st] + ([st] if emit_vn_state else [])
    out_specs = [pl.BlockSpec((tm, D_MODEL), row)] * 7 + [pl.BlockSpec((tm, D_MODEL), state)] * (
        3 if emit_vn_state else 2)
    return pl.pallas_call(
        functools.partial(_inproj_kernel, tiles_per_batch=tiles_per_batch, precise=precise),
        grid=(n_tiles,),
        in_specs=[
            pl.BlockSpec((tm, D_MODEL), row),
            pl.BlockSpec((1, D_MODEL), const),
            pl.BlockSpec((D_MODEL, N_SECTIONS * D_MODEL), const, pipeline_mode=pl.Buffered(1)),
            pl.BlockSpec((1, D_MODEL), const),
            pl.BlockSpec((1, D_MODEL), const),
            pl.BlockSpec((1, D_MODEL), const),
            pl.BlockSpec((D_MODEL, LANES), const),
            pl.BlockSpec((2 * LANES, D_MODEL), const),
        ],
        out_specs=out_specs,
        out_shape=out_shape,
        compiler_params=pltpu.CompilerParams(
            dimension_semantics=("arbitrary",), vmem_limit_bytes=VMEM_LIMIT),
        name="inproj",
    )(x, gmix, w_in, gv, gq_t, gk_t, seg, segt2)


def _softmax_pv(s, v_masked):
    m = jnp.max(s, axis=-1, keepdims=True)
    p = jnp.exp(s - m)
    l = jnp.sum(p, axis=-1, keepdims=True)
    o = jnp.dot(p.astype(BF16), v_masked, preferred_element_type=F32)
    return o * (1.0 / l)


def _attn_prompt_kernel(q_ref, k_ref, v_ref, bias_ref, o_ref, *, seq):
    lane = lax.broadcasted_iota(jnp.int32, (1, LANES), 1)
    low_head = lane < HEAD_DIM
    zero = jnp.zeros((), BF16)

    def block(qb, carry):
        q_start = pl.multiple_of(qb * Q_BLOCK, Q_BLOCK)
        shift = jnp.maximum(N_SHIFT - 1 - qb, 0)
        k_start = pl.multiple_of((qb + shift) * Q_BLOCK - (K_WINDOW - Q_BLOCK), Q_BLOCK)
        q = q_ref[pl.ds(q_start, Q_BLOCK), :]
        k = k_ref[pl.ds(k_start, K_WINDOW), :]
        v = v_ref[pl.ds(k_start, K_WINDOW), :]
        out = None
        for e in range(2):
            mine = low_head if e == 0 else jnp.logical_not(low_head)
            qe = jnp.where(mine, q, zero)
            ve = jnp.where(mine, v, zero)
            s = lax.dot_general(qe, k, _NT, preferred_element_type=F32)
            s = s + bias_ref[e, shift]
            o = _softmax_pv(s, ve)
            out = o if out is None else out + o
        o_ref[pl.ds(q_start, Q_BLOCK), :] = out.astype(BF16)
        return carry

    lax.fori_loop(0, seq // Q_BLOCK, block, 0)


def _attn_prompt(q, k, v, bias_tab, *, batch, seq):
    q3, k3, v3 = (a.reshape(batch, seq, D_MODEL) for a in (q, k, v))
    head_pair = lambda hp, b: (b, 0, hp)
    spec = pl.BlockSpec((None, seq, LANES), head_pair)
    out = pl.pallas_call(
        functools.partial(_attn_prompt_kernel, seq=seq),
        grid=(ATTN_HEADS // 2, batch),
        in_specs=[spec, spec, spec,
                  pl.BlockSpec((2, N_SHIFT, Q_BLOCK, K_WINDOW), lambda hp, b: (hp, 0, 0, 0))],
        out_specs=spec,
        out_shape=jax.ShapeDtypeStruct((batch, seq, D_MODEL), BF16),
        compiler_params=pltpu.CompilerParams(
            dimension_semantics=("arbitrary", "arbitrary"), vmem_limit_bytes=VMEM_LIMIT),
        name="attn_prompt",
    )(q3, k3, v3, bias_tab)
    return out.reshape(batch * seq, D_MODEL)


def _prompt_bias_table(rel_bias):
    span = K_WINDOW + (N_SHIFT - 1) * Q_BLOCK
    i = jnp.arange(Q_BLOCK, dtype=jnp.int32)[:, None]
    jj = jnp.arange(span, dtype=jnp.int32)[None, :]
    rel = jnp.clip(i + N_PAST_CHUNKS * CHUNK - jj, -MAX_REL, MAX_REL) + MAX_REL
    ok = (jj // CHUNK >= i // CHUNK) & (jj // CHUNK <= i // CHUNK + N_PAST_CHUNKS)
    ext = jnp.where(ok[None], rel_bias[:, rel].astype(F32), NEG)
    return jnp.stack([ext[:, :, s * Q_BLOCK:s * Q_BLOCK + K_WINDOW] for s in range(N_SHIFT)], axis=1)


def _attn_sample_kernel(q_ref, k_ref, v_ref, ck_ref, cv_ref, bc_ref, bn_ref, o_ref):
    lane = lax.broadcasted_iota(jnp.int32, (1, LANES), 1)
    low_head = lane < HEAD_DIM
    for hp in range(ATTN_HEADS // 2):
        cols = slice(hp * LANES, (hp + 1) * LANES)
        q = q_ref[:, cols]
        kn = k_ref[:, cols]
        vn = v_ref[:, cols]
        kc = ck_ref[:, cols]
        vc = cv_ref[:, cols]
        out = None
        for e in range(2):
            mine = low_head if e == 0 else jnp.logical_not(low_head)
            qe = jnp.where(mine, q, 0.0)
            s_c = _mm(qe, kc, True, _NT) + bc_ref[2 * hp + e]
            s_n = _mm(qe, kn, True, _NT) + bn_ref[2 * hp + e]
            m = jnp.maximum(jnp.max(s_c, axis=-1, keepdims=True), jnp.max(s_n, axis=-1, keepdims=True))
            p_c = jnp.exp(s_c - m)
            p_n = jnp.exp(s_n - m)
            l = jnp.sum(p_c, axis=-1, keepdims=True) + jnp.sum(p_n, axis=-1, keepdims=True)
            o = _mm(p_c, jnp.where(mine, vc, 0.0), True) + _mm(p_n, jnp.where(mine, vn, 0.0), True)
            o = o * (1.0 / l)
            out = o if out is None else out + o
        o_ref[:, cols] = out


def _attn_sample(q, k, v, cache_k, cache_v, bias_c, bias_n, *, batch, rows):
    lc = cache_k.shape[1]
    new = pl.BlockSpec((rows, D_MODEL), lambda b: (b, 0))
    cache = pl.BlockSpec((None, lc, D_MODEL), lambda b: (b, 0, 0))
    return pl.pallas_call(
        _attn_sample_kernel,
        grid=(batch,),
        in_specs=[new, new, new, cache, cache,
                  pl.BlockSpec((ATTN_HEADS, rows, lc), lambda b: (0, 0, 0)),
                  pl.BlockSpec((ATTN_HEADS, rows, rows), lambda b: (0, 0, 0))],
        out_specs=new,
        out_shape=jax.ShapeDtypeStruct((batch * rows, D_MODEL), F32),
        compiler_params=pltpu.CompilerParams(
            dimension_semantics=("arbitrary",), vmem_limit_bytes=VMEM_LIMIT),
        name="attn_sample",
    )(q, k, v, cache_k, cache_v, bias_c, bias_n)


def _sample_bias_tables(rel_bias, rows, lc, past_len):
    q_pos = past_len + jnp.arange(rows, dtype=jnp.int32)
    c_pos = past_len - lc + jnp.arange(lc, dtype=jnp.int32)

    def table(k_pos):
        rel = jnp.clip(q_pos[:, None] - k_pos[None, :], -MAX_REL, MAX_REL) + MAX_REL
        ok = ((k_pos[None, :] >= 0) & (k_pos[None, :] // CHUNK <= q_pos[:, None] // CHUNK)
              & (k_pos[None, :] // CHUNK >= q_pos[:, None] // CHUNK - N_PAST_CHUNKS))
        return jnp.where(ok[None], rel_bias[:, rel].astype(F32), NEG)

    return table(c_pos), table(q_pos)


def _merge_kernel(x_ref, u_ref, vn_ref, sga_ref, sgb_ref, ob_ref, ws_ref, bs_ref, wb_ref, wout_ref,
                  x1_ref, oa_ref, *, tm, precise):
    act = _act_dtype(precise)
    for c in range(tm // GMLP_CHUNK):
        rows = slice(c * GMLP_CHUNK, (c + 1) * GMLP_CHUNK)
        for g in range(GMLP_GROUPS):
            cols = slice(g * GMLP_GROUP_DIM, (g + 1) * GMLP_GROUP_DIM)
            mix = _mm(ws_ref[g], vn_ref[rows, cols], precise) + bs_ref[:, cols]
            oa_ref[rows, cols] = (u_ref[rows, cols].astype(F32) * mix).astype(act)
    b_a = _mm(oa_ref[...], wb_ref[0], precise)
    b_b = _mm(ob_ref[...], wb_ref[1], precise)
    m = sga_ref[...].astype(F32) * b_a + sgb_ref[...].astype(F32) * b_b
    x1_ref[...] = x_ref[...] + _mm(m, wout_ref[...], precise)


def _merge(x, u, vn, sga, sgb, ob, ws, bs_full, wb, wout, *, tm, precise):
    t = x.shape[0]
    row = pl.BlockSpec((tm, D_MODEL), lambda i: (i, 0))
    return pl.pallas_call(
        functools.partial(_merge_kernel, tm=tm, precise=precise),
        grid=(t // tm,),
        in_specs=[row, row, row, row, row, row,
                  pl.BlockSpec((GMLP_GROUPS, GMLP_CHUNK, GMLP_CHUNK), lambda i: (0, 0, 0)),
                  pl.BlockSpec((GMLP_CHUNK, D_MODEL), lambda i: (0, 0)),
                  pl.BlockSpec((2, D_MODEL, D_MODEL), lambda i: (0, 0, 0)),
                  pl.BlockSpec((D_MODEL, D_MODEL), lambda i: (0, 0))],
        out_specs=row,
        out_shape=jax.ShapeDtypeStruct((t, D_MODEL), F32),
        scratch_shapes=[pltpu.VMEM((tm, D_MODEL), _act_dtype(precise))],
        compiler_params=pltpu.CompilerParams(
            dimension_semantics=("arbitrary",), vmem_limit_bytes=VMEM_LIMIT),
        name="merge",
    )(x, u, vn, sga, sgb, ob, ws, bs_full, wb, wout)


def _first_index_of_max(vals, lane):
    top = jnp.max(vals, axis=-1, keepdims=True)
    idx = jnp.min(jnp.where(vals == top, lane, float(LANES)), axis=-1, keepdims=True)
    return top, idx


def _channel_kernel(x1_ref, p_ref, gffn_ref, wrg_ref, brg_ref, wre_ref, bre_ref,
                    wg_ref, wu_ref, wd_ref, gple_ref, wpg_ref, wpe_ref,
                    out_ref, h_ref, comb_ref, acc_ref, *, precise_router):
    g = pl.program_id(1)
    lane_i = lax.broadcasted_iota(jnp.int32, (1, LANES), 1)
    lane = lane_i.astype(F32)

    @pl.when(g == 0)
    def _():
        h = _rms(x1_ref[...], gffn_ref[...])
        h_ref[...] = h.astype(BF16)
        lg = _mm(h, wrg_ref[...], precise_router) + brg_ref[...]
        top_g, g_idx = _first_index_of_max(lg, lane)
        pg_top = 1.0 / jnp.sum(jnp.exp(lg - top_g), axis=-1, keepdims=True)
        le = _mm(h, wre_ref[...], precise_router) + bre_ref[...]
        group_of_lane = (lane_i // EXPERTS_PER_GROUP).astype(F32)
        le = jnp.where(group_of_lane == g_idx, le, -jnp.inf)
        t1, i1 = _first_index_of_max(le, lane)
        t2, i2 = _first_index_of_max(jnp.where(lane == i1, -jnp.inf, le), lane)
        e2 = jnp.exp(t2 - t1)
        den = 1.0 + e2
        comb_ref[...] = jnp.where(lane == i1, pg_top * (1.0 / den),
                                  jnp.where(lane == i2, pg_top * (e2 / den), 0.0))

    h = h_ref[...]
    comb = comb_ref[...]
    hids = []
    for j in range(EXPERTS_PER_GROUP):
        w_tok = jnp.sum(jnp.where(lane_i == g * EXPERTS_PER_GROUP + j, comb, 0.0), axis=-1, keepdims=True)
        a = jnp.dot(h, wg_ref[j], preferred_element_type=F32)
        b = jnp.dot(h, wu_ref[j], preferred_element_type=F32)
        hids.append((jax.nn.silu(a) * b * w_tok).astype(BF16))
    hid = jnp.concatenate(hids, axis=1)
    y = jnp.dot(hid, wd_ref[...].reshape(EXPERTS_PER_GROUP * D_EXPERT, D_MODEL), preferred_element_type=F32)

    @pl.when(g == 0)
    def _():
        acc_ref[...] = y

    @pl.when(g > 0)
    def _():
        acc_ref[...] += y

    @pl.when(g == N_GROUPS - 1)
    def _():
        x2 = x1_ref[...] + acc_ref[...]
        gate = jax.nn.sigmoid(jnp.dot(_rms(x2, gple_ref[...]).astype(BF16), wpg_ref[...],
                                      preferred_element_type=F32))
        pe = jnp.dot(p_ref[...].astype(BF16), wpe_ref[...], preferred_element_type=F32)
        out_ref[...] = x2 + gate * pe


def _channel(x1, p, gffn, wrg, brg, wre, bre, wg, wu, wd, gple, wpg, wpe, *, tm, precise_router):
    t = x1.shape[0]
    row = lambda i, g: (i, 0)
    const2 = lambda i, g: (0, 0)
    group = lambda i, g: (g, 0, 0)
    return pl.pallas_call(
        functools.partial(_channel_kernel, precise_router=precise_router),
        grid=(t // tm, N_GROUPS),
        in_specs=[
            pl.BlockSpec((tm, D_MODEL), row),
            pl.BlockSpec((tm, PLE_DIM), row),
            pl.BlockSpec((1, D_MODEL), const2),
            pl.BlockSpec((D_MODEL, LANES), const2),
            pl.BlockSpec((1, LANES), const2),
            pl.BlockSpec((D_MODEL, LANES), const2),
            pl.BlockSpec((1, LANES), const2),
            pl.BlockSpec((EXPERTS_PER_GROUP, D_MODEL, D_EXPERT), group),
            pl.BlockSpec((EXPERTS_PER_GROUP, D_MODEL, D_EXPERT), group),
            pl.BlockSpec((EXPERTS_PER_GROUP, D_EXPERT, D_MODEL), group),
            pl.BlockSpec((1, D_MODEL), const2),
            pl.BlockSpec((D_MODEL, D_MODEL), const2),
            pl.BlockSpec((PLE_DIM, D_MODEL), const2),
        ],
        out_specs=pl.BlockSpec((tm, D_MODEL), row),
        out_shape=jax.ShapeDtypeStruct((t, D_MODEL), F32),
        scratch_shapes=[pltpu.VMEM((tm, D_MODEL), BF16),
                        pltpu.VMEM((tm, LANES), F32),
                        pltpu.VMEM((tm, D_MODEL), F32)],
        compiler_params=pltpu.CompilerParams(
            dimension_semantics=("arbitrary", "arbitrary"), vmem_limit_bytes=VMEM_LIMIT),
        name="channel",
    )(x1, p, gffn, wrg, brg, wre, bre, wg, wu, wd, gple, wpg, wpe)


def _pad_lanes(a, fill):
    return jnp.pad(a, ((0, 0), (0, LANES - a.shape[1])), constant_values=fill)


def kernel(x_prompt, x_sample, cache_attn_k, cache_attn_v, p_prompt, p_sample, g_mix, w_in, g_gmlp_v, w_gmlp_s, b_gmlp_s, g_q, g_k, rel_bias, w_branch, w_out, g_ffn, w_router_group, b_router_group, w_router_expert, b_router_expert, w_exp_gate, w_exp_up, w_exp_down, g_ple, w_ple_gate, w_ple_proj):
    depth = g_mix.shape[0]
    batch, seq, _ = x_prompt.shape
    dec_batch, dec_seq, _ = x_sample.shape
    lc = cache_attn_k.shape[2]
    keep = min(N_PAST_CHUNKS * CHUNK, seq)
    tm = 512
    assert depth == 1 and keep == tm and seq % tm == 0 and seq % Q_BLOCK == 0
    assert dec_seq <= GMLP_CHUNK and GMLP_CHUNK % dec_seq == 0 and dec_batch * dec_seq == GMLP_CHUNK
    l = 0
    t_s = dec_batch * dec_seq

    row = lambda a: a.reshape(1, -1).astype(F32)
    w_in_b = w_in[l].astype(BF16)
    gq_t = row(jnp.tile(g_q[l], ATTN_HEADS))
    gk_t = row(jnp.tile(g_k[l], ATTN_HEADS))
    head_of_lane = jnp.arange(D_MODEL, dtype=jnp.int32) // HEAD_DIM
    seg = (head_of_lane[:, None] == jnp.arange(LANES, dtype=jnp.int32)[None, :]).astype(BF16)
    segt2 = jnp.concatenate([seg.T, seg.T], axis=0)
    ws_tril = jnp.tril(w_gmlp_s[l])
    ws_p = ws_tril.astype(BF16)
    bs_p = jnp.repeat(b_gmlp_s[l].T, GMLP_GROUP_DIM, axis=1).astype(F32)
    eye = jnp.eye(dec_batch, dtype=F32)
    ws_s = jnp.einsum('ab,gts->gatbs', eye, ws_tril[:, :dec_seq, :dec_seq]).reshape(
        GMLP_GROUPS, t_s, t_s)
    bs_s = jnp.tile(jnp.repeat(b_gmlp_s[l][:, :dec_seq].T, GMLP_GROUP_DIM, axis=1), (dec_batch, 1)).astype(F32)
    wb_b = w_branch[l].astype(BF16)
    wout_b = w_out[l].astype(BF16)
    wrg = _pad_lanes(w_router_group[l], 0.0)
    brg = _pad_lanes(row(b_router_group[l]), NEG)
    wre = _pad_lanes(w_router_expert[l], 0.0)
    bre = _pad_lanes(row(b_router_expert[l]), 0.0)
    wg_b = w_exp_gate[l].astype(BF16)
    wu_b = w_exp_up[l].astype(BF16)
    wd_b = w_exp_down[l].astype(BF16)
    wpg_b = w_ple_gate[l].astype(BF16)
    wpe_b = w_ple_proj[l].astype(BF16)
    bias_prompt = _prompt_bias_table(rel_bias[l])
    bias_c, bias_n = _sample_bias_tables(rel_bias[l], dec_seq, lc, PAST_LEN)

    def channel(x1, p, tile, precise_router):
        cast = (lambda a: a) if precise_router else (lambda a: a.astype(BF16))
        return _channel(x1, p, row(g_ffn[l]), cast(wrg), brg, cast(wre), bre, wg_b, wu_b, wd_b,
                        row(g_ple[l]), wpg_b, wpe_b, tm=tile, precise_router=precise_router)

    xp = x_prompt.reshape(batch * seq, D_MODEL)
    u, vn, q, k, v, sga, sgb, kst, vst = _inproj(
        xp, row(g_mix[l]), w_in_b, row(g_gmlp_v[l]), gq_t, gk_t, seg, segt2,
        tm=tm, tiles_per_batch=seq // tm, emit_vn_state=False, precise=False)
    ob = _attn_prompt(q, k, v, bias_prompt, batch=batch, seq=seq)
    x1 = _merge(xp, u, vn, sga, sgb, ob, ws_p, bs_p, wb_b, wout_b, tm=tm, precise=False)
    yp = channel(x1, p_prompt[l].reshape(batch * seq, PLE_DIM), tm, False)

    xs = x_sample.reshape(t_s, D_MODEL)
    u, vn, q, k, v, sga, sgb, kss, vss, gvs = _inproj(
        xs, row(g_mix[l]), w_in[l], row(g_gmlp_v[l]), gq_t, gk_t, seg, segt2,
        tm=t_s, tiles_per_batch=1, emit_vn_state=True, precise=True)
    ob = _attn_sample(q, k, v, cache_attn_k[l].reshape(dec_batch, lc, D_MODEL),
                      cache_attn_v[l].reshape(dec_batch, lc, D_MODEL), bias_c, bias_n,
                      batch=dec_batch, rows=dec_seq)
    x1 = _merge(xs, u, vn, sga, sgb, ob, ws_s, bs_s, w_branch[l], w_out[l], tm=t_s, precise=True)
    ys = channel(x1, p_sample[l].reshape(t_s, PLE_DIM), t_s, True)

    return (yp.reshape(batch, seq, D_MODEL),
            ys.reshape(dec_batch, dec_seq, D_MODEL),
            kst.reshape(1, batch, keep, ATTN_HEADS, HEAD_DIM),
            vst.reshape(1, batch, keep, ATTN_HEADS, HEAD_DIM),
            kss.reshape(1, dec_batch, dec_seq, ATTN_HEADS, HEAD_DIM),
            vss.reshape(1, dec_batch, dec_seq, ATTN_HEADS, HEAD_DIM),
            gvs.reshape(1, dec_batch, dec_seq, D_MODEL))
```

```python
import functools

import jax
import jax.numpy as jnp
import numpy as np
from jax import lax
from jax.experimental import pallas as pl
from jax.experimental.pallas import tpu as pltpu

F32 = jnp.float32
BF16 = jnp.bfloat16

D_MODEL = 1024
CHUNK = 64
N_PAST_CHUNKS = 8
HEAD_DIM = 64
ATTN_HEADS = D_MODEL // HEAD_DIM
MAX_REL = 128
GMLP_CHUNK = 128
GMLP_GROUPS = 4
GMLP_GROUP_DIM = D_MODEL // GMLP_GROUPS
N_SECTIONS = 7
N_GROUPS = 4
EXPERTS_PER_GROUP = 8
N_EXPERTS = N_GROUPS * EXPERTS_PER_GROUP
D_EXPERT = D_MODEL // 4
PLE_DIM = 256
EPS = 1e-6
NEG = -1e30
PAST_LEN = 1024

LANES = 128
Q_BLOCK = 2 * CHUNK
K_WINDOW = Q_BLOCK + N_PAST_CHUNKS * CHUNK
N_SHIFT = N_PAST_CHUNKS * CHUNK // Q_BLOCK + 1
ATTN_UNROLL = 4
VMEM_LIMIT = 56 * 1024 * 1024


def _rms(x, gain):
    return x * lax.rsqrt(jnp.mean(x * x, axis=-1, keepdims=True) + EPS) * gain


def _act_dtype(precise):
    return F32 if precise else BF16


def _mm(a, b, precise, dims=None):
    dims = dims or (((a.ndim - 1,), (0,)), ((), ()))
    if precise:
        return lax.dot_general(a.astype(F32), b.astype(F32), dims, precision=lax.Precision.HIGHEST,
                               preferred_element_type=F32)
    return lax.dot_general(a.astype(BF16), b.astype(BF16), dims, preferred_element_type=F32)


_NT = (((1,), (1,)), ((), ()))


def _head_rms(z, gain, seg, segt2, precise):
    ssum = _mm(z * z, seg, precise)
    inv = lax.rsqrt(ssum * (1.0 / HEAD_DIM) + EPS)
    if precise:
        full = _mm(inv, segt2[:LANES], True)
    else:
        hi = inv.astype(BF16)
        lo = (inv - hi.astype(F32)).astype(BF16)
        full = _mm(jnp.concatenate([hi, lo], axis=1), segt2, False)
    return z * full * gain


def _inproj_kernel(x_ref, gmix_ref, w_ref, gv_ref, gq_ref, gk_ref, seg_ref, segt2_ref,
                   u_ref, vn_ref, q_ref, k_ref, v_ref, sga_ref, sgb_ref,
                   kst_ref, vst_ref, *maybe_vnst_ref, tiles_per_batch, precise):
    act = _act_dtype(precise)
    h = _rms(x_ref[...], gmix_ref[...]).astype(act)

    def section(s):
        return _mm(h, w_ref[:, s * D_MODEL:(s + 1) * D_MODEL], precise)

    is_state_tile = (pl.program_id(0) % tiles_per_batch) == tiles_per_batch - 1

    u_ref[...] = jax.nn.gelu(section(0)).astype(act)

    vn = _rms(jax.nn.gelu(section(1)), gv_ref[...])
    vn_ref[...] = vn.astype(act)
    if maybe_vnst_ref:
        maybe_vnst_ref[0][...] = vn

    qn = _head_rms(section(2), gq_ref[...], seg_ref[...], segt2_ref[...], precise)
    q_ref[...] = (qn * (HEAD_DIM ** -0.5)).astype(act)

    kn = _head_rms(section(3), gk_ref[...], seg_ref[...], segt2_ref[...], precise)
    k_ref[...] = kn.astype(act)

    @pl.when(is_state_tile)
    def _():
        kst_ref[...] = kn

    v = section(4)
    v_ref[...] = v.astype(act)

    @pl.when(is_state_tile)
    def _():
        vst_ref[...] = v

    sga_ref[...] = jax.nn.sigmoid(section(5)).astype(act)
    sgb_ref[...] = jax.nn.sigmoid(section(6)).astype(act)


def _inproj(x, gmix, w_in, gv, gq_t, gk_t, seg, segt2, *, tm, tiles_per_batch, emit_vn_state, precise):
    t = x.shape[0]
    n_tiles = t // tm
    n_state = n_tiles // tiles_per_batch
    row = lambda i: (i, 0)
    const = lambda i: (0, 0)
    state = lambda i: (i // tiles_per_batch, 0)
    act = jax.ShapeDtypeStruct((t, D_MODEL), _act_dtype(precise))
    st = jax.ShapeDtypeStruct((n_state * tm, D_MODEL), F32)
    out_shape = [act] * 7 + [st, st] + ([st] if emit_vn_state else [])
    out_specs = [pl.BlockSpec((tm, D_MODEL), row)] * 7 + [pl.BlockSpec((tm, D_MODEL), state)] * (
        3 if emit_vn_state else 2)
    return pl.pallas_call(
        functools.partial(_inproj_kernel, tiles_per_batch=tiles_per_batch, precise=precise),
        grid=(n_tiles,),
        in_specs=[
            pl.BlockSpec((tm, D_MODEL), row),
            pl.BlockSpec((1, D_MODEL), const),
            pl.BlockSpec((D_MODEL, N_SECTIONS * D_MODEL), const, pipeline_mode=pl.Buffered(1)),
            pl.BlockSpec((1, D_MODEL), const),
            pl.BlockSpec((1, D_MODEL), const),
            pl.BlockSpec((1, D_MODEL), const),
            pl.BlockSpec((D_MODEL, LANES), const),
            pl.BlockSpec((2 * LANES, D_MODEL), const),
        ],
        out_specs=out_specs,
        out_shape=out_shape,
        compiler_params=pltpu.CompilerParams(
            dimension_semantics=("arbitrary",), vmem_limit_bytes=VMEM_LIMIT),
        name="inproj",
    )(x, gmix, w_in, gv, gq_t, gk_t, seg, segt2)


def _attn_prompt_kernel(q_ref, k_ref, v_ref, bias_ref, o_ref, *, seq):
    lane = lax.broadcasted_iota(jnp.int32, (1, LANES), 1)
    low_head = lane < HEAD_DIM
    zero = jnp.zeros((), BF16)

    def block(qb, carry):
        q_start = pl.multiple_of(qb * Q_BLOCK, Q_BLOCK)
        shift = jnp.maximum(N_SHIFT - 1 - qb, 0)
        k_start = pl.multiple_of((qb + shift) * Q_BLOCK - (K_WINDOW - Q_BLOCK), Q_BLOCK)
        q = q_ref[pl.ds(q_start, Q_BLOCK), :]
        k = k_ref[pl.ds(k_start, K_WINDOW), :]
        v = v_ref[pl.ds(k_start, K_WINDOW), :]
        q2 = jnp.concatenate([jnp.where(low_head, q, zero), jnp.where(low_head, zero, q)], axis=0)
        s = lax.dot_general(q2, k, _NT, preferred_element_type=F32) + bias_ref[shift]
        m = jnp.max(s, axis=-1, keepdims=True)
        p = jnp.exp(s - m)
        l = jnp.sum(p, axis=-1, keepdims=True)
        o = jnp.dot(p.astype(BF16), v, preferred_element_type=F32) * (1.0 / l)
        o_ref[pl.ds(q_start, Q_BLOCK), :] = jnp.where(low_head, o[:Q_BLOCK], o[Q_BLOCK:]).astype(BF16)
        return carry

    lax.fori_loop(0, seq // Q_BLOCK, block, 0, unroll=ATTN_UNROLL)


def _attn_prompt(q, k, v, bias_tab, *, batch, seq):
    q3, k3, v3 = (a.reshape(batch, seq, D_MODEL) for a in (q, k, v))
    head_pair = lambda hp, b: (b, 0, hp)
    spec = pl.BlockSpec((None, seq, LANES), head_pair)
    out = pl.pallas_call(
        functools.partial(_attn_prompt_kernel, seq=seq),
        grid=(ATTN_HEADS // 2, batch),
        in_specs=[spec, spec, spec,
                  pl.BlockSpec((None, N_SHIFT, 2 * Q_BLOCK, K_WINDOW), lambda hp, b: (hp, 0, 0, 0))],
        out_specs=spec,
        out_shape=jax.ShapeDtypeStruct((batch, seq, D_MODEL), BF16),
        compiler_params=pltpu.CompilerParams(
            dimension_semantics=("arbitrary", "arbitrary"), vmem_limit_bytes=VMEM_LIMIT),
        name="attn_prompt",
    )(q3, k3, v3, bias_tab)
    return out.reshape(batch * seq, D_MODEL)


def _rel_bias_rows(rel_bias, q0, k0, rows, ncols):
    dmax = q0 - k0 + rows - 1
    total = rows + ncols - 1
    dmin = dmax - total + 1
    n_lo = min(max(-MAX_REL - dmin, 0), total)
    n_hi = min(max(dmax - MAX_REL, 0), total)
    mid0 = dmin + n_lo + MAX_REL
    asc = jnp.concatenate([jnp.repeat(rel_bias[:, :1], n_lo, axis=1),
                           rel_bias[:, mid0:mid0 + total - n_lo - n_hi],
                           jnp.repeat(rel_bias[:, -1:], n_hi, axis=1)], axis=1).astype(F32)
    desc = asc[:, ::-1]
    return jnp.stack([desc[:, rows - 1 - i:rows - 1 - i + ncols] for i in range(rows)], axis=1)


def _band_ok(q_pos, k_pos):
    q_pos, k_pos = q_pos[:, None], k_pos[None, :]
    return (k_pos >= 0) & (k_pos // CHUNK <= q_pos // CHUNK) & (k_pos // CHUNK >= q_pos // CHUNK - N_PAST_CHUNKS)


def _prompt_bias_table(rel_bias):
    back = N_PAST_CHUNKS * CHUNK
    span = K_WINDOW + (N_SHIFT - 1) * Q_BLOCK
    ok = _band_ok(back + np.arange(Q_BLOCK), np.arange(span))
    ext = jnp.where(ok[None], _rel_bias_rows(rel_bias, back, 0, Q_BLOCK, span), NEG)
    tab = jnp.stack([ext[:, :, s * Q_BLOCK:s * Q_BLOCK + K_WINDOW] for s in range(N_SHIFT)], axis=1)
    tab = tab.reshape(ATTN_HEADS // 2, 2, N_SHIFT, Q_BLOCK, K_WINDOW).transpose(0, 2, 1, 3, 4)
    return tab.reshape(ATTN_HEADS // 2, N_SHIFT, 2 * Q_BLOCK, K_WINDOW)


def _attn_sample_kernel(q_ref, k_ref, v_ref, ck_ref, cv_ref, bc_ref, bn_ref, o_ref):
    lane = lax.broadcasted_iota(jnp.int32, (1, LANES), 1)
    low_head = lane < HEAD_DIM
    for hp in range(ATTN_HEADS // 2):
        cols = slice(hp * LANES, (hp + 1) * LANES)
        q = q_ref[:, cols]
        kn = k_ref[:, cols]
        vn = v_ref[:, cols]
        kc = ck_ref[:, cols]
        vc = cv_ref[:, cols]
        out = None
        for e in range(2):
            mine = low_head if e == 0 else jnp.logical_not(low_head)
            qe = jnp.where(mine, q, 0.0)
            s_c = _mm(qe, kc, True, _NT) + bc_ref[2 * hp + e]
            s_n = _mm(qe, kn, True, _NT) + bn_ref[2 * hp + e]
            m = jnp.maximum(jnp.max(s_c, axis=-1, keepdims=True), jnp.max(s_n, axis=-1, keepdims=True))
            p_c = jnp.exp(s_c - m)
            p_n = jnp.exp(s_n - m)
            l = jnp.sum(p_c, axis=-1, keepdims=True) + jnp.sum(p_n, axis=-1, keepdims=True)
            o = _mm(p_c, jnp.where(mine, vc, 0.0), True) + _mm(p_n, jnp.where(mine, vn, 0.0), True)
            o = o * (1.0 / l)
            out = o if out is None else out + o
        o_ref[:, cols] = out


def _attn_sample(q, k, v, cache_k, cache_v, bias_c, bias_n, *, batch, rows):
    lc = cache_k.shape[1]
    new = pl.BlockSpec((rows, D_MODEL), lambda b: (b, 0))
    cache = pl.BlockSpec((None, lc, D_MODEL), lambda b: (b, 0, 0))
    return pl.pallas_call(
        _attn_sample_kernel,
        grid=(batch,),
        in_specs=[new, new, new, cache, cache,
                  pl.BlockSpec((ATTN_HEADS, rows, lc), lambda b: (0, 0, 0)),
                  pl.BlockSpec((ATTN_HEADS, rows, rows), lambda b: (0, 0, 0))],
        out_specs=new,
        out_shape=jax.ShapeDtypeStruct((batch * rows, D_MODEL), F32),
        compiler_params=pltpu.CompilerParams(
            dimension_semantics=("arbitrary",), vmem_limit_bytes=VMEM_LIMIT),
        name="attn_sample",
    )(q, k, v, cache_k, cache_v, bias_c, bias_n)


def _sample_bias_tables(rel_bias, rows, lc, past_len):
    q_pos = past_len + np.arange(rows)
    c_pos = past_len - lc + np.arange(lc)
    bias_c = jnp.where(_band_ok(q_pos, c_pos)[None], _rel_bias_rows(rel_bias, past_len, past_len - lc, rows, lc), NEG)
    bias_n = jnp.where(_band_ok(q_pos, q_pos)[None], _rel_bias_rows(rel_bias, past_len, past_len, rows, rows), NEG)
    return bias_c, bias_n


def _merge_kernel(x_ref, u_ref, vn_ref, sga_ref, sgb_ref, ob_ref, ws_ref, bs_ref, wb_ref, wout_ref,
                  x1_ref, oa_ref, *, tm, precise):
    act = _act_dtype(precise)
    for c in range(tm // GMLP_CHUNK):
        rows = slice(c * GMLP_CHUNK, (c + 1) * GMLP_CHUNK)
        for g in range(GMLP_GROUPS):
            cols = slice(g * GMLP_GROUP_DIM, (g + 1) * GMLP_GROUP_DIM)
            mix = _mm(ws_ref[g], vn_ref[rows, cols], precise) + bs_ref[:, cols]
            oa_ref[rows, cols] = (u_ref[rows, cols].astype(F32) * mix).astype(act)
    b_a = _mm(oa_ref[...], wb_ref[0], precise)
    b_b = _mm(ob_ref[...], wb_ref[1], precise)
    m = sga_ref[...].astype(F32) * b_a + sgb_ref[...].astype(F32) * b_b
    x1_ref[...] = x_ref[...] + _mm(m, wout_ref[...], precise)


def _merge(x, u, vn, sga, sgb, ob, ws, bs_full, wb, wout, *, tm, precise):
    t = x.shape[0]
    row = pl.BlockSpec((tm, D_MODEL), lambda i: (i, 0))
    return pl.pallas_call(
        functools.partial(_merge_kernel, tm=tm, precise=precise),
        grid=(t // tm,),
        in_specs=[row, row, row, row, row, row,
                  pl.BlockSpec((GMLP_GROUPS, GMLP_CHUNK, GMLP_CHUNK), lambda i: (0, 0, 0)),
                  pl.BlockSpec((GMLP_CHUNK, D_MODEL), lambda i: (0, 0)),
                  pl.BlockSpec((2, D_MODEL, D_MODEL), lambda i: (0, 0, 0)),
                  pl.BlockSpec((D_MODEL, D_MODEL), lambda i: (0, 0))],
        out_specs=row,
        out_shape=jax.ShapeDtypeStruct((t, D_MODEL), F32),
        scratch_shapes=[pltpu.VMEM((tm, D_MODEL), _act_dtype(precise))],
        compiler_params=pltpu.CompilerParams(
            dimension_semantics=("arbitrary",), vmem_limit_bytes=VMEM_LIMIT),
        name="merge",
    )(x, u, vn, sga, sgb, ob, ws, bs_full, wb, wout)


def _first_index_of_max(vals, lane):
    top = jnp.max(vals, axis=-1, keepdims=True)
    idx = jnp.min(jnp.where(vals == top, lane, float(LANES)), axis=-1, keepdims=True)
    return top, idx


def _channel_kernel(x1_ref, p_ref, gffn_ref, wrg_ref, brg_ref, wre_ref, bre_ref,
                    wg_ref, wu_ref, wd_ref, gple_ref, wpg_ref, wpe_ref,
                    out_ref, h_ref, comb_ref, acc_ref, *, precise_router):
    g = pl.program_id(1)
    lane_i = lax.broadcasted_iota(jnp.int32, (1, LANES), 1)
    lane = lane_i.astype(F32)

    @pl.when(g == 0)
    def _():
        h = _rms(x1_ref[...], gffn_ref[...])
        h_ref[...] = h.astype(BF16)
        lg = _mm(h, wrg_ref[...], precise_router) + brg_ref[...]
        top_g, g_idx = _first_index_of_max(lg, lane)
        pg_top = 1.0 / jnp.sum(jnp.exp(lg - top_g), axis=-1, keepdims=True)
        le = _mm(h, wre_ref[...], precise_router) + bre_ref[...]
        group_of_lane = (lane_i // EXPERTS_PER_GROUP).astype(F32)
        le = jnp.where(group_of_lane == g_idx, le, -jnp.inf)
        t1, i1 = _first_index_of_max(le, lane)
        t2, i2 = _first_index_of_max(jnp.where(lane == i1, -jnp.inf, le), lane)
        e2 = jnp.exp(t2 - t1)
        den = 1.0 + e2
        comb_ref[...] = jnp.where(lane == i1, pg_top * (1.0 / den),
                                  jnp.where(lane == i2, pg_top * (e2 / den), 0.0))

    h = h_ref[...]
    comb = comb_ref[...]
    hids = []
    for j in range(EXPERTS_PER_GROUP):
        w_tok = jnp.sum(jnp.where(lane_i == g * EXPERTS_PER_GROUP + j, comb, 0.0), axis=-1, keepdims=True)
        a = jnp.dot(h, wg_ref[j], preferred_element_type=F32)
        b = jnp.dot(h, wu_ref[j], preferred_element_type=F32)
        hids.append((jax.nn.silu(a) * b * w_tok).astype(BF16))
    hid = jnp.concatenate(hids, axis=1)
    y = jnp.dot(hid, wd_ref[...].reshape(EXPERTS_PER_GROUP * D_EXPERT, D_MODEL), preferred_element_type=F32)

    @pl.when(g == 0)
    def _():
        acc_ref[...] = y

    @pl.when(g > 0)
    def _():
        acc_ref[...] += y

    @pl.when(g == N_GROUPS - 1)
    def _():
        x2 = x1_ref[...] + acc_ref[...]
        gate = jax.nn.sigmoid(jnp.dot(_rms(x2, gple_ref[...]).astype(BF16), wpg_ref[...],
                                      preferred_element_type=F32))
        pe = jnp.dot(p_ref[...].astype(BF16), wpe_ref[...], preferred_element_type=F32)
        out_ref[...] = x2 + gate * pe


def _channel(x1, p, gffn, wrg, brg, wre, bre, wg, wu, wd, gple, wpg, wpe, *, tm, precise_router):
    t = x1.shape[0]
    row = lambda i, g: (i, 0)
    const2 = lambda i, g: (0, 0)
    group = lambda i, g: (g, 0, 0)
    return pl.pallas_call(
        functools.partial(_channel_kernel, precise_router=precise_router),
        grid=(t // tm, N_GROUPS),
        in_specs=[
            pl.BlockSpec((tm, D_MODEL), row),
            pl.BlockSpec((tm, PLE_DIM), row),
            pl.BlockSpec((1, D_MODEL), const2),
            pl.BlockSpec((D_MODEL, LANES), const2),
            pl.BlockSpec((1, LANES), const2),
            pl.BlockSpec((D_MODEL, LANES), const2),
            pl.BlockSpec((1, LANES), const2),
            pl.BlockSpec((EXPERTS_PER_GROUP, D_MODEL, D_EXPERT), group),
            pl.BlockSpec((EXPERTS_PER_GROUP, D_MODEL, D_EXPERT), group),
            pl.BlockSpec((EXPERTS_PER_GROUP, D_EXPERT, D_MODEL), group),
            pl.BlockSpec((1, D_MODEL), const2),
            pl.BlockSpec((D_MODEL, D_MODEL), const2),
            pl.BlockSpec((PLE_DIM, D_MODEL), const2),
        ],
        out_specs=pl.BlockSpec((tm, D_MODEL), row),
        out_shape=jax.ShapeDtypeStruct((t, D_MODEL), F32),
        scratch_shapes=[pltpu.VMEM((tm, D_MODEL), BF16),
                        pltpu.VMEM((tm, LANES), F32),
                        pltpu.VMEM((tm, D_MODEL), F32)],
        compiler_params=pltpu.CompilerParams(
            dimension_semantics=("arbitrary", "arbitrary"), vmem_limit_bytes=VMEM_LIMIT),
        name="channel",
    )(x1, p, gffn, wrg, brg, wre, bre, wg, wu, wd, gple, wpg, wpe)


def _pad_lanes(a, fill):
    return jnp.pad(a, ((0, 0), (0, LANES - a.shape[1])), constant_values=fill)


def kernel(x_prompt, x_sample, cache_attn_k, cache_attn_v, p_prompt, p_sample, g_mix, w_in, g_gmlp_v, w_gmlp_s, b_gmlp_s, g_q, g_k, rel_bias, w_branch, w_out, g_ffn, w_router_group, b_router_group, w_router_expert, b_router_expert, w_exp_gate, w_exp_up, w_exp_down, g_ple, w_ple_gate, w_ple_proj):
    depth = g_mix.shape[0]
    batch, seq, _ = x_prompt.shape
    dec_batch, dec_seq, _ = x_sample.shape
    lc = cache_attn_k.shape[2]
    keep = min(N_PAST_CHUNKS * CHUNK, seq)
    tm = 512
    assert depth == 1 and keep == tm and seq % tm == 0 and seq % Q_BLOCK == 0
    assert dec_seq <= GMLP_CHUNK and GMLP_CHUNK % dec_seq == 0 and dec_batch * dec_seq == GMLP_CHUNK
    l = 0
    t_s = dec_batch * dec_seq

    row = lambda a: a.reshape(1, -1).astype(F32)
    w_in_b = w_in[l].astype(BF16)
    gq_t = row(jnp.tile(g_q[l], ATTN_HEADS))
    gk_t = row(jnp.tile(g_k[l], ATTN_HEADS))
    head_of_lane = jnp.arange(D_MODEL, dtype=jnp.int32) // HEAD_DIM
    seg = (head_of_lane[:, None] == jnp.arange(LANES, dtype=jnp.int32)[None, :]).astype(BF16)
    segt2 = jnp.concatenate([seg.T, seg.T], axis=0)
    ws_tril = jnp.tril(w_gmlp_s[l])
    ws_p = ws_tril.astype(BF16)
    bs_p = jnp.repeat(b_gmlp_s[l].T, GMLP_GROUP_DIM, axis=1).astype(F32)
    eye = jnp.eye(dec_batch, dtype=F32)
    ws_s = jnp.einsum('ab,gts->gatbs', eye, ws_tril[:, :dec_seq, :dec_seq]).reshape(
        GMLP_GROUPS, t_s, t_s)
    bs_s = jnp.tile(jnp.repeat(b_gmlp_s[l][:, :dec_seq].T, GMLP_GROUP_DIM, axis=1), (dec_batch, 1)).astype(F32)
    wb_b = w_branch[l].astype(BF16)
    wout_b = w_out[l].astype(BF16)
    wrg = _pad_lanes(w_router_group[l], 0.0)
    brg = _pad_lanes(row(b_router_group[l]), NEG)
    wre = _pad_lanes(w_router_expert[l], 0.0)
    bre = _pad_lanes(row(b_router_expert[l]), 0.0)
    wg_b = w_exp_gate[l].astype(BF16)
    wu_b = w_exp_up[l].astype(BF16)
    wd_b = w_exp_down[l].astype(BF16)
    wpg_b = w_ple_gate[l].astype(BF16)
    wpe_b = w_ple_proj[l].astype(BF16)
    bias_prompt = _prompt_bias_table(rel_bias[l])
    bias_c, bias_n = _sample_bias_tables(rel_bias[l], dec_seq, lc, PAST_LEN)

    def channel(x1, p, tile, precise_router):
        cast = (lambda a: a) if precise_router else (lambda a: a.astype(BF16))
        return _channel(x1, p, row(g_ffn[l]), cast(wrg), brg, cast(wre), bre, wg_b, wu_b, wd_b,
                        row(g_ple[l]), wpg_b, wpe_b, tm=tile, precise_router=precise_router)

    xp = x_prompt.reshape(batch * seq, D_MODEL)
    u, vn, q, k, v, sga, sgb, kst, vst = _inproj(
        xp, row(g_mix[l]), w_in_b, row(g_gmlp_v[l]), gq_t, gk_t, seg, segt2,
        tm=tm, tiles_per_batch=seq // tm, emit_vn_state=False, precise=False)
    ob = _attn_prompt(q, k, v, bias_prompt, batch=batch, seq=seq)
    x1 = _merge(xp, u, vn, sga, sgb, ob, ws_p, bs_p, wb_b, wout_b, tm=tm, precise=False)
    yp = channel(x1, p_prompt[l].reshape(batch * seq, PLE_DIM), tm, False)

    xs = x_sample.reshape(t_s, D_MODEL)
    u, vn, q, k, v, sga, sgb, kss, vss, gvs = _inproj(
        xs, row(g_mix[l]), w_in[l], row(g_gmlp_v[l]), gq_t, gk_t, seg, segt2,
        tm=t_s, tiles_per_batch=1, emit_vn_state=True, precise=True)
    ob = _attn_sample(q, k, v, cache_attn_k[l].reshape(dec_batch, lc, D_MODEL),
                      cache_attn_v[l].reshape(dec_batch, lc, D_MODEL), bias_c, bias_n,
                      batch=dec_batch, rows=dec_seq)
    x1 = _merge(xs, u, vn, sga, sgb, ob, ws_s, bs_s, w_branch[l], w_out[l], tm=t_s, precise=True)
    ys = channel(x1, p_sample[l].reshape(t_s, PLE_DIM), t_s, True)

    return (yp.reshape(batch, seq, D_MODEL),
            ys.reshape(dec_batch, dec_seq, D_MODEL),
            kst.reshape(1, batch, keep, ATTN_HEADS, HEAD_DIM),
            vst.reshape(1, batch, keep, ATTN_HEADS, HEAD_DIM),
            kss.reshape(1, dec_batch, dec_seq, ATTN_HEADS, HEAD_DIM),
            vss.reshape(1, dec_batch, dec_seq, ATTN_HEADS, HEAD_DIM),
            gvs.reshape(1, dec_batch, dec_seq, D_MODEL))
```

```python
import functools

import jax
import jax.numpy as jnp
import numpy as np
from jax import lax
from jax.experimental import pallas as pl
from jax.experimental.pallas import tpu as pltpu

F32 = jnp.float32
BF16 = jnp.bfloat16

D_MODEL = 1024
CHUNK = 64
N_PAST_CHUNKS = 8
HEAD_DIM = 64
ATTN_HEADS = D_MODEL // HEAD_DIM
MAX_REL = 128
GMLP_CHUNK = 128
GMLP_GROUPS = 4
GMLP_GROUP_DIM = D_MODEL // GMLP_GROUPS
N_SECTIONS = 7
N_GROUPS = 4
EXPERTS_PER_GROUP = 8
N_EXPERTS = N_GROUPS * EXPERTS_PER_GROUP
D_EXPERT = D_MODEL // 4
PLE_DIM = 256
EPS = 1e-6
NEG = -1e30
PAST_LEN = 1024

LANES = 128
SUBLANES = 8
Q_BLOCK = 2 * CHUNK
K_WINDOW = Q_BLOCK + N_PAST_CHUNKS * CHUNK
N_SHIFT = N_PAST_CHUNKS * CHUNK // Q_BLOCK + 1
ROUTE_ROWS = 16
ROUTE_TILE = 512
N_ZERO_FILLS = 2 * N_GROUPS + 1
DMA_UNROLL = 8
ATTN_UNROLL = 4
VMEM_LIMIT = 56 * 1024 * 1024


def _rms(x, gain):
    return x * lax.rsqrt(jnp.mean(x * x, axis=-1, keepdims=True) + EPS) * gain


def _act_dtype(precise):
    return F32 if precise else BF16


def _mm(a, b, precise, dims=None):
    dims = dims or (((a.ndim - 1,), (0,)), ((), ()))
    if precise:
        return lax.dot_general(a.astype(F32), b.astype(F32), dims, precision=lax.Precision.HIGHEST,
                               preferred_element_type=F32)
    return lax.dot_general(a.astype(BF16), b.astype(BF16), dims, preferred_element_type=F32)


_NT = (((1,), (1,)), ((), ()))


def _head_rms(z, gain, seg, segt2, precise):
    ssum = _mm(z * z, seg, precise)
    inv = lax.rsqrt(ssum * (1.0 / HEAD_DIM) + EPS)
    if precise:
        full = _mm(inv, segt2[:LANES], True)
    else:
        hi = inv.astype(BF16)
        lo = (inv - hi.astype(F32)).astype(BF16)
        full = _mm(jnp.concatenate([hi, lo], axis=1), segt2, False)
    return z * full * gain


def _inproj_kernel(x_ref, gmix_ref, w_ref, gv_ref, gq_ref, gk_ref, seg_ref, segt2_ref,
                   u_ref, vn_ref, q_ref, k_ref, v_ref, sga_ref, sgb_ref,
                   kst_ref, vst_ref, *maybe_vnst_ref, tiles_per_batch, precise):
    act = _act_dtype(precise)
    h = _rms(x_ref[...], gmix_ref[...]).astype(act)

    def section(s):
        return _mm(h, w_ref[:, s * D_MODEL:(s + 1) * D_MODEL], precise)

    is_state_tile = (pl.program_id(0) % tiles_per_batch) == tiles_per_batch - 1

    u_ref[...] = jax.nn.gelu(section(0)).astype(act)

    vn = _rms(jax.nn.gelu(section(1)), gv_ref[...])
    vn_ref[...] = vn.astype(act)
    if maybe_vnst_ref:
        maybe_vnst_ref[0][...] = vn

    qn = _head_rms(section(2), gq_ref[...], seg_ref[...], segt2_ref[...], precise)
    q_ref[...] = (qn * (HEAD_DIM ** -0.5)).astype(act)

    kn = _head_rms(section(3), gk_ref[...], seg_ref[...], segt2_ref[...], precise)
    k_ref[...] = kn.astype(act)

    @pl.when(is_state_tile)
    def _():
        kst_ref[...] = kn

    v = section(4)
    v_ref[...] = v.astype(act)

    @pl.when(is_state_tile)
    def _():
        vst_ref[...] = v

    sga_ref[...] = jax.nn.sigmoid(section(5)).astype(act)
    sgb_ref[...] = jax.nn.sigmoid(section(6)).astype(act)


def _inproj(x, gmix, w_in, gv, gq_t, gk_t, seg, segt2, *, tm, tiles_per_batch, emit_vn_state, precise):
    t = x.shape[0]
    n_tiles = t // tm
    n_state = n_tiles // tiles_per_batch
    row = lambda i: (i, 0)
    const = lambda i: (0, 0)
    state = lambda i: (i // tiles_per_batch, 0)
    act = jax.ShapeDtypeStruct((t, D_MODEL), _act_dtype(precise))
    st = jax.ShapeDtypeStruct((n_state * tm, D_MODEL), F32)
    out_shape = [act] * 7 + [st, st] + ([st] if emit_vn_state else [])
    out_specs = [pl.BlockSpec((tm, D_MODEL), row)] * 7 + [pl.BlockSpec((tm, D_MODEL), state)] * (
        3 if emit_vn_state else 2)
    return pl.pallas_call(
        functools.partial(_inproj_kernel, tiles_per_batch=tiles_per_batch, precise=precise),
        grid=(n_tiles,),
        in_specs=[
            pl.BlockSpec((tm, D_MODEL), row),
            pl.BlockSpec((1, D_MODEL), const),
            pl.BlockSpec((D_MODEL, N_SECTIONS * D_MODEL), const, pipeline_mode=pl.Buffered(1)),
            pl.BlockSpec((1, D_MODEL), const),
            pl.BlockSpec((1, D_MODEL), const),
            pl.BlockSpec((1, D_MODEL), const),
            pl.BlockSpec((D_MODEL, LANES), const),
            pl.BlockSpec((2 * LANES, D_MODEL), const),
        ],
        out_specs=out_specs,
        out_shape=out_shape,
        compiler_params=pltpu.CompilerParams(
            dimension_semantics=("arbitrary",), vmem_limit_bytes=VMEM_LIMIT),
        name="inproj",
    )(x, gmix, w_in, gv, gq_t, gk_t, seg, segt2)


def _attn_prompt_kernel(q_ref, k_ref, v_ref, bias_ref, o_ref, *, seq):
    lane = lax.broadcasted_iota(jnp.int32, (1, LANES), 1)
    low_head = lane < HEAD_DIM
    zero = jnp.zeros((), BF16)

    def block(qb, carry):
        q_start = pl.multiple_of(qb * Q_BLOCK, Q_BLOCK)
        shift = jnp.maximum(N_SHIFT - 1 - qb, 0)
        k_start = pl.multiple_of((qb + shift) * Q_BLOCK - (K_WINDOW - Q_BLOCK), Q_BLOCK)
        q = q_ref[pl.ds(q_start, Q_BLOCK), :]
        k = k_ref[pl.ds(k_start, K_WINDOW), :]
        v = v_ref[pl.ds(k_start, K_WINDOW), :]
        q2 = jnp.concatenate([jnp.where(low_head, q, zero), jnp.where(low_head, zero, q)], axis=0)
        s = lax.dot_general(q2, k, _NT, preferred_element_type=F32) + bias_ref[shift]
        m = jnp.max(s, axis=-1, keepdims=True)
        p = jnp.exp(s - m)
        l = jnp.sum(p, axis=-1, keepdims=True)
        o = jnp.dot(p.astype(BF16), v, preferred_element_type=F32) * (1.0 / l)
        o_ref[pl.ds(q_start, Q_BLOCK), :] = jnp.where(low_head, o[:Q_BLOCK], o[Q_BLOCK:]).astype(BF16)
        return carry

    lax.fori_loop(0, seq // Q_BLOCK, block, 0, unroll=ATTN_UNROLL)


def _attn_prompt(q, k, v, bias_tab, *, batch, seq):
    q3, k3, v3 = (a.reshape(batch, seq, D_MODEL) for a in (q, k, v))
    head_pair = lambda hp, b: (b, 0, hp)
    spec = pl.BlockSpec((None, seq, LANES), head_pair)
    out = pl.pallas_call(
        functools.partial(_attn_prompt_kernel, seq=seq),
        grid=(ATTN_HEADS // 2, batch),
        in_specs=[spec, spec, spec,
                  pl.BlockSpec((None, N_SHIFT, 2 * Q_BLOCK, K_WINDOW), lambda hp, b: (hp, 0, 0, 0))],
        out_specs=spec,
        out_shape=jax.ShapeDtypeStruct((batch, seq, D_MODEL), BF16),
        compiler_params=pltpu.CompilerParams(
            dimension_semantics=("arbitrary", "arbitrary"), vmem_limit_bytes=VMEM_LIMIT),
        name="attn_prompt",
    )(q3, k3, v3, bias_tab)
    return out.reshape(batch * seq, D_MODEL)


def _rel_bias_rows(rel_bias, q0, k0, rows, ncols):
    dmax = q0 - k0 + rows - 1
    total = rows + ncols - 1
    dmin = dmax - total + 1
    n_lo = min(max(-MAX_REL - dmin, 0), total)
    n_hi = min(max(dmax - MAX_REL, 0), total)
    mid0 = dmin + n_lo + MAX_REL
    asc = jnp.concatenate([jnp.repeat(rel_bias[:, :1], n_lo, axis=1),
                           rel_bias[:, mid0:mid0 + total - n_lo - n_hi],
                           jnp.repeat(rel_bias[:, -1:], n_hi, axis=1)], axis=1).astype(F32)
    desc = asc[:, ::-1]
    return jnp.stack([desc[:, rows - 1 - i:rows - 1 - i + ncols] for i in range(rows)], axis=1)


def _band_ok(q_pos, k_pos):
    q_pos, k_pos = q_pos[:, None], k_pos[None, :]
    return (k_pos >= 0) & (k_pos // CHUNK <= q_pos // CHUNK) & (k_pos // CHUNK >= q_pos // CHUNK - N_PAST_CHUNKS)


def _prompt_bias_table(rel_bias):
    back = N_PAST_CHUNKS * CHUNK
    span = K_WINDOW + (N_SHIFT - 1) * Q_BLOCK
    ok = _band_ok(back + np.arange(Q_BLOCK), np.arange(span))
    ext = jnp.where(ok[None], _rel_bias_rows(rel_bias, back, 0, Q_BLOCK, span), NEG)
    tab = jnp.stack([ext[:, :, s * Q_BLOCK:s * Q_BLOCK + K_WINDOW] for s in range(N_SHIFT)], axis=1)
    tab = tab.reshape(ATTN_HEADS // 2, 2, N_SHIFT, Q_BLOCK, K_WINDOW).transpose(0, 2, 1, 3, 4)
    return tab.reshape(ATTN_HEADS // 2, N_SHIFT, 2 * Q_BLOCK, K_WINDOW)


def _attn_sample_kernel(q_ref, k_ref, v_ref, ck_ref, cv_ref, bc_ref, bn_ref, o_ref):
    lane = lax.broadcasted_iota(jnp.int32, (1, LANES), 1)
    low_head = lane < HEAD_DIM
    for hp in range(ATTN_HEADS // 2):
        cols = slice(hp * LANES, (hp + 1) * LANES)
        q = q_ref[:, cols]
        kn = k_ref[:, cols]
        vn = v_ref[:, cols]
        kc = ck_ref[:, cols]
        vc = cv_ref[:, cols]
        out = None
        for e in range(2):
            mine = low_head if e == 0 else jnp.logical_not(low_head)
            qe = jnp.where(mine, q, 0.0)
            s_c = _mm(qe, kc, True, _NT) + bc_ref[2 * hp + e]
            s_n = _mm(qe, kn, True, _NT) + bn_ref[2 * hp + e]
            m = jnp.maximum(jnp.max(s_c, axis=-1, keepdims=True), jnp.max(s_n, axis=-1, keepdims=True))
            p_c = jnp.exp(s_c - m)
            p_n = jnp.exp(s_n - m)
            l = jnp.sum(p_c, axis=-1, keepdims=True) + jnp.sum(p_n, axis=-1, keepdims=True)
            o = _mm(p_c, jnp.where(mine, vc, 0.0), True) + _mm(p_n, jnp.where(mine, vn, 0.0), True)
            o = o * (1.0 / l)
            out = o if out is None else out + o
        o_ref[:, cols] = out


def _attn_sample(q, k, v, cache_k, cache_v, bias_c, bias_n, *, batch, rows):
    lc = cache_k.shape[1]
    new = pl.BlockSpec((rows, D_MODEL), lambda b: (b, 0))
    cache = pl.BlockSpec((None, lc, D_MODEL), lambda b: (b, 0, 0))
    return pl.pallas_call(
        _attn_sample_kernel,
        grid=(batch,),
        in_specs=[new, new, new, cache, cache,
                  pl.BlockSpec((ATTN_HEADS, rows, lc), lambda b: (0, 0, 0)),
                  pl.BlockSpec((ATTN_HEADS, rows, rows), lambda b: (0, 0, 0))],
        out_specs=new,
        out_shape=jax.ShapeDtypeStruct((batch * rows, D_MODEL), F32),
        compiler_params=pltpu.CompilerParams(
            dimension_semantics=("arbitrary",), vmem_limit_bytes=VMEM_LIMIT),
        name="attn_sample",
    )(q, k, v, cache_k, cache_v, bias_c, bias_n)


def _sample_bias_tables(rel_bias, rows, lc, past_len):
    q_pos = past_len + np.arange(rows)
    c_pos = past_len - lc + np.arange(lc)
    bias_c = jnp.where(_band_ok(q_pos, c_pos)[None], _rel_bias_rows(rel_bias, past_len, past_len - lc, rows, lc), NEG)
    bias_n = jnp.where(_band_ok(q_pos, q_pos)[None], _rel_bias_rows(rel_bias, past_len, past_len, rows, rows), NEG)
    return bias_c, bias_n


def _route_rows(x1, gffn_ref, wrgt_ref, brgt_ref, upper_ref, gidx_ref, rank_ref, base_ref):
    tm = x1.shape[0]

    @pl.when(pl.program_id(0) == 0)
    def _():
        base_ref[...] = jnp.zeros_like(base_ref)

    h = _rms(x1, gffn_ref[...]).astype(BF16)
    lgt = lax.dot_general(wrgt_ref[...], h, _NT, preferred_element_type=F32) + brgt_ref[...]
    sub = lax.broadcasted_iota(jnp.int32, (ROUTE_ROWS, tm), 0).astype(F32)
    top = jnp.max(lgt, axis=0, keepdims=True)
    gi = jnp.min(jnp.where(lgt == top, sub, float(ROUTE_ROWS)), axis=0, keepdims=True)
    onehot = jnp.where(sub == gi, 1.0, 0.0)
    before = jnp.dot(onehot.astype(BF16), upper_ref[...], preferred_element_type=F32)
    base = base_ref[...]
    rank = jnp.sum(onehot * (before + base[:, :1]), axis=0, keepdims=True)
    base_ref[...] = base + jnp.sum(onehot, axis=1, keepdims=True)
    gidx_ref[...] = gi.astype(jnp.int32)
    rank_ref[...] = rank.astype(jnp.int32)


def _merge_kernel(x_ref, u_ref, vn_ref, sga_ref, sgb_ref, ob_ref, ws_ref, bs_ref, wb_ref, wout_ref,
                  *rest, tm, precise, route):
    if route:
        gffn_ref, wrgt_ref, brgt_ref, upper_ref, x1_ref, gidx_ref, rank_ref, oa_ref, base_ref = rest
    else:
        x1_ref, oa_ref = rest
    act = _act_dtype(precise)
    for c in range(tm // GMLP_CHUNK):
        rows = slice(c * GMLP_CHUNK, (c + 1) * GMLP_CHUNK)
        for g in range(GMLP_GROUPS):
            cols = slice(g * GMLP_GROUP_DIM, (g + 1) * GMLP_GROUP_DIM)
            mix = _mm(ws_ref[g], vn_ref[rows, cols], precise) + bs_ref[:, cols]
            oa_ref[rows, cols] = (u_ref[rows, cols].astype(F32) * mix).astype(act)
    b_a = _mm(oa_ref[...], wb_ref[0], precise)
    b_b = _mm(ob_ref[...], wb_ref[1], precise)
    m = sga_ref[...].astype(F32) * b_a + sgb_ref[...].astype(F32) * b_b
    x1 = x_ref[...] + _mm(m, wout_ref[...], precise)
    x1_ref[...] = x1
    if route:
        _route_rows(x1, gffn_ref, wrgt_ref, brgt_ref, upper_ref, gidx_ref, rank_ref, base_ref)


def _merge(x, u, vn, sga, sgb, ob, ws, bs_full, wb, wout, route_params=None, *, tm, precise):
    t = x.shape[0]
    n_tiles = t // tm
    route = route_params is not None
    row = pl.BlockSpec((tm, D_MODEL), lambda i: (i, 0))
    const = lambda i: (0, 0)
    in_specs = [row, row, row, row, row, row,
                pl.BlockSpec((GMLP_GROUPS, GMLP_CHUNK, GMLP_CHUNK), lambda i: (0, 0, 0)),
                pl.BlockSpec((GMLP_CHUNK, D_MODEL), const),
                pl.BlockSpec((2, D_MODEL, D_MODEL), lambda i: (0, 0, 0)),
                pl.BlockSpec((D_MODEL, D_MODEL), const)]
    out_specs = [row]
    out_shape = [jax.ShapeDtypeStruct((t, D_MODEL), F32)]
    scratch = [pltpu.VMEM((tm, D_MODEL), _act_dtype(precise))]
    args = [x, u, vn, sga, sgb, ob, ws, bs_full, wb, wout]
    if route:
        in_specs += [pl.BlockSpec((1, D_MODEL), const), pl.BlockSpec((ROUTE_ROWS, D_MODEL), const),
                     pl.BlockSpec((ROUTE_ROWS, 1), const), pl.BlockSpec((tm, tm), const)]
        tok_row = pl.BlockSpec((None, 1, tm), lambda i: (i, 0, 0))
        out_specs += [tok_row, tok_row]
        out_shape += [jax.ShapeDtypeStruct((n_tiles, 1, tm), jnp.int32)] * 2
        scratch += [pltpu.VMEM((ROUTE_ROWS, LANES), F32)]
        args += list(route_params)
    return pl.pallas_call(
        functools.partial(_merge_kernel, tm=tm, precise=precise, route=route),
        grid=(n_tiles,),
        in_specs=in_specs,
        out_specs=out_specs,
        out_shape=out_shape,
        scratch_shapes=scratch,
        compiler_params=pltpu.CompilerParams(
            dimension_semantics=("arbitrary",), vmem_limit_bytes=VMEM_LIMIT),
        name="merge",
    )(*args)


def _first_index_of_max(vals, lane):
    top = jnp.max(vals, axis=-1, keepdims=True)
    idx = jnp.min(jnp.where(vals == top, lane, float(LANES)), axis=-1, keepdims=True)
    return top, idx


def _top2_weights(le_in_group, lane, pg_top):
    t1, i1 = _first_index_of_max(le_in_group, lane)
    t2, i2 = _first_index_of_max(jnp.where(lane == i1, -jnp.inf, le_in_group), lane)
    e2 = jnp.exp(t2 - t1)
    den = 1.0 + e2
    return jnp.where(lane == i1, pg_top * (1.0 / den), jnp.where(lane == i2, pg_top * (e2 / den), 0.0))


def _group_experts(h, comb, lane_i, g, wg_ref, wu_ref, wd_ref):
    hids = []
    for j in range(EXPERTS_PER_GROUP):
        w_tok = jnp.sum(jnp.where(lane_i == g * EXPERTS_PER_GROUP + j, comb, 0.0), axis=-1, keepdims=True)
        a = jnp.dot(h, wg_ref[j], preferred_element_type=F32)
        b = jnp.dot(h, wu_ref[j], preferred_element_type=F32)
        hids.append((jax.nn.silu(a) * b * w_tok).astype(BF16))
    hid = jnp.concatenate(hids, axis=1)
    return jnp.dot(hid, wd_ref[...].reshape(EXPERTS_PER_GROUP * D_EXPERT, D_MODEL), preferred_element_type=F32)


def _embedding_gate(x2, p_ref, gple_ref, wpg_ref, wpe_ref):
    gate = jax.nn.sigmoid(jnp.dot(_rms(x2, gple_ref[...]).astype(BF16), wpg_ref[...], preferred_element_type=F32))
    pe = jnp.dot(p_ref[...].astype(BF16), wpe_ref[...], preferred_element_type=F32)
    return x2 + gate * pe


def _dispatch_kernel(pos_ref, zstart_ref, x1_hbm, zeros_hbm, xs_hbm, row_sem, zero_sem, *, tm):
    i = pl.program_id(0)

    def zero_copy(g):
        start = pl.multiple_of(zstart_ref[g], SUBLANES)
        return pltpu.make_async_copy(zeros_hbm, xs_hbm.at[pl.ds(start, ROUTE_TILE)], zero_sem)

    @pl.when(i == 0)
    def _():
        for g in range(N_ZERO_FILLS):
            zero_copy(g).start()
            zero_copy(g).wait()

    def issue(r, carry):
        t = i * tm + r
        pltpu.make_async_copy(x1_hbm.at[t], xs_hbm.at[pos_ref[t]], row_sem).start()
        return carry

    lax.fori_loop(0, tm, issue, 0, unroll=DMA_UNROLL)

    def wait_tile():
        pltpu.make_async_copy(x1_hbm.at[pl.ds(0, tm)], xs_hbm.at[pl.ds(0, tm)], row_sem).wait()

    @pl.when(i > 0)
    def _():
        wait_tile()

    @pl.when(i == pl.num_programs(0) - 1)
    def _():
        wait_tile()


def _dispatch(pos, zstart, x1, *, tm, n_rows):
    t = x1.shape[0]
    zeros = jnp.zeros((ROUTE_TILE, D_MODEL), F32)
    return pl.pallas_call(
        functools.partial(_dispatch_kernel, tm=tm),
        grid_spec=pltpu.PrefetchScalarGridSpec(
            num_scalar_prefetch=2,
            grid=(t // tm,),
            in_specs=[pl.BlockSpec(memory_space=pl.ANY), pl.BlockSpec(memory_space=pl.ANY)],
            out_specs=pl.BlockSpec(memory_space=pl.ANY),
            scratch_shapes=[pltpu.SemaphoreType.DMA(()), pltpu.SemaphoreType.DMA(())]),
        out_shape=jax.ShapeDtypeStruct((n_rows, D_MODEL), F32),
        compiler_params=pltpu.CompilerParams(dimension_semantics=("arbitrary",), has_side_effects=True),
        name="dispatch",
    )(pos, zstart, x1, zeros)


def _experts_kernel(tile_group_ref, n_valid_ref, xs_ref, gffn_ref, wrg_ref, brg_ref, wre_ref, bre_ref,
                    wg_ref, wu_ref, wd_ref, ys_ref):
    i = pl.program_id(0)

    @pl.when(i < n_valid_ref[0])
    def _():
        g = tile_group_ref[i]
        lane_i = lax.broadcasted_iota(jnp.int32, (1, LANES), 1)
        lane = lane_i.astype(F32)
        h = _rms(xs_ref[...], gffn_ref[...]).astype(BF16)
        lg = _mm(h, wrg_ref[...], False) + brg_ref[...]
        eg = jnp.exp(lg - jnp.max(lg, axis=-1, keepdims=True))
        pg = jnp.sum(jnp.where(lane_i == g, eg, 0.0), axis=-1, keepdims=True) / jnp.sum(eg, axis=-1, keepdims=True)
        le = _mm(h, wre_ref[...], False) + bre_ref[...]
        comb = _top2_weights(jnp.where(lane_i // EXPERTS_PER_GROUP == g, le, -jnp.inf), lane, pg)
        ys_ref[...] = _group_experts(h, comb, lane_i, g, wg_ref, wu_ref, wd_ref)

    @pl.when(i >= n_valid_ref[0])
    def _():
        ys_ref[...] = jnp.zeros_like(ys_ref)


def _experts(tile_group, n_valid, xs, gffn, wrg, brg, wre, bre, wg, wu, wd, *, n_tiles):
    tile = lambda i, tg, nv: (jnp.minimum(i, nv[0] - 1), 0)
    const = lambda i, tg, nv: (0, 0)
    group = lambda i, tg, nv: (tg[i], 0, 0)
    return pl.pallas_call(
        _experts_kernel,
        grid_spec=pltpu.PrefetchScalarGridSpec(
            num_scalar_prefetch=2,
            grid=(n_tiles,),
            in_specs=[
                pl.BlockSpec((ROUTE_TILE, D_MODEL), tile),
                pl.BlockSpec((1, D_MODEL), const),
                pl.BlockSpec((D_MODEL, LANES), const),
                pl.BlockSpec((1, LANES), const),
                pl.BlockSpec((D_MODEL, LANES), const),
                pl.BlockSpec((1, LANES), const),
                pl.BlockSpec((EXPERTS_PER_GROUP, D_MODEL, D_EXPERT), group),
                pl.BlockSpec((EXPERTS_PER_GROUP, D_MODEL, D_EXPERT), group),
                pl.BlockSpec((EXPERTS_PER_GROUP, D_EXPERT, D_MODEL), group),
            ],
            out_specs=pl.BlockSpec((ROUTE_TILE, D_MODEL), lambda i, tg, nv: (i, 0))),
        out_shape=jax.ShapeDtypeStruct((n_tiles * ROUTE_TILE, D_MODEL), F32),
        compiler_params=pltpu.CompilerParams(
            dimension_semantics=("arbitrary",), vmem_limit_bytes=VMEM_LIMIT),
        name="experts",
    )(tile_group, n_valid, xs, gffn, wrg, brg, wre, bre, wg, wu, wd)


def _combine_kernel(pos_ref, x1_ref, p_ref, gple_ref, wpg_ref, wpe_ref, ys_hbm, out_ref, ybuf, sems, *, tm):
    i = pl.program_id(0)
    n = pl.num_programs(0)

    def fetch(tile, slot):
        def issue(r, carry):
            pltpu.make_async_copy(ys_hbm.at[pos_ref[tile * tm + r]], ybuf.at[slot, r], sems.at[slot]).start()
            return carry
        lax.fori_loop(0, tm, issue, 0, unroll=DMA_UNROLL)

    @pl.when(i == 0)
    def _():
        fetch(0, 0)

    @pl.when(i + 1 < n)
    def _():
        fetch(i + 1, (i + 1) % 2)

    slot = i % 2
    pltpu.make_async_copy(ys_hbm.at[pl.ds(0, tm)], ybuf.at[slot], sems.at[slot]).wait()
    out_ref[...] = _embedding_gate(x1_ref[...] + ybuf[slot], p_ref, gple_ref, wpg_ref, wpe_ref)


def _combine(pos, x1, p, gple, wpg, wpe, ys, *, tm):
    t = x1.shape[0]
    row = lambda i, pos: (i, 0)
    const = lambda i, pos: (0, 0)
    return pl.pallas_call(
        functools.partial(_combine_kernel, tm=tm),
        grid_spec=pltpu.PrefetchScalarGridSpec(
            num_scalar_prefetch=1,
            grid=(t // tm,),
            in_specs=[pl.BlockSpec((tm, D_MODEL), row),
                      pl.BlockSpec((tm, PLE_DIM), row),
                      pl.BlockSpec((1, D_MODEL), const),
                      pl.BlockSpec((D_MODEL, D_MODEL), const),
                      pl.BlockSpec((PLE_DIM, D_MODEL), const),
                      pl.BlockSpec(memory_space=pl.ANY)],
            out_specs=pl.BlockSpec((tm, D_MODEL), row),
            scratch_shapes=[pltpu.VMEM((2, tm, D_MODEL), F32), pltpu.SemaphoreType.DMA((2,))]),
        out_shape=jax.ShapeDtypeStruct((t, D_MODEL), F32),
        compiler_params=pltpu.CompilerParams(
            dimension_semantics=("arbitrary",), vmem_limit_bytes=VMEM_LIMIT),
        name="combine",
    )(pos, x1, p, gple, wpg, wpe, ys)


def _route_plan(gidx, rank, *, n_tiles):
    groups = jnp.arange(N_GROUPS, dtype=jnp.int32)
    member = gidx[None, :] == groups[:, None]
    counts = jnp.sum(member, axis=1, dtype=jnp.int32)
    padded = (counts + ROUTE_TILE - 1) // ROUTE_TILE * ROUTE_TILE
    ends = jnp.cumsum(padded)
    offsets = ends - padded
    pos = rank + jnp.sum(jnp.where(member, offsets[:, None], 0), axis=0, dtype=jnp.int32)
    n_valid = jnp.maximum(ends[-1] // ROUTE_TILE, 1)
    tile_start = jnp.minimum(jnp.arange(n_tiles, dtype=jnp.int32), n_valid - 1) * ROUTE_TILE
    tile_group = jnp.minimum(jnp.sum(tile_start[:, None] >= ends[None, :], axis=1, dtype=jnp.int32), N_GROUPS - 1)
    zero_start = (offsets + counts) // SUBLANES * SUBLANES
    tail = jnp.minimum(ends[-1] + ROUTE_TILE * jnp.arange(N_GROUPS + 1, dtype=jnp.int32), n_tiles * ROUTE_TILE)
    zero_start = jnp.concatenate([zero_start, tail])
    return pos, zero_start, tile_group, n_valid.reshape(1).astype(jnp.int32)


def _channel_kernel(x1_ref, p_ref, gffn_ref, wrg_ref, brg_ref, wre_ref, bre_ref,
                    wg_ref, wu_ref, wd_ref, gple_ref, wpg_ref, wpe_ref,
                    out_ref, h_ref, comb_ref, acc_ref, *, precise_router):
    g = pl.program_id(1)
    lane_i = lax.broadcasted_iota(jnp.int32, (1, LANES), 1)
    lane = lane_i.astype(F32)

    @pl.when(g == 0)
    def _():
        h = _rms(x1_ref[...], gffn_ref[...])
        h_ref[...] = h.astype(BF16)
        lg = _mm(h, wrg_ref[...], precise_router) + brg_ref[...]
        top_g, g_idx = _first_index_of_max(lg, lane)
        pg_top = 1.0 / jnp.sum(jnp.exp(lg - top_g), axis=-1, keepdims=True)
        le = _mm(h, wre_ref[...], precise_router) + bre_ref[...]
        group_of_lane = (lane_i // EXPERTS_PER_GROUP).astype(F32)
        comb_ref[...] = _top2_weights(jnp.where(group_of_lane == g_idx, le, -jnp.inf), lane, pg_top)

    y = _group_experts(h_ref[...], comb_ref[...], lane_i, g, wg_ref, wu_ref, wd_ref)

    @pl.when(g == 0)
    def _():
        acc_ref[...] = y

    @pl.when(g > 0)
    def _():
        acc_ref[...] += y

    @pl.when(g == N_GROUPS - 1)
    def _():
        out_ref[...] = _embedding_gate(x1_ref[...] + acc_ref[...], p_ref, gple_ref, wpg_ref, wpe_ref)


def _channel(x1, p, gffn, wrg, brg, wre, bre, wg, wu, wd, gple, wpg, wpe, *, tm, precise_router):
    t = x1.shape[0]
    row = lambda i, g: (i, 0)
    const2 = lambda i, g: (0, 0)
    group = lambda i, g: (g, 0, 0)
    return pl.pallas_call(
        functools.partial(_channel_kernel, precise_router=precise_router),
        grid=(t // tm, N_GROUPS),
        in_specs=[
            pl.BlockSpec((tm, D_MODEL), row),
            pl.BlockSpec((tm, PLE_DIM), row),
            pl.BlockSpec((1, D_MODEL), const2),
            pl.BlockSpec((D_MODEL, LANES), const2),
            pl.BlockSpec((1, LANES), const2),
            pl.BlockSpec((D_MODEL, LANES), const2),
            pl.BlockSpec((1, LANES), const2),
            pl.BlockSpec((EXPERTS_PER_GROUP, D_MODEL, D_EXPERT), group),
            pl.BlockSpec((EXPERTS_PER_GROUP, D_MODEL, D_EXPERT), group),
            pl.BlockSpec((EXPERTS_PER_GROUP, D_EXPERT, D_MODEL), group),
            pl.BlockSpec((1, D_MODEL), const2),
            pl.BlockSpec((D_MODEL, D_MODEL), const2),
            pl.BlockSpec((PLE_DIM, D_MODEL), const2),
        ],
        out_specs=pl.BlockSpec((tm, D_MODEL), row),
        out_shape=jax.ShapeDtypeStruct((t, D_MODEL), F32),
        scratch_shapes=[pltpu.VMEM((tm, D_MODEL), BF16),
                        pltpu.VMEM((tm, LANES), F32),
                        pltpu.VMEM((tm, D_MODEL), F32)],
        compiler_params=pltpu.CompilerParams(
            dimension_semantics=("arbitrary", "arbitrary"), vmem_limit_bytes=VMEM_LIMIT),
        name="channel",
    )(x1, p, gffn, wrg, brg, wre, bre, wg, wu, wd, gple, wpg, wpe)


def _pad_lanes(a, fill):
    return jnp.pad(a, ((0, 0), (0, LANES - a.shape[1])), constant_values=fill)


def kernel(x_prompt, x_sample, cache_attn_k, cache_attn_v, p_prompt, p_sample, g_mix, w_in, g_gmlp_v, w_gmlp_s, b_gmlp_s, g_q, g_k, rel_bias, w_branch, w_out, g_ffn, w_router_group, b_router_group, w_router_expert, b_router_expert, w_exp_gate, w_exp_up, w_exp_down, g_ple, w_ple_gate, w_ple_proj):
    depth = g_mix.shape[0]
    batch, seq, _ = x_prompt.shape
    dec_batch, dec_seq, _ = x_sample.shape
    lc = cache_attn_k.shape[2]
    keep = min(N_PAST_CHUNKS * CHUNK, seq)
    tm = 512
    assert depth == 1 and keep == tm and seq % tm == 0 and seq % Q_BLOCK == 0
    assert dec_seq <= GMLP_CHUNK and GMLP_CHUNK % dec_seq == 0 and dec_batch * dec_seq == GMLP_CHUNK
    l = 0
    t_s = dec_batch * dec_seq

    row = lambda a: a.reshape(1, -1).astype(F32)
    w_in_b = w_in[l].astype(BF16)
    gq_t = row(jnp.tile(g_q[l], ATTN_HEADS))
    gk_t = row(jnp.tile(g_k[l], ATTN_HEADS))
    head_of_lane = jnp.arange(D_MODEL, dtype=jnp.int32) // HEAD_DIM
    seg = (head_of_lane[:, None] == jnp.arange(LANES, dtype=jnp.int32)[None, :]).astype(BF16)
    segt2 = jnp.concatenate([seg.T, seg.T], axis=0)
    ws_tril = jnp.tril(w_gmlp_s[l])
    ws_p = ws_tril.astype(BF16)
    bs_p = jnp.repeat(b_gmlp_s[l].T, GMLP_GROUP_DIM, axis=1).astype(F32)
    eye = jnp.eye(dec_batch, dtype=F32)
    ws_s = jnp.einsum('ab,gts->gatbs', eye, ws_tril[:, :dec_seq, :dec_seq]).reshape(
        GMLP_GROUPS, t_s, t_s)
    bs_s = jnp.tile(jnp.repeat(b_gmlp_s[l][:, :dec_seq].T, GMLP_GROUP_DIM, axis=1), (dec_batch, 1)).astype(F32)
    wb_b = w_branch[l].astype(BF16)
    wout_b = w_out[l].astype(BF16)
    wrg = _pad_lanes(w_router_group[l], 0.0)
    brg = _pad_lanes(row(b_router_group[l]), NEG)
    wre = _pad_lanes(w_router_expert[l], 0.0)
    bre = _pad_lanes(row(b_router_expert[l]), 0.0)
    wg_b = w_exp_gate[l].astype(BF16)
    wu_b = w_exp_up[l].astype(BF16)
    wd_b = w_exp_down[l].astype(BF16)
    wpg_b = w_ple_gate[l].astype(BF16)
    wpe_b = w_ple_proj[l].astype(BF16)
    bias_prompt = _prompt_bias_table(rel_bias[l])
    bias_c, bias_n = _sample_bias_tables(rel_bias[l], dec_seq, lc, PAST_LEN)

    wrgt = jnp.pad(w_router_group[l].T, ((0, ROUTE_ROWS - N_GROUPS), (0, 0))).astype(BF16)
    brgt = jnp.pad(b_router_group[l].astype(F32), (0, ROUTE_ROWS - N_GROUPS), constant_values=NEG).reshape(ROUTE_ROWS, 1)
    upper = jnp.asarray(np.triu(np.ones((tm, tm), np.float32), 1), BF16)

    xp = x_prompt.reshape(batch * seq, D_MODEL)
    u, vn, q, k, v, sga, sgb, kst, vst = _inproj(
        xp, row(g_mix[l]), w_in_b, row(g_gmlp_v[l]), gq_t, gk_t, seg, segt2,
        tm=tm, tiles_per_batch=seq // tm, emit_vn_state=False, precise=False)
    ob = _attn_prompt(q, k, v, bias_prompt, batch=batch, seq=seq)
    x1, gidx, rank = _merge(xp, u, vn, sga, sgb, ob, ws_p, bs_p, wb_b, wout_b,
                            (row(g_ffn[l]), wrgt, brgt, upper), tm=tm, precise=False)
    n_route_tiles = batch * seq // ROUTE_TILE + N_GROUPS
    pos, zero_start, tile_group, n_valid = _route_plan(gidx.reshape(-1), rank.reshape(-1), n_tiles=n_route_tiles)
    x_sorted = _dispatch(pos, zero_start, x1, tm=tm, n_rows=(n_route_tiles + 1) * ROUTE_TILE)
    y_sorted = _experts(tile_group, n_valid, x_sorted, row(g_ffn[l]), wrg.astype(BF16), brg, wre.astype(BF16), bre,
                        wg_b, wu_b, wd_b, n_tiles=n_route_tiles)
    yp = _combine(pos, x1, p_prompt[l].reshape(batch * seq, PLE_DIM), row(g_ple[l]), wpg_b, wpe_b, y_sorted, tm=tm)

    xs = x_sample.reshape(t_s, D_MODEL)
    u, vn, q, k, v, sga, sgb, kss, vss, gvs = _inproj(
        xs, row(g_mix[l]), w_in[l], row(g_gmlp_v[l]), gq_t, gk_t, seg, segt2,
        tm=t_s, tiles_per_batch=1, emit_vn_state=True, precise=True)
    ob = _attn_sample(q, k, v, cache_attn_k[l].reshape(dec_batch, lc, D_MODEL),
                      cache_attn_v[l].reshape(dec_batch, lc, D_MODEL), bias_c, bias_n,
                      batch=dec_batch, rows=dec_seq)
    x1, = _merge(xs, u, vn, sga, sgb, ob, ws_s, bs_s, w_branch[l], w_out[l], tm=t_s, precise=True)
    ys = _channel(x1, p_sample[l].reshape(t_s, PLE_DIM), row(g_ffn[l]), wrg, brg, wre, bre, wg_b, wu_b, wd_b,
                  row(g_ple[l]), wpg_b, wpe_b, tm=t_s, precise_router=True)

    return (yp.reshape(batch, seq, D_MODEL),
            ys.reshape(dec_batch, dec_seq, D_MODEL),
            kst.reshape(1, batch, keep, ATTN_HEADS, HEAD_DIM),
            vst.reshape(1, batch, keep, ATTN_HEADS, HEAD_DIM),
            kss.reshape(1, dec_batch, dec_seq, ATTN_HEADS, HEAD_DIM),
            vss.reshape(1, dec_batch, dec_seq, ATTN_HEADS, HEAD_DIM),
            gvs.reshape(1, dec_batch, dec_seq, D_MODEL))
```

```python
import functools

import jax
import jax.numpy as jnp
import numpy as np
from jax import lax
from jax.experimental import pallas as pl
from jax.experimental.pallas import tpu as pltpu

F32 = jnp.float32
BF16 = jnp.bfloat16

D_MODEL = 1024
CHUNK = 64
N_PAST_CHUNKS = 8
HEAD_DIM = 64
ATTN_HEADS = D_MODEL // HEAD_DIM
MAX_REL = 128
GMLP_CHUNK = 128
GMLP_GROUPS = 4
GMLP_GROUP_DIM = D_MODEL // GMLP_GROUPS
N_SECTIONS = 7
N_GROUPS = 4
EXPERTS_PER_GROUP = 8
N_EXPERTS = N_GROUPS * EXPERTS_PER_GROUP
D_EXPERT = D_MODEL // 4
PLE_DIM = 256
EPS = 1e-6
NEG = -1e30
PAST_LEN = 1024

LANES = 128
Q_BLOCK = 2 * CHUNK
K_WINDOW = Q_BLOCK + N_PAST_CHUNKS * CHUNK
N_SHIFT = N_PAST_CHUNKS * CHUNK // Q_BLOCK + 1
BIAS_SPAN = K_WINDOW + (N_SHIFT - 1) * Q_BLOCK
ROUTE_ROWS = 16
ROUTE_TILE = 512
DMA_UNROLL = 8
ATTN_UNROLL = 4
VMEM_LIMIT = 56 * 1024 * 1024


def _rms(x, gain):
    return x * lax.rsqrt(jnp.mean(x * x, axis=-1, keepdims=True) + EPS) * gain


def _act_dtype(precise):
    return F32 if precise else BF16


def _mm(a, b, precise, dims=None):
    dims = dims or (((a.ndim - 1,), (0,)), ((), ()))
    if precise:
        return lax.dot_general(a.astype(F32), b.astype(F32), dims, precision=lax.Precision.HIGHEST,
                               preferred_element_type=F32)
    return lax.dot_general(a.astype(BF16), b.astype(BF16), dims, preferred_element_type=F32)


_NT = (((1,), (1,)), ((), ()))


def _head_rms(z, gain, seg, segt2, precise):
    ssum = _mm(z * z, seg, precise)
    inv = lax.rsqrt(ssum * (1.0 / HEAD_DIM) + EPS)
    if precise:
        full = _mm(inv, segt2[:LANES], True)
    else:
        hi = inv.astype(BF16)
        lo = (inv - hi.astype(F32)).astype(BF16)
        full = _mm(jnp.concatenate([hi, lo], axis=1), segt2, False)
    return z * full * gain


def _inproj_kernel(x_ref, gmix_ref, w_ref, gv_ref, gq_ref, gk_ref, seg_ref, segt2_ref,
                   u_ref, vn_ref, q_ref, k_ref, v_ref, sga_ref, sgb_ref,
                   kst_ref, vst_ref, *maybe_vnst_ref, tiles_per_batch, precise):
    act = _act_dtype(precise)
    h = _rms(x_ref[...], gmix_ref[...]).astype(act)

    def section(s):
        return _mm(h, w_ref[:, s * D_MODEL:(s + 1) * D_MODEL], precise)

    is_state_tile = (pl.program_id(0) % tiles_per_batch) == tiles_per_batch - 1

    u_ref[...] = jax.nn.gelu(section(0)).astype(act)

    vn = _rms(jax.nn.gelu(section(1)), gv_ref[...])
    vn_ref[...] = vn.astype(act)
    if maybe_vnst_ref:
        maybe_vnst_ref[0][...] = vn

    qn = _head_rms(section(2), gq_ref[...], seg_ref[...], segt2_ref[...], precise)
    q_ref[...] = (qn * (HEAD_DIM ** -0.5)).astype(act)

    kn = _head_rms(section(3), gk_ref[...], seg_ref[...], segt2_ref[...], precise)
    k_ref[...] = kn.astype(act)

    @pl.when(is_state_tile)
    def _():
        kst_ref[...] = kn

    v = section(4)
    v_ref[...] = v.astype(act)

    @pl.when(is_state_tile)
    def _():
        vst_ref[...] = v

    sga_ref[...] = jax.nn.sigmoid(section(5)).astype(act)
    sgb_ref[...] = jax.nn.sigmoid(section(6)).astype(act)


def _inproj(x, gmix, w_in, gv, gq_t, gk_t, seg, segt2, *, tm, tiles_per_batch, emit_vn_state, precise):
    t = x.shape[0]
    n_tiles = t // tm
    n_state = n_tiles // tiles_per_batch
    row = lambda i: (i, 0)
    const = lambda i: (0, 0)
    state = lambda i: (i // tiles_per_batch, 0)
    act = jax.ShapeDtypeStruct((t, D_MODEL), _act_dtype(precise))
    st = jax.ShapeDtypeStruct((n_state * tm, D_MODEL), F32)
    out_shape = [act] * 7 + [st, st] + ([st] if emit_vn_state else [])
    out_specs = [pl.BlockSpec((tm, D_MODEL), row)] * 7 + [pl.BlockSpec((tm, D_MODEL), state)] * (
        3 if emit_vn_state else 2)
    return pl.pallas_call(
        functools.partial(_inproj_kernel, tiles_per_batch=tiles_per_batch, precise=precise),
        grid=(n_tiles,),
        in_specs=[
            pl.BlockSpec((tm, D_MODEL), row),
            pl.BlockSpec((1, D_MODEL), const),
            pl.BlockSpec((D_MODEL, N_SECTIONS * D_MODEL), const, pipeline_mode=pl.Buffered(1)),
            pl.BlockSpec((1, D_MODEL), const),
            pl.BlockSpec((1, D_MODEL), const),
            pl.BlockSpec((1, D_MODEL), const),
            pl.BlockSpec((D_MODEL, LANES), const),
            pl.BlockSpec((2 * LANES, D_MODEL), const),
        ],
        out_specs=out_specs,
        out_shape=out_shape,
        compiler_params=pltpu.CompilerParams(
            dimension_semantics=("arbitrary",), vmem_limit_bytes=VMEM_LIMIT),
        name="inproj",
    )(x, gmix, w_in, gv, gq_t, gk_t, seg, segt2)


def _attn_prompt_kernel(q_ref, k_ref, v_ref, bias_ref, o_ref, *, seq):
    lane = lax.broadcasted_iota(jnp.int32, (1, LANES), 1)
    low_head = lane < HEAD_DIM
    zero = jnp.zeros((), BF16)

    def block(qb, shift):
        q_start = pl.multiple_of(qb * Q_BLOCK, Q_BLOCK)
        k_start = pl.multiple_of((qb + shift) * Q_BLOCK - (K_WINDOW - Q_BLOCK), Q_BLOCK)
        q = q_ref[pl.ds(q_start, Q_BLOCK), :]
        k = k_ref[pl.ds(k_start, K_WINDOW), :]
        v = v_ref[pl.ds(k_start, K_WINDOW), :]
        q2 = jnp.concatenate([jnp.where(low_head, q, zero), jnp.where(low_head, zero, q)], axis=0)
        bias = bias_ref[:, shift * Q_BLOCK:shift * Q_BLOCK + K_WINDOW]
        s = lax.dot_general(q2, k, _NT, preferred_element_type=F32) + bias
        m = jnp.max(s, axis=-1, keepdims=True)
        p = jnp.exp(s - m)
        l = jnp.sum(p, axis=-1, keepdims=True)
        o = jnp.dot(p.astype(BF16), v, preferred_element_type=F32) * (1.0 / l)
        o_ref[pl.ds(q_start, Q_BLOCK), :] = jnp.where(low_head, o[:Q_BLOCK], o[Q_BLOCK:]).astype(BF16)

    for qb in range(N_SHIFT - 1):
        block(qb, N_SHIFT - 1 - qb)

    def steady(qb, carry):
        block(qb, 0)
        return carry

    lax.fori_loop(N_SHIFT - 1, seq // Q_BLOCK, steady, 0, unroll=ATTN_UNROLL)


def _attn_prompt(q, k, v, bias_tab, *, batch, seq):
    q3, k3, v3 = (a.reshape(batch, seq, D_MODEL) for a in (q, k, v))
    head_pair = lambda hp, b: (b, 0, hp)
    spec = pl.BlockSpec((None, seq, LANES), head_pair)
    out = pl.pallas_call(
        functools.partial(_attn_prompt_kernel, seq=seq),
        grid=(ATTN_HEADS // 2, batch),
        in_specs=[spec, spec, spec,
                  pl.BlockSpec((None, 2 * Q_BLOCK, BIAS_SPAN), lambda hp, b: (hp, 0, 0))],
        out_specs=spec,
        out_shape=jax.ShapeDtypeStruct((batch, seq, D_MODEL), BF16),
        compiler_params=pltpu.CompilerParams(
            dimension_semantics=("arbitrary", "arbitrary"), vmem_limit_bytes=VMEM_LIMIT),
        name="attn_prompt",
    )(q3, k3, v3, bias_tab)
    return out.reshape(batch * seq, D_MODEL)


def _rel_bias_rows(rel_bias, q0, k0, rows, ncols):
    dmax = q0 - k0 + rows - 1
    total = rows + ncols - 1
    dmin = dmax - total + 1
    n_lo = min(max(-MAX_REL - dmin, 0), total)
    n_hi = min(max(dmax - MAX_REL, 0), total)
    mid0 = dmin + n_lo + MAX_REL
    asc = jnp.concatenate([jnp.repeat(rel_bias[:, :1], n_lo, axis=1),
                           rel_bias[:, mid0:mid0 + total - n_lo - n_hi],
                           jnp.repeat(rel_bias[:, -1:], n_hi, axis=1)], axis=1).astype(F32)
    desc = asc[:, ::-1]
    heads = desc.shape[0]
    period = jnp.concatenate([desc[:, rows - 1:], jnp.zeros((heads, 1), F32), desc[:, :rows - 1]], axis=1)
    flat = jnp.tile(period, (1, rows))[:, :rows * total]
    return flat.reshape(heads, rows, total)[:, :, :ncols]


def _band_ok(q_pos, k_pos):
    q_pos, k_pos = q_pos[:, None], k_pos[None, :]
    return (k_pos >= 0) & (k_pos // CHUNK <= q_pos // CHUNK) & (k_pos // CHUNK >= q_pos // CHUNK - N_PAST_CHUNKS)


def _prompt_bias_table(rel_bias):
    back = N_PAST_CHUNKS * CHUNK
    ok = _band_ok(back + np.arange(Q_BLOCK), np.arange(BIAS_SPAN))
    ext = jnp.where(ok[None], _rel_bias_rows(rel_bias, back, 0, Q_BLOCK, BIAS_SPAN), NEG)
    return ext.reshape(ATTN_HEADS // 2, 2 * Q_BLOCK, BIAS_SPAN)


def _attn_sample_kernel(q_ref, k_ref, v_ref, ck_ref, cv_ref, bc_ref, bn_ref, o_ref):
    lane = lax.broadcasted_iota(jnp.int32, (1, LANES), 1)
    low_head = lane < HEAD_DIM
    for hp in range(ATTN_HEADS // 2):
        cols = slice(hp * LANES, (hp + 1) * LANES)
        q = q_ref[:, cols]
        kn = k_ref[:, cols]
        vn = v_ref[:, cols]
        kc = ck_ref[:, cols]
        vc = cv_ref[:, cols]
        out = None
        for e in range(2):
            mine = low_head if e == 0 else jnp.logical_not(low_head)
            qe = jnp.where(mine, q, 0.0)
            s_c = _mm(qe, kc, True, _NT) + bc_ref[2 * hp + e]
            s_n = _mm(qe, kn, True, _NT) + bn_ref[2 * hp + e]
            m = jnp.maximum(jnp.max(s_c, axis=-1, keepdims=True), jnp.max(s_n, axis=-1, keepdims=True))
            p_c = jnp.exp(s_c - m)
            p_n = jnp.exp(s_n - m)
            l = jnp.sum(p_c, axis=-1, keepdims=True) + jnp.sum(p_n, axis=-1, keepdims=True)
            o = _mm(p_c, jnp.where(mine, vc, 0.0), True) + _mm(p_n, jnp.where(mine, vn, 0.0), True)
            o = o * (1.0 / l)
            out = o if out is None else out + o
        o_ref[:, cols] = out


def _attn_sample(q, k, v, cache_k, cache_v, bias_c, bias_n, *, batch, rows):
    lc = cache_k.shape[1]
    new = pl.BlockSpec((rows, D_MODEL), lambda b: (b, 0))
    cache = pl.BlockSpec((None, lc, D_MODEL), lambda b: (b, 0, 0))
    return pl.pallas_call(
        _attn_sample_kernel,
        grid=(batch,),
        in_specs=[new, new, new, cache, cache,
                  pl.BlockSpec((ATTN_HEADS, rows, lc), lambda b: (0, 0, 0)),
                  pl.BlockSpec((ATTN_HEADS, rows, rows), lambda b: (0, 0, 0))],
        out_specs=new,
        out_shape=jax.ShapeDtypeStruct((batch * rows, D_MODEL), F32),
        compiler_params=pltpu.CompilerParams(
            dimension_semantics=("arbitrary",), vmem_limit_bytes=VMEM_LIMIT),
        name="attn_sample",
    )(q, k, v, cache_k, cache_v, bias_c, bias_n)


def _sample_bias_tables(rel_bias, rows, lc, past_len):
    q_pos = past_len + np.arange(rows)
    c_pos = past_len - lc + np.arange(lc)
    bias_c = jnp.where(_band_ok(q_pos, c_pos)[None], _rel_bias_rows(rel_bias, past_len, past_len - lc, rows, lc), NEG)
    bias_n = jnp.where(_band_ok(q_pos, q_pos)[None], _rel_bias_rows(rel_bias, past_len, past_len, rows, rows), NEG)
    return bias_c, bias_n


def _route_rows(x1, gffn_ref, wrgt_ref, brgt_ref, upper_ref, gidx_ref, rank_ref, base_ref):
    tm = x1.shape[0]

    @pl.when(pl.program_id(0) == 0)
    def _():
        base_ref[...] = jnp.zeros_like(base_ref)

    h = _rms(x1, gffn_ref[...]).astype(BF16)
    lgt = lax.dot_general(wrgt_ref[...], h, _NT, preferred_element_type=F32) + brgt_ref[...]
    sub = lax.broadcasted_iota(jnp.int32, (ROUTE_ROWS, tm), 0).astype(F32)
    top = jnp.max(lgt, axis=0, keepdims=True)
    gi = jnp.min(jnp.where(lgt == top, sub, float(ROUTE_ROWS)), axis=0, keepdims=True)
    onehot = jnp.where(sub == gi, 1.0, 0.0)
    before = jnp.dot(onehot.astype(BF16), upper_ref[...], preferred_element_type=F32)
    base = base_ref[...]
    rank = jnp.sum(onehot * (before + base[:, :1]), axis=0, keepdims=True)
    base_ref[...] = base + jnp.sum(onehot, axis=1, keepdims=True)
    gidx_ref[...] = gi.astype(jnp.int32)
    rank_ref[...] = rank.astype(jnp.int32)


def _merge_kernel(x_ref, u_ref, vn_ref, sga_ref, sgb_ref, ob_ref, ws_ref, bs_ref, wb_ref, wout_ref,
                  *rest, tm, precise, route):
    if route:
        gffn_ref, wrgt_ref, brgt_ref, upper_ref, x1_ref, gidx_ref, rank_ref, oa_ref, base_ref = rest
    else:
        x1_ref, oa_ref = rest
    act = _act_dtype(precise)
    for c in range(tm // GMLP_CHUNK):
        rows = slice(c * GMLP_CHUNK, (c + 1) * GMLP_CHUNK)
        for g in range(GMLP_GROUPS):
            cols = slice(g * GMLP_GROUP_DIM, (g + 1) * GMLP_GROUP_DIM)
            mix = _mm(ws_ref[g], vn_ref[rows, cols], precise) + bs_ref[:, cols]
            oa_ref[rows, cols] = (u_ref[rows, cols].astype(F32) * mix).astype(act)
    b_a = _mm(oa_ref[...], wb_ref[0], precise)
    b_b = _mm(ob_ref[...], wb_ref[1], precise)
    m = sga_ref[...].astype(F32) * b_a + sgb_ref[...].astype(F32) * b_b
    x1 = x_ref[...] + _mm(m, wout_ref[...], precise)
    x1_ref[...] = x1
    if route:
        _route_rows(x1, gffn_ref, wrgt_ref, brgt_ref, upper_ref, gidx_ref, rank_ref, base_ref)


def _merge(x, u, vn, sga, sgb, ob, ws, bs_full, wb, wout, route_params=None, *, tm, precise):
    t = x.shape[0]
    n_tiles = t // tm
    route = route_params is not None
    row = pl.BlockSpec((tm, D_MODEL), lambda i: (i, 0))
    const = lambda i: (0, 0)
    in_specs = [row, row, row, row, row, row,
                pl.BlockSpec((GMLP_GROUPS, GMLP_CHUNK, GMLP_CHUNK), lambda i: (0, 0, 0)),
                pl.BlockSpec((GMLP_CHUNK, D_MODEL), const),
                pl.BlockSpec((2, D_MODEL, D_MODEL), lambda i: (0, 0, 0)),
                pl.BlockSpec((D_MODEL, D_MODEL), const)]
    out_specs = [row]
    out_shape = [jax.ShapeDtypeStruct((t, D_MODEL), F32)]
    scratch = [pltpu.VMEM((tm, D_MODEL), _act_dtype(precise))]
    args = [x, u, vn, sga, sgb, ob, ws, bs_full, wb, wout]
    if route:
        in_specs += [pl.BlockSpec((1, D_MODEL), const), pl.BlockSpec((ROUTE_ROWS, D_MODEL), const),
                     pl.BlockSpec((ROUTE_ROWS, 1), const), pl.BlockSpec((tm, tm), const)]
        tok_row = pl.BlockSpec((None, 1, tm), lambda i: (i, 0, 0))
        out_specs += [tok_row, tok_row]
        out_shape += [jax.ShapeDtypeStruct((n_tiles, 1, tm), jnp.int32)] * 2
        scratch += [pltpu.VMEM((ROUTE_ROWS, LANES), F32)]
        args += list(route_params)
    return pl.pallas_call(
        functools.partial(_merge_kernel, tm=tm, precise=precise, route=route),
        grid=(n_tiles,),
        in_specs=in_specs,
        out_specs=out_specs,
        out_shape=out_shape,
        scratch_shapes=scratch,
        compiler_params=pltpu.CompilerParams(
            dimension_semantics=("arbitrary",), vmem_limit_bytes=VMEM_LIMIT),
        name="merge",
    )(*args)


def _first_index_of_max(vals, lane):
    top = jnp.max(vals, axis=-1, keepdims=True)
    idx = jnp.min(jnp.where(vals == top, lane, float(LANES)), axis=-1, keepdims=True)
    return top, idx


def _top2_weights(le_in_group, lane, pg_top):
    t1, i1 = _first_index_of_max(le_in_group, lane)
    t2, i2 = _first_index_of_max(jnp.where(lane == i1, -jnp.inf, le_in_group), lane)
    e2 = jnp.exp(t2 - t1)
    den = 1.0 + e2
    return jnp.where(lane == i1, pg_top * (1.0 / den), jnp.where(lane == i2, pg_top * (e2 / den), 0.0))


def _group_experts(h, comb, lane_i, g, wg_ref, wu_ref, wd_ref):
    hids = []
    for j in range(EXPERTS_PER_GROUP):
        w_tok = jnp.sum(jnp.where(lane_i == g * EXPERTS_PER_GROUP + j, comb, 0.0), axis=-1, keepdims=True)
        a = jnp.dot(h, wg_ref[j], preferred_element_type=F32)
        b = jnp.dot(h, wu_ref[j], preferred_element_type=F32)
        hids.append((jax.nn.silu(a) * b * w_tok).astype(BF16))
    hid = jnp.concatenate(hids, axis=1)
    return jnp.dot(hid, wd_ref[...].reshape(EXPERTS_PER_GROUP * D_EXPERT, D_MODEL), preferred_element_type=F32)


def _embedding_gate(x2, p_ref, gple_ref, wpg_ref, wpe_ref):
    gate = jax.nn.sigmoid(jnp.dot(_rms(x2, gple_ref[...]).astype(BF16), wpg_ref[...], preferred_element_type=F32))
    pe = jnp.dot(p_ref[...].astype(BF16), wpe_ref[...], preferred_element_type=F32)
    return x2 + gate * pe


def _gather_rows(src_hbm, index_ref, first, buf, slot, sem):
    def issue(r, carry):
        pltpu.make_async_copy(src_hbm.at[index_ref[first + r]], buf.at[slot, r], sem.at[slot]).start()
        return carry
    lax.fori_loop(0, buf.shape[1], issue, 0, unroll=DMA_UNROLL)


def _wait_rows(src_hbm, buf, slot, sem):
    pltpu.make_async_copy(src_hbm.at[pl.ds(0, buf.shape[1])], buf.at[slot], sem.at[slot]).wait()


def _experts_kernel(src_row_ref, tile_group_ref, n_valid_ref, x1_hbm, gffn_ref, wrg_ref, brg_ref, wre_ref, bre_ref,
                    wg_ref, wu_ref, wd_ref, ys_ref, xbuf, sems):
    i = pl.program_id(0)
    n_valid = n_valid_ref[0]

    @pl.when(i == 0)
    def _():
        _gather_rows(x1_hbm, src_row_ref, 0, xbuf, 0, sems)

    @pl.when(i + 1 < n_valid)
    def _():
        _gather_rows(x1_hbm, src_row_ref, (i + 1) * ROUTE_TILE, xbuf, (i + 1) % 2, sems)

    @pl.when(i < n_valid)
    def _():
        slot = i % 2
        _wait_rows(x1_hbm, xbuf, slot, sems)
        g = tile_group_ref[i]
        lane_i = lax.broadcasted_iota(jnp.int32, (1, LANES), 1)
        lane = lane_i.astype(F32)
        h = _rms(xbuf[slot], gffn_ref[...]).astype(BF16)
        lg = _mm(h, wrg_ref[...], False) + brg_ref[...]
        eg = jnp.exp(lg - jnp.max(lg, axis=-1, keepdims=True))
        pg = jnp.sum(jnp.where(lane_i == g, eg, 0.0), axis=-1, keepdims=True) / jnp.sum(eg, axis=-1, keepdims=True)
        le = _mm(h, wre_ref[...], False) + bre_ref[...]
        comb = _top2_weights(jnp.where(lane_i // EXPERTS_PER_GROUP == g, le, -jnp.inf), lane, pg)
        ys_ref[...] = _group_experts(h, comb, lane_i, g, wg_ref, wu_ref, wd_ref)

    @pl.when(i >= n_valid)
    def _():
        ys_ref[...] = jnp.zeros_like(ys_ref)


def _experts(src_row, tile_group, n_valid, x1, gffn, wrg, brg, wre, bre, wg, wu, wd, *, n_tiles):
    const = lambda i, sr, tg, nv: (0, 0)
    group = lambda i, sr, tg, nv: (tg[i], 0, 0)
    return pl.pallas_call(
        _experts_kernel,
        grid_spec=pltpu.PrefetchScalarGridSpec(
            num_scalar_prefetch=3,
            grid=(n_tiles,),
            in_specs=[
                pl.BlockSpec(memory_space=pl.ANY),
                pl.BlockSpec((1, D_MODEL), const),
                pl.BlockSpec((D_MODEL, LANES), const),
                pl.BlockSpec((1, LANES), const),
                pl.BlockSpec((D_MODEL, LANES), const),
                pl.BlockSpec((1, LANES), const),
                pl.BlockSpec((EXPERTS_PER_GROUP, D_MODEL, D_EXPERT), group),
                pl.BlockSpec((EXPERTS_PER_GROUP, D_MODEL, D_EXPERT), group),
                pl.BlockSpec((EXPERTS_PER_GROUP, D_EXPERT, D_MODEL), group),
            ],
            out_specs=pl.BlockSpec((ROUTE_TILE, D_MODEL), lambda i, sr, tg, nv: (i, 0)),
            scratch_shapes=[pltpu.VMEM((2, ROUTE_TILE, D_MODEL), F32), pltpu.SemaphoreType.DMA((2,))]),
        out_shape=jax.ShapeDtypeStruct((n_tiles * ROUTE_TILE, D_MODEL), F32),
        compiler_params=pltpu.CompilerParams(
            dimension_semantics=("arbitrary",), vmem_limit_bytes=VMEM_LIMIT),
        name="experts",
    )(src_row, tile_group, n_valid, x1, gffn, wrg, brg, wre, bre, wg, wu, wd)


def _combine_kernel(pos_ref, x1_ref, p_ref, gple_ref, wpg_ref, wpe_ref, ys_hbm, out_ref, ybuf, sems, *, tm):
    i = pl.program_id(0)

    @pl.when(i == 0)
    def _():
        _gather_rows(ys_hbm, pos_ref, 0, ybuf, 0, sems)

    @pl.when(i + 1 < pl.num_programs(0))
    def _():
        _gather_rows(ys_hbm, pos_ref, (i + 1) * tm, ybuf, (i + 1) % 2, sems)

    slot = i % 2
    _wait_rows(ys_hbm, ybuf, slot, sems)
    out_ref[...] = _embedding_gate(x1_ref[...] + ybuf[slot], p_ref, gple_ref, wpg_ref, wpe_ref)


def _combine(pos, x1, p, gple, wpg, wpe, ys, *, tm):
    t = x1.shape[0]
    row = lambda i, pos: (i, 0)
    const = lambda i, pos: (0, 0)
    return pl.pallas_call(
        functools.partial(_combine_kernel, tm=tm),
        grid_spec=pltpu.PrefetchScalarGridSpec(
            num_scalar_prefetch=1,
            grid=(t // tm,),
            in_specs=[pl.BlockSpec((tm, D_MODEL), row),
                      pl.BlockSpec((tm, PLE_DIM), row),
                      pl.BlockSpec((1, D_MODEL), const),
                      pl.BlockSpec((D_MODEL, D_MODEL), const),
                      pl.BlockSpec((PLE_DIM, D_MODEL), const),
                      pl.BlockSpec(memory_space=pl.ANY)],
            out_specs=pl.BlockSpec((tm, D_MODEL), row),
            scratch_shapes=[pltpu.VMEM((2, tm, D_MODEL), F32), pltpu.SemaphoreType.DMA((2,))]),
        out_shape=jax.ShapeDtypeStruct((t, D_MODEL), F32),
        compiler_params=pltpu.CompilerParams(
            dimension_semantics=("arbitrary",), vmem_limit_bytes=VMEM_LIMIT),
        name="combine",
    )(pos, x1, p, gple, wpg, wpe, ys)


def _route_plan(gidx, rank, *, n_tiles):
    t = gidx.shape[0]
    groups = jnp.arange(N_GROUPS, dtype=jnp.int32)
    member = gidx[None, :] == groups[:, None]
    counts = jnp.sum(member, axis=1, dtype=jnp.int32)
    padded = (counts + ROUTE_TILE - 1) // ROUTE_TILE * ROUTE_TILE
    ends = jnp.cumsum(padded)
    offsets = ends - padded
    pos = rank + jnp.sum(jnp.where(member, offsets[:, None], 0), axis=0, dtype=jnp.int32)
    src_row = jnp.zeros((n_tiles * ROUTE_TILE,), jnp.int32).at[pos].set(
        jnp.arange(t, dtype=jnp.int32), unique_indices=True, indices_are_sorted=False)
    n_valid = jnp.maximum(ends[-1] // ROUTE_TILE, 1)
    tile_start = jnp.minimum(jnp.arange(n_tiles, dtype=jnp.int32), n_valid - 1) * ROUTE_TILE
    tile_group = jnp.minimum(jnp.sum(tile_start[:, None] >= ends[None, :], axis=1, dtype=jnp.int32), N_GROUPS - 1)
    return pos, src_row, tile_group, n_valid.reshape(1).astype(jnp.int32)


def _channel_kernel(x1_ref, p_ref, gffn_ref, wrg_ref, brg_ref, wre_ref, bre_ref,
                    wg_ref, wu_ref, wd_ref, gple_ref, wpg_ref, wpe_ref,
                    out_ref, h_ref, comb_ref, acc_ref, *, precise_router):
    g = pl.program_id(1)
    lane_i = lax.broadcasted_iota(jnp.int32, (1, LANES), 1)
    lane = lane_i.astype(F32)

    @pl.when(g == 0)
    def _():
        h = _rms(x1_ref[...], gffn_ref[...])
        h_ref[...] = h.astype(BF16)
        lg = _mm(h, wrg_ref[...], precise_router) + brg_ref[...]
        top_g, g_idx = _first_index_of_max(lg, lane)
        pg_top = 1.0 / jnp.sum(jnp.exp(lg - top_g), axis=-1, keepdims=True)
        le = _mm(h, wre_ref[...], precise_router) + bre_ref[...]
        group_of_lane = (lane_i // EXPERTS_PER_GROUP).astype(F32)
        comb_ref[...] = _top2_weights(jnp.where(group_of_lane == g_idx, le, -jnp.inf), lane, pg_top)

    y = _group_experts(h_ref[...], comb_ref[...], lane_i, g, wg_ref, wu_ref, wd_ref)

    @pl.when(g == 0)
    def _():
        acc_ref[...] = y

    @pl.when(g > 0)
    def _():
        acc_ref[...] += y

    @pl.when(g == N_GROUPS - 1)
    def _():
        out_ref[...] = _embedding_gate(x1_ref[...] + acc_ref[...], p_ref, gple_ref, wpg_ref, wpe_ref)


def _channel(x1, p, gffn, wrg, brg, wre, bre, wg, wu, wd, gple, wpg, wpe, *, tm, precise_router):
    t = x1.shape[0]
    row = lambda i, g: (i, 0)
    const2 = lambda i, g: (0, 0)
    group = lambda i, g: (g, 0, 0)
    return pl.pallas_call(
        functools.partial(_channel_kernel, precise_router=precise_router),
        grid=(t // tm, N_GROUPS),
        in_specs=[
            pl.BlockSpec((tm, D_MODEL), row),
            pl.BlockSpec((tm, PLE_DIM), row),
            pl.BlockSpec((1, D_MODEL), const2),
            pl.BlockSpec((D_MODEL, LANES), const2),
            pl.BlockSpec((1, LANES), const2),
            pl.BlockSpec((D_MODEL, LANES), const2),
            pl.BlockSpec((1, LANES), const2),
            pl.BlockSpec((EXPERTS_PER_GROUP, D_MODEL, D_EXPERT), group),
            pl.BlockSpec((EXPERTS_PER_GROUP, D_MODEL, D_EXPERT), group),
            pl.BlockSpec((EXPERTS_PER_GROUP, D_EXPERT, D_MODEL), group),
            pl.BlockSpec((1, D_MODEL), const2),
            pl.BlockSpec((D_MODEL, D_MODEL), const2),
            pl.BlockSpec((PLE_DIM, D_MODEL), const2),
        ],
        out_specs=pl.BlockSpec((tm, D_MODEL), row),
        out_shape=jax.ShapeDtypeStruct((t, D_MODEL), F32),
        scratch_shapes=[pltpu.VMEM((tm, D_MODEL), BF16),
                        pltpu.VMEM((tm, LANES), F32),
                        pltpu.VMEM((tm, D_MODEL), F32)],
        compiler_params=pltpu.CompilerParams(
            dimension_semantics=("arbitrary", "arbitrary"), vmem_limit_bytes=VMEM_LIMIT),
        name="channel",
    )(x1, p, gffn, wrg, brg, wre, bre, wg, wu, wd, gple, wpg, wpe)


def _pad_lanes(a, fill):
    return jnp.pad(a, ((0, 0), (0, LANES - a.shape[1])), constant_values=fill)


def kernel(x_prompt, x_sample, cache_attn_k, cache_attn_v, p_prompt, p_sample, g_mix, w_in, g_gmlp_v, w_gmlp_s, b_gmlp_s, g_q, g_k, rel_bias, w_branch, w_out, g_ffn, w_router_group, b_router_group, w_router_expert, b_router_expert, w_exp_gate, w_exp_up, w_exp_down, g_ple, w_ple_gate, w_ple_proj):
    depth = g_mix.shape[0]
    batch, seq, _ = x_prompt.shape
    dec_batch, dec_seq, _ = x_sample.shape
    lc = cache_attn_k.shape[2]
    keep = min(N_PAST_CHUNKS * CHUNK, seq)
    tm = 512
    assert depth == 1 and keep == tm and seq % tm == 0 and seq % Q_BLOCK == 0
    assert dec_seq <= GMLP_CHUNK and GMLP_CHUNK % dec_seq == 0 and dec_batch * dec_seq == GMLP_CHUNK
    l = 0
    t_s = dec_batch * dec_seq

    row = lambda a: a.reshape(1, -1).astype(F32)
    w_in_b = w_in[l].astype(BF16)
    gq_t = row(jnp.tile(g_q[l], ATTN_HEADS))
    gk_t = row(jnp.tile(g_k[l], ATTN_HEADS))
    head_of_lane = jnp.arange(D_MODEL, dtype=jnp.int32) // HEAD_DIM
    seg = (head_of_lane[:, None] == jnp.arange(LANES, dtype=jnp.int32)[None, :]).astype(BF16)
    segt2 = jnp.concatenate([seg.T, seg.T], axis=0)
    ws_tril = jnp.tril(w_gmlp_s[l])
    ws_p = ws_tril.astype(BF16)
    bs_p = jnp.repeat(b_gmlp_s[l].T, GMLP_GROUP_DIM, axis=1).astype(F32)
    eye = jnp.eye(dec_batch, dtype=F32)
    ws_s = jnp.einsum('ab,gts->gatbs', eye, ws_tril[:, :dec_seq, :dec_seq]).reshape(
        GMLP_GROUPS, t_s, t_s)
    bs_s = jnp.tile(jnp.repeat(b_gmlp_s[l][:, :dec_seq].T, GMLP_GROUP_DIM, axis=1), (dec_batch, 1)).astype(F32)
    wb_b = w_branch[l].astype(BF16)
    wout_b = w_out[l].astype(BF16)
    wrg = _pad_lanes(w_router_group[l], 0.0)
    brg = _pad_lanes(row(b_router_group[l]), NEG)
    wre = _pad_lanes(w_router_expert[l], 0.0)
    bre = _pad_lanes(row(b_router_expert[l]), 0.0)
    wg_b = w_exp_gate[l].astype(BF16)
    wu_b = w_exp_up[l].astype(BF16)
    wd_b = w_exp_down[l].astype(BF16)
    wpg_b = w_ple_gate[l].astype(BF16)
    wpe_b = w_ple_proj[l].astype(BF16)
    bias_prompt = _prompt_bias_table(rel_bias[l])
    bias_c, bias_n = _sample_bias_tables(rel_bias[l], dec_seq, lc, PAST_LEN)

    wrgt = jnp.pad(w_router_group[l].T, ((0, ROUTE_ROWS - N_GROUPS), (0, 0))).astype(BF16)
    brgt = jnp.pad(b_router_group[l].astype(F32), (0, ROUTE_ROWS - N_GROUPS), constant_values=NEG).reshape(ROUTE_ROWS, 1)
    upper = jnp.asarray(np.triu(np.ones((tm, tm), np.float32), 1), BF16)

    xp = x_prompt.reshape(batch * seq, D_MODEL)
    u, vn, q, k, v, sga, sgb, kst, vst = _inproj(
        xp, row(g_mix[l]), w_in_b, row(g_gmlp_v[l]), gq_t, gk_t, seg, segt2,
        tm=tm, tiles_per_batch=seq // tm, emit_vn_state=False, precise=False)
    ob = _attn_prompt(q, k, v, bias_prompt, batch=batch, seq=seq)
    x1, gidx, rank = _merge(xp, u, vn, sga, sgb, ob, ws_p, bs_p, wb_b, wout_b,
                            (row(g_ffn[l]), wrgt, brgt, upper), tm=tm, precise=False)
    n_route_tiles = batch * seq // ROUTE_TILE + N_GROUPS
    pos, src_row, tile_group, n_valid = _route_plan(gidx.reshape(-1), rank.reshape(-1), n_tiles=n_route_tiles)
    y_sorted = _experts(src_row, tile_group, n_valid, x1, row(g_ffn[l]), wrg.astype(BF16), brg, wre.astype(BF16), bre,
                        wg_b, wu_b, wd_b, n_tiles=n_route_tiles)
    yp = _combine(pos, x1, p_prompt[l].reshape(batch * seq, PLE_DIM), row(g_ple[l]), wpg_b, wpe_b, y_sorted, tm=tm)

    xs = x_sample.reshape(t_s, D_MODEL)
    u, vn, q, k, v, sga, sgb, kss, vss, gvs = _inproj(
        xs, row(g_mix[l]), w_in[l], row(g_gmlp_v[l]), gq_t, gk_t, seg, segt2,
        tm=t_s, tiles_per_batch=1, emit_vn_state=True, precise=True)
    ob = _attn_sample(q, k, v, cache_attn_k[l].reshape(dec_batch, lc, D_MODEL),
                      cache_attn_v[l].reshape(dec_batch, lc, D_MODEL), bias_c, bias_n,
                      batch=dec_batch, rows=dec_seq)
    x1, = _merge(xs, u, vn, sga, sgb, ob, ws_s, bs_s, w_branch[l], w_out[l], tm=t_s, precise=True)
    ys = _channel(x1, p_sample[l].reshape(t_s, PLE_DIM), row(g_ffn[l]), wrg, brg, wre, bre, wg_b, wu_b, wd_b,
                  row(g_ple[l]), wpg_b, wpe_b, tm=t_s, precise_router=True)

    return (yp.reshape(batch, seq, D_MODEL),
            ys.reshape(dec_batch, dec_seq, D_MODEL),
            kst.reshape(1, batch, keep, ATTN_HEADS, HEAD_DIM),
            vst.reshape(1, batch, keep, ATTN_HEADS, HEAD_DIM),
            kss.reshape(1, dec_batch, dec_seq, ATTN_HEADS, HEAD_DIM),
            vss.reshape(1, dec_batch, dec_seq, ATTN_HEADS, HEAD_DIM),
            gvs.reshape(1, dec_batch, dec_seq, D_MODEL))
```

```python
import functools

import jax
import jax.numpy as jnp
import numpy as np
from jax import lax
from jax.experimental import pallas as pl
from jax.experimental.pallas import tpu as pltpu

F32 = jnp.float32
BF16 = jnp.bfloat16

D_MODEL = 1024
CHUNK = 64
N_PAST_CHUNKS = 8
HEAD_DIM = 64
ATTN_HEADS = D_MODEL // HEAD_DIM
MAX_REL = 128
GMLP_CHUNK = 128
GMLP_GROUPS = 4
GMLP_GROUP_DIM = D_MODEL // GMLP_GROUPS
N_SECTIONS = 7
N_GROUPS = 4
EXPERTS_PER_GROUP = 8
N_EXPERTS = N_GROUPS * EXPERTS_PER_GROUP
D_EXPERT = D_MODEL // 4
PLE_DIM = 256
EPS = 1e-6
NEG = -1e30
PAST_LEN = 1024

LANES = 128
Q_BLOCK = 2 * CHUNK
K_WINDOW = Q_BLOCK + N_PAST_CHUNKS * CHUNK
N_SHIFT = N_PAST_CHUNKS * CHUNK // Q_BLOCK + 1
BIAS_SPAN = K_WINDOW + (N_SHIFT - 1) * Q_BLOCK
ROUTE_ROWS = 16
ROUTE_TILE = 512
DMA_UNROLL = 8
ATTN_UNROLL = 6
VMEM_LIMIT = 56 * 1024 * 1024


def _rms(x, gain):
    return x * lax.rsqrt(jnp.mean(x * x, axis=-1, keepdims=True) + EPS) * gain


def _act_dtype(precise):
    return F32 if precise else BF16


def _mm(a, b, precise, dims=None):
    dims = dims or (((a.ndim - 1,), (0,)), ((), ()))
    if precise:
        return lax.dot_general(a.astype(F32), b.astype(F32), dims, precision=lax.Precision.HIGHEST,
                               preferred_element_type=F32)
    return lax.dot_general(a.astype(BF16), b.astype(BF16), dims, preferred_element_type=F32)


_NT = (((1,), (1,)), ((), ()))


def _head_rms(z, gain, seg, segt2, precise):
    ssum = _mm(z * z, seg, precise)
    inv = lax.rsqrt(ssum * (1.0 / HEAD_DIM) + EPS)
    if precise:
        full = _mm(inv, segt2[:LANES], True)
    else:
        hi = inv.astype(BF16)
        lo = (inv - hi.astype(F32)).astype(BF16)
        full = _mm(jnp.concatenate([hi, lo], axis=1), segt2, False)
    return z * full * gain


def _inproj_kernel(x_ref, gmix_ref, w_ref, gv_ref, gq_ref, gk_ref, seg_ref, segt2_ref,
                   u_ref, vn_ref, q_ref, k_ref, v_ref, sga_ref, sgb_ref,
                   kst_ref, vst_ref, *maybe_vnst_ref, tiles_per_batch, precise):
    act = _act_dtype(precise)
    h = _rms(x_ref[...], gmix_ref[...]).astype(act)

    def section(s):
        return _mm(h, w_ref[:, s * D_MODEL:(s + 1) * D_MODEL], precise)

    is_state_tile = (pl.program_id(0) % tiles_per_batch) == tiles_per_batch - 1

    u_ref[...] = jax.nn.gelu(section(0)).astype(act)

    vn = _rms(jax.nn.gelu(section(1)), gv_ref[...])
    vn_ref[...] = vn.astype(act)
    if maybe_vnst_ref:
        maybe_vnst_ref[0][...] = vn

    qn = _head_rms(section(2), gq_ref[...], seg_ref[...], segt2_ref[...], precise)
    q_ref[...] = (qn * (HEAD_DIM ** -0.5)).astype(act)

    kn = _head_rms(section(3), gk_ref[...], seg_ref[...], segt2_ref[...], precise)
    k_ref[...] = kn.astype(act)

    @pl.when(is_state_tile)
    def _():
        kst_ref[...] = kn

    v = section(4)
    v_ref[...] = v.astype(act)

    @pl.when(is_state_tile)
    def _():
        vst_ref[...] = v

    sga_ref[...] = jax.nn.sigmoid(section(5)).astype(act)
    sgb_ref[...] = jax.nn.sigmoid(section(6)).astype(act)


def _inproj(x, gmix, w_in, gv, gq_t, gk_t, seg, segt2, *, tm, tiles_per_batch, emit_vn_state, precise):
    t = x.shape[0]
    n_tiles = t // tm
    n_state = n_tiles // tiles_per_batch
    row = lambda i: (i, 0)
    const = lambda i: (0, 0)
    state = lambda i: (i // tiles_per_batch, 0)
    act = jax.ShapeDtypeStruct((t, D_MODEL), _act_dtype(precise))
    st = jax.ShapeDtypeStruct((n_state * tm, D_MODEL), F32)
    out_shape = [act] * 7 + [st, st] + ([st] if emit_vn_state else [])
    out_specs = [pl.BlockSpec((tm, D_MODEL), row)] * 7 + [pl.BlockSpec((tm, D_MODEL), state)] * (
        3 if emit_vn_state else 2)
    return pl.pallas_call(
        functools.partial(_inproj_kernel, tiles_per_batch=tiles_per_batch, precise=precise),
        grid=(n_tiles,),
        in_specs=[
            pl.BlockSpec((tm, D_MODEL), row),
            pl.BlockSpec((1, D_MODEL), const),
            pl.BlockSpec((D_MODEL, N_SECTIONS * D_MODEL), const, pipeline_mode=pl.Buffered(1)),
            pl.BlockSpec((1, D_MODEL), const),
            pl.BlockSpec((1, D_MODEL), const),
            pl.BlockSpec((1, D_MODEL), const),
            pl.BlockSpec((D_MODEL, LANES), const),
            pl.BlockSpec((2 * LANES, D_MODEL), const),
        ],
        out_specs=out_specs,
        out_shape=out_shape,
        compiler_params=pltpu.CompilerParams(
            dimension_semantics=("arbitrary",), vmem_limit_bytes=VMEM_LIMIT),
        name="inproj",
    )(x, gmix, w_in, gv, gq_t, gk_t, seg, segt2)


def _attn_prompt_kernel(q_ref, k_ref, v_ref, bias_ref, o_ref, *, seq):
    lane = lax.broadcasted_iota(jnp.int32, (1, LANES), 1)
    low_head = lane < HEAD_DIM
    zero = jnp.zeros((), BF16)

    def block(qb, shift):
        q_start = pl.multiple_of(qb * Q_BLOCK, Q_BLOCK)
        k_start = pl.multiple_of((qb + shift) * Q_BLOCK - (K_WINDOW - Q_BLOCK), Q_BLOCK)
        q = q_ref[pl.ds(q_start, Q_BLOCK), :]
        k = k_ref[pl.ds(k_start, K_WINDOW), :]
        v = v_ref[pl.ds(k_start, K_WINDOW), :]
        q2 = jnp.concatenate([jnp.where(low_head, q, zero), jnp.where(low_head, zero, q)], axis=0)
        bias = bias_ref[:, shift * Q_BLOCK:shift * Q_BLOCK + K_WINDOW]
        s = lax.dot_general(q2, k, _NT, preferred_element_type=F32) + bias
        m = jnp.max(s, axis=-1, keepdims=True)
        p = jnp.exp(s - m)
        l = jnp.sum(p, axis=-1, keepdims=True)
        o = jnp.dot(p.astype(BF16), v, preferred_element_type=F32) * (1.0 / l)
        o_ref[pl.ds(q_start, Q_BLOCK), :] = jnp.where(low_head, o[:Q_BLOCK], o[Q_BLOCK:]).astype(BF16)

    for qb in range(N_SHIFT - 1):
        block(qb, N_SHIFT - 1 - qb)

    def steady(qb, carry):
        block(qb, 0)
        return carry

    lax.fori_loop(N_SHIFT - 1, seq // Q_BLOCK, steady, 0, unroll=ATTN_UNROLL)


def _attn_prompt(q, k, v, bias_tab, *, batch, seq):
    q3, k3, v3 = (a.reshape(batch, seq, D_MODEL) for a in (q, k, v))
    head_pair = lambda hp, b: (b, 0, hp)
    spec = pl.BlockSpec((None, seq, LANES), head_pair)
    out = pl.pallas_call(
        functools.partial(_attn_prompt_kernel, seq=seq),
        grid=(ATTN_HEADS // 2, batch),
        in_specs=[spec, spec, spec,
                  pl.BlockSpec((None, 2 * Q_BLOCK, BIAS_SPAN), lambda hp, b: (hp, 0, 0))],
        out_specs=spec,
        out_shape=jax.ShapeDtypeStruct((batch, seq, D_MODEL), BF16),
        compiler_params=pltpu.CompilerParams(
            dimension_semantics=("arbitrary", "arbitrary"), vmem_limit_bytes=VMEM_LIMIT),
        name="attn_prompt",
    )(q3, k3, v3, bias_tab)
    return out.reshape(batch * seq, D_MODEL)


def _rel_bias_rows(rel_bias, q0, k0, rows, ncols):
    dmax = q0 - k0 + rows - 1
    total = rows + ncols - 1
    dmin = dmax - total + 1
    n_lo = min(max(-MAX_REL - dmin, 0), total)
    n_hi = min(max(dmax - MAX_REL, 0), total)
    mid0 = dmin + n_lo + MAX_REL
    asc = jnp.concatenate([jnp.repeat(rel_bias[:, :1], n_lo, axis=1),
                           rel_bias[:, mid0:mid0 + total - n_lo - n_hi],
                           jnp.repeat(rel_bias[:, -1:], n_hi, axis=1)], axis=1).astype(F32)
    desc = asc[:, ::-1]
    heads = desc.shape[0]
    period = jnp.concatenate([desc[:, rows - 1:], jnp.zeros((heads, 1), F32), desc[:, :rows - 1]], axis=1)
    flat = jnp.tile(period, (1, rows))[:, :rows * total]
    return flat.reshape(heads, rows, total)[:, :, :ncols]


def _band_ok(q_pos, k_pos):
    q_pos, k_pos = q_pos[:, None], k_pos[None, :]
    return (k_pos >= 0) & (k_pos // CHUNK <= q_pos // CHUNK) & (k_pos // CHUNK >= q_pos // CHUNK - N_PAST_CHUNKS)


def _prompt_bias_table(rel_bias):
    back = N_PAST_CHUNKS * CHUNK
    ok = _band_ok(back + np.arange(Q_BLOCK), np.arange(BIAS_SPAN))
    ext = jnp.where(ok[None], _rel_bias_rows(rel_bias, back, 0, Q_BLOCK, BIAS_SPAN), NEG)
    return ext.reshape(ATTN_HEADS // 2, 2 * Q_BLOCK, BIAS_SPAN)


def _attn_sample_kernel(q_ref, k_ref, v_ref, ck_ref, cv_ref, bc_ref, bn_ref, o_ref):
    lane = lax.broadcasted_iota(jnp.int32, (1, LANES), 1)
    low_head = lane < HEAD_DIM
    for hp in range(ATTN_HEADS // 2):
        cols = slice(hp * LANES, (hp + 1) * LANES)
        q = q_ref[:, cols]
        kn = k_ref[:, cols]
        vn = v_ref[:, cols]
        kc = ck_ref[:, cols]
        vc = cv_ref[:, cols]
        out = None
        for e in range(2):
            mine = low_head if e == 0 else jnp.logical_not(low_head)
            qe = jnp.where(mine, q, 0.0)
            s_c = _mm(qe, kc, True, _NT) + bc_ref[2 * hp + e]
            s_n = _mm(qe, kn, True, _NT) + bn_ref[2 * hp + e]
            m = jnp.maximum(jnp.max(s_c, axis=-1, keepdims=True), jnp.max(s_n, axis=-1, keepdims=True))
            p_c = jnp.exp(s_c - m)
            p_n = jnp.exp(s_n - m)
            l = jnp.sum(p_c, axis=-1, keepdims=True) + jnp.sum(p_n, axis=-1, keepdims=True)
            o = _mm(p_c, jnp.where(mine, vc, 0.0), True) + _mm(p_n, jnp.where(mine, vn, 0.0), True)
            o = o * (1.0 / l)
            out = o if out is None else out + o
        o_ref[:, cols] = out


def _attn_sample(q, k, v, cache_k, cache_v, bias_c, bias_n, *, batch, rows):
    lc = cache_k.shape[1]
    new = pl.BlockSpec((rows, D_MODEL), lambda b: (b, 0))
    cache = pl.BlockSpec((None, lc, D_MODEL), lambda b: (b, 0, 0))
    return pl.pallas_call(
        _attn_sample_kernel,
        grid=(batch,),
        in_specs=[new, new, new, cache, cache,
                  pl.BlockSpec((ATTN_HEADS, rows, lc), lambda b: (0, 0, 0)),
                  pl.BlockSpec((ATTN_HEADS, rows, rows), lambda b: (0, 0, 0))],
        out_specs=new,
        out_shape=jax.ShapeDtypeStruct((batch * rows, D_MODEL), F32),
        compiler_params=pltpu.CompilerParams(
            dimension_semantics=("arbitrary",), vmem_limit_bytes=VMEM_LIMIT),
        name="attn_sample",
    )(q, k, v, cache_k, cache_v, bias_c, bias_n)


def _sample_bias_tables(rel_bias, rows, lc, past_len):
    q_pos = past_len + np.arange(rows)
    c_pos = past_len - lc + np.arange(lc)
    bias_c = jnp.where(_band_ok(q_pos, c_pos)[None], _rel_bias_rows(rel_bias, past_len, past_len - lc, rows, lc), NEG)
    bias_n = jnp.where(_band_ok(q_pos, q_pos)[None], _rel_bias_rows(rel_bias, past_len, past_len, rows, rows), NEG)
    return bias_c, bias_n


def _route_rows(x1, gffn_ref, wrgt_ref, brgt_ref, upper_ref, gidx_ref, rank_ref, base_ref):
    tm = x1.shape[0]

    @pl.when(pl.program_id(0) == 0)
    def _():
        base_ref[...] = jnp.zeros_like(base_ref)

    h = _rms(x1, gffn_ref[...]).astype(BF16)
    lgt = lax.dot_general(wrgt_ref[...], h, _NT, preferred_element_type=F32) + brgt_ref[...]
    sub = lax.broadcasted_iota(jnp.int32, (ROUTE_ROWS, tm), 0).astype(F32)
    top = jnp.max(lgt, axis=0, keepdims=True)
    gi = jnp.min(jnp.where(lgt == top, sub, float(ROUTE_ROWS)), axis=0, keepdims=True)
    onehot = jnp.where(sub == gi, 1.0, 0.0)
    before = jnp.dot(onehot.astype(BF16), upper_ref[...], preferred_element_type=F32)
    base = base_ref[...]
    rank = jnp.sum(onehot * (before + base[:, :1]), axis=0, keepdims=True)
    base_ref[...] = base + jnp.sum(onehot, axis=1, keepdims=True)
    gidx_ref[...] = gi.astype(jnp.int32)
    rank_ref[...] = rank.astype(jnp.int32)


def _merge_kernel(x_ref, u_ref, vn_ref, sga_ref, sgb_ref, ob_ref, ws_ref, bs_ref, wb_ref, wout_ref,
                  *rest, tm, precise, route):
    if route:
        gffn_ref, wrgt_ref, brgt_ref, upper_ref, x1_ref, gidx_ref, rank_ref, oa_ref, base_ref = rest
    else:
        x1_ref, oa_ref = rest
    act = _act_dtype(precise)
    for c in range(tm // GMLP_CHUNK):
        rows = slice(c * GMLP_CHUNK, (c + 1) * GMLP_CHUNK)
        for g in range(GMLP_GROUPS):
            cols = slice(g * GMLP_GROUP_DIM, (g + 1) * GMLP_GROUP_DIM)
            mix = _mm(ws_ref[g], vn_ref[rows, cols], precise) + bs_ref[:, cols]
            oa_ref[rows, cols] = (u_ref[rows, cols].astype(F32) * mix).astype(act)
    b_a = _mm(oa_ref[...], wb_ref[0], precise)
    b_b = _mm(ob_ref[...], wb_ref[1], precise)
    m = sga_ref[...].astype(F32) * b_a + sgb_ref[...].astype(F32) * b_b
    x1 = x_ref[...] + _mm(m, wout_ref[...], precise)
    x1_ref[...] = x1
    if route:
        _route_rows(x1, gffn_ref, wrgt_ref, brgt_ref, upper_ref, gidx_ref, rank_ref, base_ref)


def _merge(x, u, vn, sga, sgb, ob, ws, bs_full, wb, wout, route_params=None, *, tm, precise):
    t = x.shape[0]
    n_tiles = t // tm
    route = route_params is not None
    row = pl.BlockSpec((tm, D_MODEL), lambda i: (i, 0))
    const = lambda i: (0, 0)
    in_specs = [row, row, row, row, row, row,
                pl.BlockSpec((GMLP_GROUPS, GMLP_CHUNK, GMLP_CHUNK), lambda i: (0, 0, 0)),
                pl.BlockSpec((GMLP_CHUNK, D_MODEL), const),
                pl.BlockSpec((2, D_MODEL, D_MODEL), lambda i: (0, 0, 0)),
                pl.BlockSpec((D_MODEL, D_MODEL), const)]
    out_specs = [row]
    out_shape = [jax.ShapeDtypeStruct((t, D_MODEL), F32)]
    scratch = [pltpu.VMEM((tm, D_MODEL), _act_dtype(precise))]
    args = [x, u, vn, sga, sgb, ob, ws, bs_full, wb, wout]
    if route:
        in_specs += [pl.BlockSpec((1, D_MODEL), const), pl.BlockSpec((ROUTE_ROWS, D_MODEL), const),
                     pl.BlockSpec((ROUTE_ROWS, 1), const), pl.BlockSpec((tm, tm), const)]
        tok_row = pl.BlockSpec((None, 1, tm), lambda i: (i, 0, 0))
        out_specs += [tok_row, tok_row]
        out_shape += [jax.ShapeDtypeStruct((n_tiles, 1, tm), jnp.int32)] * 2
        scratch += [pltpu.VMEM((ROUTE_ROWS, LANES), F32)]
        args += list(route_params)
    return pl.pallas_call(
        functools.partial(_merge_kernel, tm=tm, precise=precise, route=route),
        grid=(n_tiles,),
        in_specs=in_specs,
        out_specs=out_specs,
        out_shape=out_shape,
        scratch_shapes=scratch,
        compiler_params=pltpu.CompilerParams(
            dimension_semantics=("arbitrary",), vmem_limit_bytes=VMEM_LIMIT),
        name="merge",
    )(*args)


def _first_index_of_max(vals, lane):
    top = jnp.max(vals, axis=-1, keepdims=True)
    idx = jnp.min(jnp.where(vals == top, lane, float(LANES)), axis=-1, keepdims=True)
    return top, idx


def _top2_weights(le_in_group, lane, pg_top):
    t1, i1 = _first_index_of_max(le_in_group, lane)
    t2, i2 = _first_index_of_max(jnp.where(lane == i1, -jnp.inf, le_in_group), lane)
    e2 = jnp.exp(t2 - t1)
    den = 1.0 + e2
    return jnp.where(lane == i1, pg_top * (1.0 / den), jnp.where(lane == i2, pg_top * (e2 / den), 0.0))


def _group_experts(h, comb, lane_i, g, wg_ref, wu_ref, wd_ref):
    hids = []
    for j in range(EXPERTS_PER_GROUP):
        w_tok = jnp.sum(jnp.where(lane_i == g * EXPERTS_PER_GROUP + j, comb, 0.0), axis=-1, keepdims=True)
        a = jnp.dot(h, wg_ref[j], preferred_element_type=F32)
        b = jnp.dot(h, wu_ref[j], preferred_element_type=F32)
        hids.append((jax.nn.silu(a) * b * w_tok).astype(BF16))
    hid = jnp.concatenate(hids, axis=1)
    return jnp.dot(hid, wd_ref[...].reshape(EXPERTS_PER_GROUP * D_EXPERT, D_MODEL), preferred_element_type=F32)


def _embedding_gate(x2, p_ref, gple_ref, wpg_ref, wpe_ref):
    gate = jax.nn.sigmoid(jnp.dot(_rms(x2, gple_ref[...]).astype(BF16), wpg_ref[...], preferred_element_type=F32))
    pe = jnp.dot(p_ref[...].astype(BF16), wpe_ref[...], preferred_element_type=F32)
    return x2 + gate * pe


def _gather_rows(src_hbm, index_ref, first, buf, slot, sem):
    if isinstance(slot, int):
        for r in range(buf.shape[1]):
            pltpu.make_async_copy(src_hbm.at[index_ref[first + r]], buf.at[slot, r], sem.at[slot]).start(
                priority=r % 2)
        return

    def issue(r, carry):
        pltpu.make_async_copy(src_hbm.at[index_ref[first + r]], buf.at[slot, r], sem.at[slot]).start()
        return carry
    lax.fori_loop(0, buf.shape[1], issue, 0, unroll=DMA_UNROLL)


def _gather_next_tile(src_hbm, index_ref, i, n_tiles, buf, sem):
    for parity in range(2):
        @pl.when((i + 1 < n_tiles) & (i % 2 == parity))
        def _():
            _gather_rows(src_hbm, index_ref, (i + 1) * buf.shape[1], buf, 1 - parity, sem)


def _wait_rows(src_hbm, buf, slot, sem):
    pltpu.make_async_copy(src_hbm.at[pl.ds(0, buf.shape[1])], buf.at[slot], sem.at[slot]).wait()


def _experts_kernel(src_row_ref, tile_group_ref, n_valid_ref, x1_hbm, gffn_ref, wrg_ref, brg_ref, wre_ref, bre_ref,
                    wg_ref, wu_ref, wd_ref, ys_ref, xbuf, sems):
    i = pl.program_id(0)
    n_valid = n_valid_ref[0]

    @pl.when(i == 0)
    def _():
        _gather_rows(x1_hbm, src_row_ref, 0, xbuf, 0, sems)

    _gather_next_tile(x1_hbm, src_row_ref, i, n_valid, xbuf, sems)

    @pl.when(i < n_valid)
    def _():
        slot = i % 2
        _wait_rows(x1_hbm, xbuf, slot, sems)
        g = tile_group_ref[i]
        lane_i = lax.broadcasted_iota(jnp.int32, (1, LANES), 1)
        lane = lane_i.astype(F32)
        h = _rms(xbuf[slot], gffn_ref[...]).astype(BF16)
        lg = _mm(h, wrg_ref[...], False) + brg_ref[...]
        eg = jnp.exp(lg - jnp.max(lg, axis=-1, keepdims=True))
        pg = jnp.sum(jnp.where(lane_i == g, eg, 0.0), axis=-1, keepdims=True) / jnp.sum(eg, axis=-1, keepdims=True)
        le = _mm(h, wre_ref[...], False) + bre_ref[...]
        comb = _top2_weights(jnp.where(lane_i // EXPERTS_PER_GROUP == g, le, -jnp.inf), lane, pg)
        ys_ref[...] = _group_experts(h, comb, lane_i, g, wg_ref, wu_ref, wd_ref)

    @pl.when(i >= n_valid)
    def _():
        ys_ref[...] = jnp.zeros_like(ys_ref)


def _experts(src_row, tile_group, n_valid, x1, gffn, wrg, brg, wre, bre, wg, wu, wd, *, n_tiles):
    const = lambda i, sr, tg, nv: (0, 0)
    group = lambda i, sr, tg, nv: (tg[i], 0, 0)
    return pl.pallas_call(
        _experts_kernel,
        grid_spec=pltpu.PrefetchScalarGridSpec(
            num_scalar_prefetch=3,
            grid=(n_tiles,),
            in_specs=[
                pl.BlockSpec(memory_space=pl.ANY),
                pl.BlockSpec((1, D_MODEL), const),
                pl.BlockSpec((D_MODEL, LANES), const),
                pl.BlockSpec((1, LANES), const),
                pl.BlockSpec((D_MODEL, LANES), const),
                pl.BlockSpec((1, LANES), const),
                pl.BlockSpec((EXPERTS_PER_GROUP, D_MODEL, D_EXPERT), group),
                pl.BlockSpec((EXPERTS_PER_GROUP, D_MODEL, D_EXPERT), group),
                pl.BlockSpec((EXPERTS_PER_GROUP, D_EXPERT, D_MODEL), group),
            ],
            out_specs=pl.BlockSpec((ROUTE_TILE, D_MODEL), lambda i, sr, tg, nv: (i, 0)),
            scratch_shapes=[pltpu.VMEM((2, ROUTE_TILE, D_MODEL), F32), pltpu.SemaphoreType.DMA((2,))]),
        out_shape=jax.ShapeDtypeStruct((n_tiles * ROUTE_TILE, D_MODEL), F32),
        compiler_params=pltpu.CompilerParams(
            dimension_semantics=("arbitrary",), vmem_limit_bytes=VMEM_LIMIT),
        name="experts",
    )(src_row, tile_group, n_valid, x1, gffn, wrg, brg, wre, bre, wg, wu, wd)


def _combine_kernel(pos_ref, x1_ref, p_ref, gple_ref, wpg_ref, wpe_ref, ys_hbm, out_ref, ybuf, sems, *, tm):
    i = pl.program_id(0)

    @pl.when(i == 0)
    def _():
        _gather_rows(ys_hbm, pos_ref, 0, ybuf, 0, sems)

    _gather_next_tile(ys_hbm, pos_ref, i, pl.num_programs(0), ybuf, sems)

    slot = i % 2
    _wait_rows(ys_hbm, ybuf, slot, sems)
    out_ref[...] = _embedding_gate(x1_ref[...] + ybuf[slot], p_ref, gple_ref, wpg_ref, wpe_ref)


def _combine(pos, x1, p, gple, wpg, wpe, ys, *, tm):
    t = x1.shape[0]
    row = lambda i, pos: (i, 0)
    const = lambda i, pos: (0, 0)
    return pl.pallas_call(
        functools.partial(_combine_kernel, tm=tm),
        grid_spec=pltpu.PrefetchScalarGridSpec(
            num_scalar_prefetch=1,
            grid=(t // tm,),
            in_specs=[pl.BlockSpec((tm, D_MODEL), row),
                      pl.BlockSpec((tm, PLE_DIM), row),
                      pl.BlockSpec((1, D_MODEL), const),
                      pl.BlockSpec((D_MODEL, D_MODEL), const),
                      pl.BlockSpec((PLE_DIM, D_MODEL), const),
                      pl.BlockSpec(memory_space=pl.ANY)],
            out_specs=pl.BlockSpec((tm, D_MODEL), row),
            scratch_shapes=[pltpu.VMEM((2, tm, D_MODEL), F32), pltpu.SemaphoreType.DMA((2,))]),
        out_shape=jax.ShapeDtypeStruct((t, D_MODEL), F32),
        compiler_params=pltpu.CompilerParams(
            dimension_semantics=("arbitrary",), vmem_limit_bytes=VMEM_LIMIT),
        name="combine",
    )(pos, x1, p, gple, wpg, wpe, ys)


def _route_plan(gidx, rank, *, n_tiles):
    t = gidx.shape[0]
    groups = jnp.arange(N_GROUPS, dtype=jnp.int32)
    member = gidx[None, :] == groups[:, None]
    counts = jnp.sum(member, axis=1, dtype=jnp.int32)
    padded = (counts + ROUTE_TILE - 1) // ROUTE_TILE * ROUTE_TILE
    ends = jnp.cumsum(padded)
    offsets = ends - padded
    pos = rank + jnp.sum(jnp.where(member, offsets[:, None], 0), axis=0, dtype=jnp.int32)
    src_row = jnp.zeros((n_tiles * ROUTE_TILE,), jnp.int32).at[pos].set(
        jnp.arange(t, dtype=jnp.int32), unique_indices=True, indices_are_sorted=False)
    n_valid = jnp.maximum(ends[-1] // ROUTE_TILE, 1)
    tile_start = jnp.minimum(jnp.arange(n_tiles, dtype=jnp.int32), n_valid - 1) * ROUTE_TILE
    tile_group = jnp.minimum(jnp.sum(tile_start[:, None] >= ends[None, :], axis=1, dtype=jnp.int32), N_GROUPS - 1)
    return pos, src_row, tile_group, n_valid.reshape(1).astype(jnp.int32)


def _channel_kernel(x1_ref, p_ref, gffn_ref, wrg_ref, brg_ref, wre_ref, bre_ref,
                    wg_ref, wu_ref, wd_ref, gple_ref, wpg_ref, wpe_ref,
                    out_ref, h_ref, comb_ref, acc_ref, *, precise_router):
    g = pl.program_id(1)
    lane_i = lax.broadcasted_iota(jnp.int32, (1, LANES), 1)
    lane = lane_i.astype(F32)

    @pl.when(g == 0)
    def _():
        h = _rms(x1_ref[...], gffn_ref[...])
        h_ref[...] = h.astype(BF16)
        lg = _mm(h, wrg_ref[...], precise_router) + brg_ref[...]
        top_g, g_idx = _first_index_of_max(lg, lane)
        pg_top = 1.0 / jnp.sum(jnp.exp(lg - top_g), axis=-1, keepdims=True)
        le = _mm(h, wre_ref[...], precise_router) + bre_ref[...]
        group_of_lane = (lane_i // EXPERTS_PER_GROUP).astype(F32)
        comb_ref[...] = _top2_weights(jnp.where(group_of_lane == g_idx, le, -jnp.inf), lane, pg_top)

    y = _group_experts(h_ref[...], comb_ref[...], lane_i, g, wg_ref, wu_ref, wd_ref)

    @pl.when(g == 0)
    def _():
        acc_ref[...] = y

    @pl.when(g > 0)
    def _():
        acc_ref[...] += y

    @pl.when(g == N_GROUPS - 1)
    def _():
        out_ref[...] = _embedding_gate(x1_ref[...] + acc_ref[...], p_ref, gple_ref, wpg_ref, wpe_ref)


def _channel(x1, p, gffn, wrg, brg, wre, bre, wg, wu, wd, gple, wpg, wpe, *, tm, precise_router):
    t = x1.shape[0]
    row = lambda i, g: (i, 0)
    const2 = lambda i, g: (0, 0)
    group = lambda i, g: (g, 0, 0)
    return pl.pallas_call(
        functools.partial(_channel_kernel, precise_router=precise_router),
        grid=(t // tm, N_GROUPS),
        in_specs=[
            pl.BlockSpec((tm, D_MODEL), row),
            pl.BlockSpec((tm, PLE_DIM), row),
            pl.BlockSpec((1, D_MODEL), const2),
            pl.BlockSpec((D_MODEL, LANES), const2),
            pl.BlockSpec((1, LANES), const2),
            pl.BlockSpec((D_MODEL, LANES), const2),
            pl.BlockSpec((1, LANES), const2),
            pl.BlockSpec((EXPERTS_PER_GROUP, D_MODEL, D_EXPERT), group),
            pl.BlockSpec((EXPERTS_PER_GROUP, D_MODEL, D_EXPERT), group),
            pl.BlockSpec((EXPERTS_PER_GROUP, D_EXPERT, D_MODEL), group),
            pl.BlockSpec((1, D_MODEL), const2),
            pl.BlockSpec((D_MODEL, D_MODEL), const2),
            pl.BlockSpec((PLE_DIM, D_MODEL), const2),
        ],
        out_specs=pl.BlockSpec((tm, D_MODEL), row),
        out_shape=jax.ShapeDtypeStruct((t, D_MODEL), F32),
        scratch_shapes=[pltpu.VMEM((tm, D_MODEL), BF16),
                        pltpu.VMEM((tm, LANES), F32),
                        pltpu.VMEM((tm, D_MODEL), F32)],
        compiler_params=pltpu.CompilerParams(
            dimension_semantics=("arbitrary", "arbitrary"), vmem_limit_bytes=VMEM_LIMIT),
        name="channel",
    )(x1, p, gffn, wrg, brg, wre, bre, wg, wu, wd, gple, wpg, wpe)


def _pad_lanes(a, fill):
    return jnp.pad(a, ((0, 0), (0, LANES - a.shape[1])), constant_values=fill)


def kernel(x_prompt, x_sample, cache_attn_k, cache_attn_v, p_prompt, p_sample, g_mix, w_in, g_gmlp_v, w_gmlp_s, b_gmlp_s, g_q, g_k, rel_bias, w_branch, w_out, g_ffn, w_router_group, b_router_group, w_router_expert, b_router_expert, w_exp_gate, w_exp_up, w_exp_down, g_ple, w_ple_gate, w_ple_proj):
    depth = g_mix.shape[0]
    batch, seq, _ = x_prompt.shape
    dec_batch, dec_seq, _ = x_sample.shape
    lc = cache_attn_k.shape[2]
    keep = min(N_PAST_CHUNKS * CHUNK, seq)
    tm = 512
    assert depth == 1 and keep == tm and seq % tm == 0 and seq % Q_BLOCK == 0
    assert dec_seq <= GMLP_CHUNK and GMLP_CHUNK % dec_seq == 0 and dec_batch * dec_seq == GMLP_CHUNK
    l = 0
    t_s = dec_batch * dec_seq

    row = lambda a: a.reshape(1, -1).astype(F32)
    w_in_b = w_in[l].astype(BF16)
    gq_t = row(jnp.tile(g_q[l], ATTN_HEADS))
    gk_t = row(jnp.tile(g_k[l], ATTN_HEADS))
    head_of_lane = jnp.arange(D_MODEL, dtype=jnp.int32) // HEAD_DIM
    seg = (head_of_lane[:, None] == jnp.arange(LANES, dtype=jnp.int32)[None, :]).astype(BF16)
    segt2 = jnp.concatenate([seg.T, seg.T], axis=0)
    ws_tril = jnp.tril(w_gmlp_s[l])
    ws_p = ws_tril.astype(BF16)
    bs_p = jnp.repeat(b_gmlp_s[l].T, GMLP_GROUP_DIM, axis=1).astype(F32)
    eye = jnp.eye(dec_batch, dtype=F32)
    ws_s = jnp.einsum('ab,gts->gatbs', eye, ws_tril[:, :dec_seq, :dec_seq]).reshape(
        GMLP_GROUPS, t_s, t_s)
    bs_s = jnp.tile(jnp.repeat(b_gmlp_s[l][:, :dec_seq].T, GMLP_GROUP_DIM, axis=1), (dec_batch, 1)).astype(F32)
    wb_b = w_branch[l].astype(BF16)
    wout_b = w_out[l].astype(BF16)
    wrg = _pad_lanes(w_router_group[l], 0.0)
    brg = _pad_lanes(row(b_router_group[l]), NEG)
    wre = _pad_lanes(w_router_expert[l], 0.0)
    bre = _pad_lanes(row(b_router_expert[l]), 0.0)
    wg_b = w_exp_gate[l].astype(BF16)
    wu_b = w_exp_up[l].astype(BF16)
    wd_b = w_exp_down[l].astype(BF16)
    wpg_b = w_ple_gate[l].astype(BF16)
    wpe_b = w_ple_proj[l].astype(BF16)
    bias_prompt = _prompt_bias_table(rel_bias[l])
    bias_c, bias_n = _sample_bias_tables(rel_bias[l], dec_seq, lc, PAST_LEN)

    wrgt = jnp.pad(w_router_group[l].T, ((0, ROUTE_ROWS - N_GROUPS), (0, 0))).astype(BF16)
    brgt = jnp.pad(b_router_group[l].astype(F32), (0, ROUTE_ROWS - N_GROUPS), constant_values=NEG).reshape(ROUTE_ROWS, 1)
    upper = jnp.asarray(np.triu(np.ones((tm, tm), np.float32), 1), BF16)

    xp = x_prompt.reshape(batch * seq, D_MODEL)
    u, vn, q, k, v, sga, sgb, kst, vst = _inproj(
        xp, row(g_mix[l]), w_in_b, row(g_gmlp_v[l]), gq_t, gk_t, seg, segt2,
        tm=tm, tiles_per_batch=seq // tm, emit_vn_state=False, precise=False)
    ob = _attn_prompt(q, k, v, bias_prompt, batch=batch, seq=seq)
    x1, gidx, rank = _merge(xp, u, vn, sga, sgb, ob, ws_p, bs_p, wb_b, wout_b,
                            (row(g_ffn[l]), wrgt, brgt, upper), tm=tm, precise=False)
    n_route_tiles = batch * seq // ROUTE_TILE + N_GROUPS
    pos, src_row, tile_group, n_valid = _route_plan(gidx.reshape(-1), rank.reshape(-1), n_tiles=n_route_tiles)
    y_sorted = _experts(src_row, tile_group, n_valid, x1, row(g_ffn[l]), wrg.astype(BF16), brg, wre.astype(BF16), bre,
                        wg_b, wu_b, wd_b, n_tiles=n_route_tiles)
    yp = _combine(pos, x1, p_prompt[l].reshape(batch * seq, PLE_DIM), row(g_ple[l]), wpg_b, wpe_b, y_sorted, tm=tm)

    xs = x_sample.reshape(t_s, D_MODEL)
    u, vn, q, k, v, sga, sgb, kss, vss, gvs = _inproj(
        xs, row(g_mix[l]), w_in[l], row(g_gmlp_v[l]), gq_t, gk_t, seg, segt2,
        tm=t_s, tiles_per_batch=1, emit_vn_state=True, precise=True)
    ob = _attn_sample(q, k, v, cache_attn_k[l].reshape(dec_batch, lc, D_MODEL),
                      cache_attn_v[l].reshape(dec_batch, lc, D_MODEL), bias_c, bias_n,
                      batch=dec_batch, rows=dec_seq)
    x1, = _merge(xs, u, vn, sga, sgb, ob, ws_s, bs_s, w_branch[l], w_out[l], tm=t_s, precise=True)
    ys = _channel(x1, p_sample[l].reshape(t_s, PLE_DIM), row(g_ffn[l]), wrg, brg, wre, bre, wg_b, wu_b, wd_b,
                  row(g_ple[l]), wpg_b, wpe_b, tm=t_s, precise_router=True)

    return (yp.reshape(batch, seq, D_MODEL),
            ys.reshape(dec_batch, dec_seq, D_MODEL),
            kst.reshape(1, batch, keep, ATTN_HEADS, HEAD_DIM),
            vst.reshape(1, batch, keep, ATTN_HEADS, HEAD_DIM),
            kss.reshape(1, dec_batch, dec_seq, ATTN_HEADS, HEAD_DIM),
            vss.reshape(1, dec_batch, dec_seq, ATTN_HEADS, HEAD_DIM),
            gvs.reshape(1, dec_batch, dec_seq, D_MODEL))
```

```python
import functools

import jax
import jax.numpy as jnp
import numpy as np
from jax import lax
from jax.experimental import pallas as pl
from jax.experimental.pallas import tpu as pltpu

F32 = jnp.float32
BF16 = jnp.bfloat16

D_MODEL = 1024
CHUNK = 64
N_PAST_CHUNKS = 8
HEAD_DIM = 64
ATTN_HEADS = D_MODEL // HEAD_DIM
MAX_REL = 128
GMLP_CHUNK = 128
GMLP_GROUPS = 4
GMLP_GROUP_DIM = D_MODEL // GMLP_GROUPS
N_SECTIONS = 7
N_GROUPS = 4
EXPERTS_PER_GROUP = 8
N_EXPERTS = N_GROUPS * EXPERTS_PER_GROUP
D_EXPERT = D_MODEL // 4
PLE_DIM = 256
EPS = 1e-6
NEG = -1e30
PAST_LEN = 1024

LANES = 128
Q_BLOCK = 2 * CHUNK
K_WINDOW = Q_BLOCK + N_PAST_CHUNKS * CHUNK
N_SHIFT = N_PAST_CHUNKS * CHUNK // Q_BLOCK + 1
BIAS_SPAN = K_WINDOW + (N_SHIFT - 1) * Q_BLOCK
ROUTE_ROWS = 16
ROUTE_TILE = 512
DMA_UNROLL = 8
ATTN_UNROLL = 6
VMEM_LIMIT = 56 * 1024 * 1024


def _rms(x, gain):
    return x * lax.rsqrt(jnp.mean(x * x, axis=-1, keepdims=True) + EPS) * gain


def _act_dtype(precise):
    return F32 if precise else BF16


def _mm(a, b, precise, dims=None):
    dims = dims or (((a.ndim - 1,), (0,)), ((), ()))
    if precise:
        return lax.dot_general(a.astype(F32), b.astype(F32), dims, precision=lax.Precision.HIGHEST,
                               preferred_element_type=F32)
    return lax.dot_general(a.astype(BF16), b.astype(BF16), dims, preferred_element_type=F32)


_NT = (((1,), (1,)), ((), ()))


def _head_rms(z, gain, seg, segt2, precise):
    ssum = _mm(z * z, seg, precise)
    inv = lax.rsqrt(ssum * (1.0 / HEAD_DIM) + EPS)
    if precise:
        full = _mm(inv, segt2[:LANES], True)
    else:
        hi = inv.astype(BF16)
        lo = (inv - hi.astype(F32)).astype(BF16)
        full = _mm(jnp.concatenate([hi, lo], axis=1), segt2, False)
    return z * full * gain


def _inproj_kernel(x_ref, gmix_ref, w_ref, gv_ref, gq_ref, gk_ref, seg_ref, segt2_ref,
                   u_ref, vn_ref, q_ref, k_ref, v_ref, sga_ref, sgb_ref,
                   kst_ref, vst_ref, *maybe_vnst_ref, tiles_per_batch, precise):
    act = _act_dtype(precise)
    h = _rms(x_ref[...], gmix_ref[...]).astype(act)

    def section(s):
        return _mm(h, w_ref[:, s * D_MODEL:(s + 1) * D_MODEL], precise)

    is_state_tile = (pl.program_id(0) % tiles_per_batch) == tiles_per_batch - 1

    u_ref[...] = jax.nn.gelu(section(0)).astype(act)

    vn = _rms(jax.nn.gelu(section(1)), gv_ref[...])
    vn_ref[...] = vn.astype(act)
    if maybe_vnst_ref:
        maybe_vnst_ref[0][...] = vn

    qn = _head_rms(section(2), gq_ref[...], seg_ref[...], segt2_ref[...], precise)
    q_ref[...] = (qn * (HEAD_DIM ** -0.5)).astype(act)

    kn = _head_rms(section(3), gk_ref[...], seg_ref[...], segt2_ref[...], precise)
    k_ref[...] = kn.astype(act)

    @pl.when(is_state_tile)
    def _():
        kst_ref[...] = kn

    v = section(4)
    v_ref[...] = v.astype(act)

    @pl.when(is_state_tile)
    def _():
        vst_ref[...] = v

    sga_ref[...] = jax.nn.sigmoid(section(5)).astype(act)
    sgb_ref[...] = jax.nn.sigmoid(section(6)).astype(act)


def _inproj(x, gmix, w_in, gv, gq_t, gk_t, seg, segt2, *, tm, tiles_per_batch, emit_vn_state, precise):
    t = x.shape[0]
    n_tiles = t // tm
    n_state = n_tiles // tiles_per_batch
    row = lambda i: (i, 0)
    const = lambda i: (0, 0)
    state = lambda i: (i // tiles_per_batch, 0)
    act = jax.ShapeDtypeStruct((t, D_MODEL), _act_dtype(precise))
    st = jax.ShapeDtypeStruct((n_state * tm, D_MODEL), F32)
    out_shape = [act] * 7 + [st, st] + ([st] if emit_vn_state else [])
    out_specs = [pl.BlockSpec((tm, D_MODEL), row)] * 7 + [pl.BlockSpec((tm, D_MODEL), state)] * (
        3 if emit_vn_state else 2)
    return pl.pallas_call(
        functools.partial(_inproj_kernel, tiles_per_batch=tiles_per_batch, precise=precise),
        grid=(n_tiles,),
        in_specs=[
            pl.BlockSpec((tm, D_MODEL), row),
            pl.BlockSpec((1, D_MODEL), const),
            pl.BlockSpec((D_MODEL, N_SECTIONS * D_MODEL), const, pipeline_mode=pl.Buffered(1)),
            pl.BlockSpec((1, D_MODEL), const),
            pl.BlockSpec((1, D_MODEL), const),
            pl.BlockSpec((1, D_MODEL), const),
            pl.BlockSpec((D_MODEL, LANES), const),
            pl.BlockSpec((2 * LANES, D_MODEL), const),
        ],
        out_specs=out_specs,
        out_shape=out_shape,
        compiler_params=pltpu.CompilerParams(
            dimension_semantics=("arbitrary",), vmem_limit_bytes=VMEM_LIMIT),
        name="inproj",
    )(x, gmix, w_in, gv, gq_t, gk_t, seg, segt2)


def _attn_prompt_kernel(q_ref, k_ref, v_ref, bias_ref, o_ref, *, seq):
    lane = lax.broadcasted_iota(jnp.int32, (1, LANES), 1)
    low_head = lane < HEAD_DIM
    zero = jnp.zeros((), BF16)

    def scores(qb, shift):
        q_start = pl.multiple_of(qb * Q_BLOCK, Q_BLOCK)
        k_start = pl.multiple_of((qb + shift) * Q_BLOCK - (K_WINDOW - Q_BLOCK), Q_BLOCK)
        q = q_ref[pl.ds(q_start, Q_BLOCK), :]
        k = k_ref[pl.ds(k_start, K_WINDOW), :]
        q2 = jnp.concatenate([jnp.where(low_head, q, zero), jnp.where(low_head, zero, q)], axis=0)
        bias = bias_ref[:, shift * Q_BLOCK:shift * Q_BLOCK + K_WINDOW]
        return lax.dot_general(q2, k, _NT, preferred_element_type=F32) + bias, q_start, k_start

    def attend(s, q_start, k_start):
        v = v_ref[pl.ds(k_start, K_WINDOW), :]
        m = jnp.max(s, axis=-1, keepdims=True)
        p = jnp.exp(s - m)
        l = jnp.sum(p, axis=-1, keepdims=True)
        o = jnp.dot(p.astype(BF16), v, preferred_element_type=F32) * (1.0 / l)
        o_ref[pl.ds(q_start, Q_BLOCK), :] = jnp.where(low_head, o[:Q_BLOCK], o[Q_BLOCK:]).astype(BF16)

    def blocks(specs):
        for args in [scores(qb, shift) for qb, shift in specs]:
            attend(*args)

    blocks([(qb, N_SHIFT - 1 - qb) for qb in range(N_SHIFT - 1)])

    def steady(j, carry):
        first = N_SHIFT - 1 + j * ATTN_UNROLL
        blocks([(first + u, 0) for u in range(ATTN_UNROLL)])
        return carry

    lax.fori_loop(0, (seq // Q_BLOCK - (N_SHIFT - 1)) // ATTN_UNROLL, steady, 0)


def _attn_prompt(q, k, v, bias_tab, *, batch, seq):
    q3, k3, v3 = (a.reshape(batch, seq, D_MODEL) for a in (q, k, v))
    head_pair = lambda hp, b: (b, 0, hp)
    spec = pl.BlockSpec((None, seq, LANES), head_pair)
    out = pl.pallas_call(
        functools.partial(_attn_prompt_kernel, seq=seq),
        grid=(ATTN_HEADS // 2, batch),
        in_specs=[spec, spec, spec,
                  pl.BlockSpec((None, 2 * Q_BLOCK, BIAS_SPAN), lambda hp, b: (hp, 0, 0))],
        out_specs=spec,
        out_shape=jax.ShapeDtypeStruct((batch, seq, D_MODEL), BF16),
        compiler_params=pltpu.CompilerParams(
            dimension_semantics=("arbitrary", "arbitrary"), vmem_limit_bytes=VMEM_LIMIT),
        name="attn_prompt",
    )(q3, k3, v3, bias_tab)
    return out.reshape(batch * seq, D_MODEL)


def _rel_bias_rows(rel_bias, q0, k0, rows, ncols):
    dmax = q0 - k0 + rows - 1
    total = rows + ncols - 1
    dmin = dmax - total + 1
    n_lo = min(max(-MAX_REL - dmin, 0), total)
    n_hi = min(max(dmax - MAX_REL, 0), total)
    mid0 = dmin + n_lo + MAX_REL
    asc = jnp.concatenate([jnp.repeat(rel_bias[:, :1], n_lo, axis=1),
                           rel_bias[:, mid0:mid0 + total - n_lo - n_hi],
                           jnp.repeat(rel_bias[:, -1:], n_hi, axis=1)], axis=1).astype(F32)
    desc = asc[:, ::-1]
    heads = desc.shape[0]
    period = jnp.concatenate([desc[:, rows - 1:], jnp.zeros((heads, 1), F32), desc[:, :rows - 1]], axis=1)
    flat = jnp.tile(period, (1, rows))[:, :rows * total]
    return flat.reshape(heads, rows, total)[:, :, :ncols]


def _band_ok(q_pos, k_pos):
    q_pos, k_pos = q_pos[:, None], k_pos[None, :]
    return (k_pos >= 0) & (k_pos // CHUNK <= q_pos // CHUNK) & (k_pos // CHUNK >= q_pos // CHUNK - N_PAST_CHUNKS)


def _prompt_bias_table(rel_bias):
    back = N_PAST_CHUNKS * CHUNK
    ok = _band_ok(back + np.arange(Q_BLOCK), np.arange(BIAS_SPAN))
    ext = jnp.where(ok[None], _rel_bias_rows(rel_bias, back, 0, Q_BLOCK, BIAS_SPAN), NEG)
    return ext.reshape(ATTN_HEADS // 2, 2 * Q_BLOCK, BIAS_SPAN)


def _attn_sample_kernel(q_ref, k_ref, v_ref, ck_ref, cv_ref, bc_ref, bn_ref, o_ref):
    rows = q_ref.shape[0]
    lane = lax.broadcasted_iota(jnp.int32, (1, LANES), 1)
    low_head = lane < HEAD_DIM
    scored = []
    for hp in range(ATTN_HEADS // 2):
        cols = slice(hp * LANES, (hp + 1) * LANES)
        q = q_ref[:, cols]
        q2 = jnp.concatenate([jnp.where(low_head, q, 0.0), jnp.where(low_head, 0.0, q)], axis=0)
        s_c = _mm(q2, ck_ref[:, cols], True, _NT) + bc_ref[hp]
        s_n = _mm(q2, k_ref[:, cols], True, _NT) + bn_ref[hp]
        scored.append((cols, s_c, s_n))
    for cols, s_c, s_n in scored:
        m = jnp.maximum(jnp.max(s_c, axis=-1, keepdims=True), jnp.max(s_n, axis=-1, keepdims=True))
        p_c = jnp.exp(s_c - m)
        p_n = jnp.exp(s_n - m)
        l = jnp.sum(p_c, axis=-1, keepdims=True) + jnp.sum(p_n, axis=-1, keepdims=True)
        o = (_mm(p_c, cv_ref[:, cols], True) + _mm(p_n, v_ref[:, cols], True)) * (1.0 / l)
        o_ref[:, cols] = jnp.where(low_head, o[:rows], o[rows:])


def _attn_sample(q, k, v, cache_k, cache_v, bias_c, bias_n, *, batch, rows):
    lc = cache_k.shape[1]
    new = pl.BlockSpec((rows, D_MODEL), lambda b: (b, 0))
    cache = pl.BlockSpec((None, lc, D_MODEL), lambda b: (b, 0, 0))
    return pl.pallas_call(
        _attn_sample_kernel,
        grid=(batch,),
        in_specs=[new, new, new, cache, cache,
                  pl.BlockSpec((ATTN_HEADS // 2, 2 * rows, lc), lambda b: (0, 0, 0)),
                  pl.BlockSpec((ATTN_HEADS // 2, 2 * rows, rows), lambda b: (0, 0, 0))],
        out_specs=new,
        out_shape=jax.ShapeDtypeStruct((batch * rows, D_MODEL), F32),
        compiler_params=pltpu.CompilerParams(
            dimension_semantics=("arbitrary",), vmem_limit_bytes=VMEM_LIMIT),
        name="attn_sample",
    )(q, k, v, cache_k, cache_v, bias_c, bias_n)


def _sample_bias_tables(rel_bias, rows, lc, past_len):
    q_pos = past_len + np.arange(rows)
    c_pos = past_len - lc + np.arange(lc)
    bias_c = jnp.where(_band_ok(q_pos, c_pos)[None], _rel_bias_rows(rel_bias, past_len, past_len - lc, rows, lc), NEG)
    bias_n = jnp.where(_band_ok(q_pos, q_pos)[None], _rel_bias_rows(rel_bias, past_len, past_len, rows, rows), NEG)
    pair = lambda tab: tab.reshape(ATTN_HEADS // 2, 2 * rows, tab.shape[-1])
    return pair(bias_c), pair(bias_n)


def _route_rows(x1, gffn_ref, wrgt_ref, brgt_ref, upper_ref, gidx_ref, rank_ref, base_ref):
    tm = x1.shape[0]

    @pl.when(pl.program_id(0) == 0)
    def _():
        base_ref[...] = jnp.zeros_like(base_ref)

    h = _rms(x1, gffn_ref[...]).astype(BF16)
    lgt = lax.dot_general(wrgt_ref[...], h, _NT, preferred_element_type=F32) + brgt_ref[...]
    sub = lax.broadcasted_iota(jnp.int32, (ROUTE_ROWS, tm), 0).astype(F32)
    top = jnp.max(lgt, axis=0, keepdims=True)
    gi = jnp.min(jnp.where(lgt == top, sub, float(ROUTE_ROWS)), axis=0, keepdims=True)
    onehot = jnp.where(sub == gi, 1.0, 0.0)
    before = jnp.dot(onehot.astype(BF16), upper_ref[...], preferred_element_type=F32)
    base = base_ref[...]
    rank = jnp.sum(onehot * (before + base[:, :1]), axis=0, keepdims=True)
    base_ref[...] = base + jnp.sum(onehot, axis=1, keepdims=True)
    gidx_ref[...] = gi.astype(jnp.int32)
    rank_ref[...] = rank.astype(jnp.int32)


def _merge_kernel(x_ref, u_ref, vn_ref, sga_ref, sgb_ref, ob_ref, ws_ref, bs_ref, wb_ref, wout_ref,
                  *rest, tm, precise, route):
    if route:
        gffn_ref, wrgt_ref, brgt_ref, upper_ref, x1_ref, gidx_ref, rank_ref, oa_ref, base_ref = rest
    else:
        x1_ref, oa_ref = rest
    act = _act_dtype(precise)
    for c in range(tm // GMLP_CHUNK):
        rows = slice(c * GMLP_CHUNK, (c + 1) * GMLP_CHUNK)
        for g in range(GMLP_GROUPS):
            cols = slice(g * GMLP_GROUP_DIM, (g + 1) * GMLP_GROUP_DIM)
            mix = _mm(ws_ref[g], vn_ref[rows, cols], precise) + bs_ref[:, cols]
            oa_ref[rows, cols] = (u_ref[rows, cols].astype(F32) * mix).astype(act)
    b_a = _mm(oa_ref[...], wb_ref[0], precise)
    b_b = _mm(ob_ref[...], wb_ref[1], precise)
    m = sga_ref[...].astype(F32) * b_a + sgb_ref[...].astype(F32) * b_b
    x1 = x_ref[...] + _mm(m, wout_ref[...], precise)
    x1_ref[...] = x1
    if route:
        _route_rows(x1, gffn_ref, wrgt_ref, brgt_ref, upper_ref, gidx_ref, rank_ref, base_ref)


def _merge(x, u, vn, sga, sgb, ob, ws, bs_full, wb, wout, route_params=None, *, tm, precise):
    t = x.shape[0]
    n_tiles = t // tm
    route = route_params is not None
    row = pl.BlockSpec((tm, D_MODEL), lambda i: (i, 0))
    const = lambda i: (0, 0)
    in_specs = [row, row, row, row, row, row,
                pl.BlockSpec((GMLP_GROUPS, GMLP_CHUNK, GMLP_CHUNK), lambda i: (0, 0, 0)),
                pl.BlockSpec((GMLP_CHUNK, D_MODEL), const),
                pl.BlockSpec((2, D_MODEL, D_MODEL), lambda i: (0, 0, 0)),
                pl.BlockSpec((D_MODEL, D_MODEL), const)]
    out_specs = [row]
    out_shape = [jax.ShapeDtypeStruct((t, D_MODEL), F32)]
    scratch = [pltpu.VMEM((tm, D_MODEL), _act_dtype(precise))]
    args = [x, u, vn, sga, sgb, ob, ws, bs_full, wb, wout]
    if route:
        in_specs += [pl.BlockSpec((1, D_MODEL), const), pl.BlockSpec((ROUTE_ROWS, D_MODEL), const),
                     pl.BlockSpec((ROUTE_ROWS, 1), const), pl.BlockSpec((tm, tm), const)]
        tok_row = pl.BlockSpec((None, 1, tm), lambda i: (i, 0, 0))
        out_specs += [tok_row, tok_row]
        out_shape += [jax.ShapeDtypeStruct((n_tiles, 1, tm), jnp.int32)] * 2
        scratch += [pltpu.VMEM((ROUTE_ROWS, LANES), F32)]
        args += list(route_params)
    return pl.pallas_call(
        functools.partial(_merge_kernel, tm=tm, precise=precise, route=route),
        grid=(n_tiles,),
        in_specs=in_specs,
        out_specs=out_specs,
        out_shape=out_shape,
        scratch_shapes=scratch,
        compiler_params=pltpu.CompilerParams(
            dimension_semantics=("arbitrary",), vmem_limit_bytes=VMEM_LIMIT),
        name="merge",
    )(*args)


def _first_index_of_max(vals, lane):
    top = jnp.max(vals, axis=-1, keepdims=True)
    idx = jnp.min(jnp.where(vals == top, lane, float(LANES)), axis=-1, keepdims=True)
    return top, idx


def _top2_weights(le_in_group, lane, pg_top):
    t1, i1 = _first_index_of_max(le_in_group, lane)
    t2, i2 = _first_index_of_max(jnp.where(lane == i1, -jnp.inf, le_in_group), lane)
    e2 = jnp.exp(t2 - t1)
    den = 1.0 + e2
    return jnp.where(lane == i1, pg_top * (1.0 / den), jnp.where(lane == i2, pg_top * (e2 / den), 0.0))


def _group_experts(h, comb, lane_i, g, wg_ref, wu_ref, wd_ref):
    hids = []
    for j in range(EXPERTS_PER_GROUP):
        w_tok = jnp.sum(jnp.where(lane_i == g * EXPERTS_PER_GROUP + j, comb, 0.0), axis=-1, keepdims=True)
        a = jnp.dot(h, wg_ref[j], preferred_element_type=F32)
        b = jnp.dot(h, wu_ref[j], preferred_element_type=F32)
        hids.append((jax.nn.silu(a) * b * w_tok).astype(BF16))
    hid = jnp.concatenate(hids, axis=1)
    return jnp.dot(hid, wd_ref[...].reshape(EXPERTS_PER_GROUP * D_EXPERT, D_MODEL), preferred_element_type=F32)


def _embedding_gate(x2, p_ref, gple_ref, wpg_ref, wpe_ref):
    gate = jax.nn.sigmoid(jnp.dot(_rms(x2, gple_ref[...]).astype(BF16), wpg_ref[...], preferred_element_type=F32))
    pe = jnp.dot(p_ref[...].astype(BF16), wpe_ref[...], preferred_element_type=F32)
    return x2 + gate * pe


def _gather_rows(src_hbm, index_ref, first, buf, slot, sem):
    if isinstance(slot, int):
        for r in range(buf.shape[1]):
            pltpu.make_async_copy(src_hbm.at[index_ref[first + r]], buf.at[slot, r], sem.at[slot]).start(
                priority=r % 2)
        return

    def issue(r, carry):
        pltpu.make_async_copy(src_hbm.at[index_ref[first + r]], buf.at[slot, r], sem.at[slot]).start()
        return carry
    lax.fori_loop(0, buf.shape[1], issue, 0, unroll=DMA_UNROLL)


def _gather_next_tile(src_hbm, index_ref, i, n_tiles, buf, sem):
    for parity in range(2):
        @pl.when((i + 1 < n_tiles) & (i % 2 == parity))
        def _():
            _gather_rows(src_hbm, index_ref, (i + 1) * buf.shape[1], buf, 1 - parity, sem)


def _wait_rows(src_hbm, buf, slot, sem):
    pltpu.make_async_copy(src_hbm.at[pl.ds(0, buf.shape[1])], buf.at[slot], sem.at[slot]).wait()


def _experts_kernel(src_row_ref, tile_group_ref, n_valid_ref, x1_hbm, gffn_ref, wrg_ref, brg_ref, wre_ref, bre_ref,
                    wg_ref, wu_ref, wd_ref, ys_ref, xbuf, sems):
    i = pl.program_id(0)
    n_valid = n_valid_ref[0]

    @pl.when(i == 0)
    def _():
        _gather_rows(x1_hbm, src_row_ref, 0, xbuf, 0, sems)

    _gather_next_tile(x1_hbm, src_row_ref, i, n_valid, xbuf, sems)

    @pl.when(i < n_valid)
    def _():
        slot = i % 2
        _wait_rows(x1_hbm, xbuf, slot, sems)
        g = tile_group_ref[i]
        lane_i = lax.broadcasted_iota(jnp.int32, (1, LANES), 1)
        lane = lane_i.astype(F32)
        h = _rms(xbuf[slot], gffn_ref[...]).astype(BF16)
        lg = _mm(h, wrg_ref[...], False) + brg_ref[...]
        eg = jnp.exp(lg - jnp.max(lg, axis=-1, keepdims=True))
        pg = jnp.sum(jnp.where(lane_i == g, eg, 0.0), axis=-1, keepdims=True) / jnp.sum(eg, axis=-1, keepdims=True)
        le = _mm(h, wre_ref[...], False) + bre_ref[...]
        comb = _top2_weights(jnp.where(lane_i // EXPERTS_PER_GROUP == g, le, -jnp.inf), lane, pg)
        ys_ref[...] = _group_experts(h, comb, lane_i, g, wg_ref, wu_ref, wd_ref)

    @pl.when(i >= n_valid)
    def _():
        ys_ref[...] = jnp.zeros_like(ys_ref)


def _experts(src_row, tile_group, n_valid, x1, gffn, wrg, brg, wre, bre, wg, wu, wd, *, n_tiles):
    const = lambda i, sr, tg, nv: (0, 0)
    group = lambda i, sr, tg, nv: (tg[i], 0, 0)
    return pl.pallas_call(
        _experts_kernel,
        grid_spec=pltpu.PrefetchScalarGridSpec(
            num_scalar_prefetch=3,
            grid=(n_tiles,),
            in_specs=[
                pl.BlockSpec(memory_space=pl.ANY),
                pl.BlockSpec((1, D_MODEL), const),
                pl.BlockSpec((D_MODEL, LANES), const),
                pl.BlockSpec((1, LANES), const),
                pl.BlockSpec((D_MODEL, LANES), const),
                pl.BlockSpec((1, LANES), const),
                pl.BlockSpec((EXPERTS_PER_GROUP, D_MODEL, D_EXPERT), group),
                pl.BlockSpec((EXPERTS_PER_GROUP, D_MODEL, D_EXPERT), group),
                pl.BlockSpec((EXPERTS_PER_GROUP, D_EXPERT, D_MODEL), group),
            ],
            out_specs=pl.BlockSpec((ROUTE_TILE, D_MODEL), lambda i, sr, tg, nv: (i, 0)),
            scratch_shapes=[pltpu.VMEM((2, ROUTE_TILE, D_MODEL), F32), pltpu.SemaphoreType.DMA((2,))]),
        out_shape=jax.ShapeDtypeStruct((n_tiles * ROUTE_TILE, D_MODEL), F32),
        compiler_params=pltpu.CompilerParams(
            dimension_semantics=("arbitrary",), vmem_limit_bytes=VMEM_LIMIT),
        name="experts",
    )(src_row, tile_group, n_valid, x1, gffn, wrg, brg, wre, bre, wg, wu, wd)


def _combine_kernel(pos_ref, x1_ref, p_ref, gple_ref, wpg_ref, wpe_ref, ys_hbm, out_ref, ybuf, sems, *, tm):
    i = pl.program_id(0)

    @pl.when(i == 0)
    def _():
        _gather_rows(ys_hbm, pos_ref, 0, ybuf, 0, sems)

    _gather_next_tile(ys_hbm, pos_ref, i, pl.num_programs(0), ybuf, sems)

    slot = i % 2
    _wait_rows(ys_hbm, ybuf, slot, sems)
    out_ref[...] = _embedding_gate(x1_ref[...] + ybuf[slot], p_ref, gple_ref, wpg_ref, wpe_ref)


def _combine(pos, x1, p, gple, wpg, wpe, ys, *, tm):
    t = x1.shape[0]
    row = lambda i, pos: (i, 0)
    const = lambda i, pos: (0, 0)
    return pl.pallas_call(
        functools.partial(_combine_kernel, tm=tm),
        grid_spec=pltpu.PrefetchScalarGridSpec(
            num_scalar_prefetch=1,
            grid=(t // tm,),
            in_specs=[pl.BlockSpec((tm, D_MODEL), row),
                      pl.BlockSpec((tm, PLE_DIM), row),
                      pl.BlockSpec((1, D_MODEL), const),
                      pl.BlockSpec((D_MODEL, D_MODEL), const),
                      pl.BlockSpec((PLE_DIM, D_MODEL), const),
                      pl.BlockSpec(memory_space=pl.ANY)],
            out_specs=pl.BlockSpec((tm, D_MODEL), row),
            scratch_shapes=[pltpu.VMEM((2, tm, D_MODEL), F32), pltpu.SemaphoreType.DMA((2,))]),
        out_shape=jax.ShapeDtypeStruct((t, D_MODEL), F32),
        compiler_params=pltpu.CompilerParams(
            dimension_semantics=("arbitrary",), vmem_limit_bytes=VMEM_LIMIT),
        name="combine",
    )(pos, x1, p, gple, wpg, wpe, ys)


def _route_plan(gidx, rank, *, n_tiles):
    t = gidx.shape[0]
    groups = jnp.arange(N_GROUPS, dtype=jnp.int32)
    member = gidx[None, :] == groups[:, None]
    counts = jnp.sum(member, axis=1, dtype=jnp.int32)
    padded = (counts + ROUTE_TILE - 1) // ROUTE_TILE * ROUTE_TILE
    ends = jnp.cumsum(padded)
    offsets = ends - padded
    pos = rank + jnp.sum(jnp.where(member, offsets[:, None], 0), axis=0, dtype=jnp.int32)
    src_row = jnp.zeros((n_tiles * ROUTE_TILE,), jnp.int32).at[pos].set(
        jnp.arange(t, dtype=jnp.int32), unique_indices=True, indices_are_sorted=False)
    n_valid = jnp.maximum(ends[-1] // ROUTE_TILE, 1)
    tile_start = jnp.minimum(jnp.arange(n_tiles, dtype=jnp.int32), n_valid - 1) * ROUTE_TILE
    tile_group = jnp.minimum(jnp.sum(tile_start[:, None] >= ends[None, :], axis=1, dtype=jnp.int32), N_GROUPS - 1)
    return pos, src_row, tile_group, n_valid.reshape(1).astype(jnp.int32)


def _channel_kernel(x1_ref, p_ref, gffn_ref, wrg_ref, brg_ref, wre_ref, bre_ref,
                    wg_ref, wu_ref, wd_ref, gple_ref, wpg_ref, wpe_ref,
                    out_ref, h_ref, comb_ref, acc_ref, *, precise_router):
    g = pl.program_id(1)
    lane_i = lax.broadcasted_iota(jnp.int32, (1, LANES), 1)
    lane = lane_i.astype(F32)

    @pl.when(g == 0)
    def _():
        h = _rms(x1_ref[...], gffn_ref[...])
        h_ref[...] = h.astype(BF16)
        lg = _mm(h, wrg_ref[...], precise_router) + brg_ref[...]
        top_g, g_idx = _first_index_of_max(lg, lane)
        pg_top = 1.0 / jnp.sum(jnp.exp(lg - top_g), axis=-1, keepdims=True)
        le = _mm(h, wre_ref[...], precise_router) + bre_ref[...]
        group_of_lane = (lane_i // EXPERTS_PER_GROUP).astype(F32)
        comb_ref[...] = _top2_weights(jnp.where(group_of_lane == g_idx, le, -jnp.inf), lane, pg_top)

    y = _group_experts(h_ref[...], comb_ref[...], lane_i, g, wg_ref, wu_ref, wd_ref)

    @pl.when(g == 0)
    def _():
        acc_ref[...] = y

    @pl.when(g > 0)
    def _():
        acc_ref[...] += y

    @pl.when(g == N_GROUPS - 1)
    def _():
        out_ref[...] = _embedding_gate(x1_ref[...] + acc_ref[...], p_ref, gple_ref, wpg_ref, wpe_ref)


def _channel(x1, p, gffn, wrg, brg, wre, bre, wg, wu, wd, gple, wpg, wpe, *, tm, precise_router):
    t = x1.shape[0]
    row = lambda i, g: (i, 0)
    const2 = lambda i, g: (0, 0)
    group = lambda i, g: (g, 0, 0)
    return pl.pallas_call(
        functools.partial(_channel_kernel, precise_router=precise_router),
        grid=(t // tm, N_GROUPS),
        in_specs=[
            pl.BlockSpec((tm, D_MODEL), row),
            pl.BlockSpec((tm, PLE_DIM), row),
            pl.BlockSpec((1, D_MODEL), const2),
            pl.BlockSpec((D_MODEL, LANES), const2),
            pl.BlockSpec((1, LANES), const2),
            pl.BlockSpec((D_MODEL, LANES), const2),
            pl.BlockSpec((1, LANES), const2),
            pl.BlockSpec((EXPERTS_PER_GROUP, D_MODEL, D_EXPERT), group),
            pl.BlockSpec((EXPERTS_PER_GROUP, D_MODEL, D_EXPERT), group),
            pl.BlockSpec((EXPERTS_PER_GROUP, D_EXPERT, D_MODEL), group),
            pl.BlockSpec((1, D_MODEL), const2),
            pl.BlockSpec((D_MODEL, D_MODEL), const2),
            pl.BlockSpec((PLE_DIM, D_MODEL), const2),
        ],
        out_specs=pl.BlockSpec((tm, D_MODEL), row),
        out_shape=jax.ShapeDtypeStruct((t, D_MODEL), F32),
        scratch_shapes=[pltpu.VMEM((tm, D_MODEL), BF16),
                        pltpu.VMEM((tm, LANES), F32),
                        pltpu.VMEM((tm, D_MODEL), F32)],
        compiler_params=pltpu.CompilerParams(
            dimension_semantics=("arbitrary", "arbitrary"), vmem_limit_bytes=VMEM_LIMIT),
        name="channel",
    )(x1, p, gffn, wrg, brg, wre, bre, wg, wu, wd, gple, wpg, wpe)


def _pad_lanes(a, fill):
    return jnp.pad(a, ((0, 0), (0, LANES - a.shape[1])), constant_values=fill)


def kernel(x_prompt, x_sample, cache_attn_k, cache_attn_v, p_prompt, p_sample, g_mix, w_in, g_gmlp_v, w_gmlp_s, b_gmlp_s, g_q, g_k, rel_bias, w_branch, w_out, g_ffn, w_router_group, b_router_group, w_router_expert, b_router_expert, w_exp_gate, w_exp_up, w_exp_down, g_ple, w_ple_gate, w_ple_proj):
    depth = g_mix.shape[0]
    batch, seq, _ = x_prompt.shape
    dec_batch, dec_seq, _ = x_sample.shape
    lc = cache_attn_k.shape[2]
    keep = min(N_PAST_CHUNKS * CHUNK, seq)
    tm = 512
    assert depth == 1 and keep == tm and seq % tm == 0 and seq % Q_BLOCK == 0
    assert dec_seq <= GMLP_CHUNK and GMLP_CHUNK % dec_seq == 0 and dec_batch * dec_seq == GMLP_CHUNK
    l = 0
    t_s = dec_batch * dec_seq

    row = lambda a: a.reshape(1, -1).astype(F32)
    w_in_b = w_in[l].astype(BF16)
    gq_t = row(jnp.tile(g_q[l], ATTN_HEADS))
    gk_t = row(jnp.tile(g_k[l], ATTN_HEADS))
    head_of_lane = jnp.arange(D_MODEL, dtype=jnp.int32) // HEAD_DIM
    seg = (head_of_lane[:, None] == jnp.arange(LANES, dtype=jnp.int32)[None, :]).astype(BF16)
    segt2 = jnp.concatenate([seg.T, seg.T], axis=0)
    ws_tril = jnp.tril(w_gmlp_s[l])
    ws_p = ws_tril.astype(BF16)
    bs_p = jnp.repeat(b_gmlp_s[l].T, GMLP_GROUP_DIM, axis=1).astype(F32)
    eye = jnp.eye(dec_batch, dtype=F32)
    ws_s = jnp.einsum('ab,gts->gatbs', eye, ws_tril[:, :dec_seq, :dec_seq]).reshape(
        GMLP_GROUPS, t_s, t_s)
    bs_s = jnp.tile(jnp.repeat(b_gmlp_s[l][:, :dec_seq].T, GMLP_GROUP_DIM, axis=1), (dec_batch, 1)).astype(F32)
    wb_b = w_branch[l].astype(BF16)
    wout_b = w_out[l].astype(BF16)
    wrg = _pad_lanes(w_router_group[l], 0.0)
    brg = _pad_lanes(row(b_router_group[l]), NEG)
    wre = _pad_lanes(w_router_expert[l], 0.0)
    bre = _pad_lanes(row(b_router_expert[l]), 0.0)
    wg_b = w_exp_gate[l].astype(BF16)
    wu_b = w_exp_up[l].astype(BF16)
    wd_b = w_exp_down[l].astype(BF16)
    wpg_b = w_ple_gate[l].astype(BF16)
    wpe_b = w_ple_proj[l].astype(BF16)
    bias_prompt = _prompt_bias_table(rel_bias[l])
    bias_c, bias_n = _sample_bias_tables(rel_bias[l], dec_seq, lc, PAST_LEN)

    wrgt = jnp.pad(w_router_group[l].T, ((0, ROUTE_ROWS - N_GROUPS), (0, 0))).astype(BF16)
    brgt = jnp.pad(b_router_group[l].astype(F32), (0, ROUTE_ROWS - N_GROUPS), constant_values=NEG).reshape(ROUTE_ROWS, 1)
    upper = jnp.asarray(np.triu(np.ones((tm, tm), np.float32), 1), BF16)

    xp = x_prompt.reshape(batch * seq, D_MODEL)
    u, vn, q, k, v, sga, sgb, kst, vst = _inproj(
        xp, row(g_mix[l]), w_in_b, row(g_gmlp_v[l]), gq_t, gk_t, seg, segt2,
        tm=tm, tiles_per_batch=seq // tm, emit_vn_state=False, precise=False)
    ob = _attn_prompt(q, k, v, bias_prompt, batch=batch, seq=seq)
    x1, gidx, rank = _merge(xp, u, vn, sga, sgb, ob, ws_p, bs_p, wb_b, wout_b,
                            (row(g_ffn[l]), wrgt, brgt, upper), tm=tm, precise=False)
    n_route_tiles = batch * seq // ROUTE_TILE + N_GROUPS
    pos, src_row, tile_group, n_valid = _route_plan(gidx.reshape(-1), rank.reshape(-1), n_tiles=n_route_tiles)
    y_sorted = _experts(src_row, tile_group, n_valid, x1, row(g_ffn[l]), wrg.astype(BF16), brg, wre.astype(BF16), bre,
                        wg_b, wu_b, wd_b, n_tiles=n_route_tiles)
    yp = _combine(pos, x1, p_prompt[l].reshape(batch * seq, PLE_DIM), row(g_ple[l]), wpg_b, wpe_b, y_sorted, tm=tm)

    xs = x_sample.reshape(t_s, D_MODEL)
    u, vn, q, k, v, sga, sgb, kss, vss, gvs = _inproj(
        xs, row(g_mix[l]), w_in[l], row(g_gmlp_v[l]), gq_t, gk_t, seg, segt2,
        tm=t_s, tiles_per_batch=1, emit_vn_state=True, precise=True)
    ob = _attn_sample(q, k, v, cache_attn_k[l].reshape(dec_batch, lc, D_MODEL),
                      cache_attn_v[l].reshape(dec_batch, lc, D_MODEL), bias_c, bias_n,
                      batch=dec_batch, rows=dec_seq)
    x1, = _merge(xs, u, vn, sga, sgb, ob, ws_s, bs_s, w_branch[l], w_out[l], tm=t_s, precise=True)
    ys = _channel(x1, p_sample[l].reshape(t_s, PLE_DIM), row(g_ffn[l]), wrg, brg, wre, bre, wg_b, wu_b, wd_b,
                  row(g_ple[l]), wpg_b, wpe_b, tm=t_s, precise_router=True)

    return (yp.reshape(batch, seq, D_MODEL),
            ys.reshape(dec_batch, dec_seq, D_MODEL),
            kst.reshape(1, batch, keep, ATTN_HEADS, HEAD_DIM),
            vst.reshape(1, batch, keep, ATTN_HEADS, HEAD_DIM),
            kss.reshape(1, dec_batch, dec_seq, ATTN_HEADS, HEAD_DIM),
            vss.reshape(1, dec_batch, dec_seq, ATTN_HEADS, HEAD_DIM),
            gvs.reshape(1, dec_batch, dec_seq, D_MODEL))
```

```python
import functools

import jax
import jax.numpy as jnp
import numpy as np
from jax import lax
from jax.experimental import pallas as pl
from jax.experimental.pallas import tpu as pltpu

F32 = jnp.float32
BF16 = jnp.bfloat16

D_MODEL = 1024
CHUNK = 64
N_PAST_CHUNKS = 8
HEAD_DIM = 64
ATTN_HEADS = D_MODEL // HEAD_DIM
MAX_REL = 128
GMLP_CHUNK = 128
GMLP_GROUPS = 4
GMLP_GROUP_DIM = D_MODEL // GMLP_GROUPS
N_SECTIONS = 7
N_GROUPS = 4
EXPERTS_PER_GROUP = 8
N_EXPERTS = N_GROUPS * EXPERTS_PER_GROUP
D_EXPERT = D_MODEL // 4
PLE_DIM = 256
EPS = 1e-6
NEG = -1e30
PAST_LEN = 1024

LANES = 128
Q_BLOCK = 2 * CHUNK
K_WINDOW = Q_BLOCK + N_PAST_CHUNKS * CHUNK
N_SHIFT = N_PAST_CHUNKS * CHUNK // Q_BLOCK + 1
BIAS_SPAN = K_WINDOW + (N_SHIFT - 1) * Q_BLOCK
ROUTE_ROWS = 16
ROUTE_TILE = 512
DMA_UNROLL = 8
ATTN_UNROLL = 6
VMEM_LIMIT = 56 * 1024 * 1024


def _rms(x, gain):
    return x * lax.rsqrt(jnp.mean(x * x, axis=-1, keepdims=True) + EPS) * gain


def _act_dtype(precise):
    return F32 if precise else BF16


def _mm(a, b, precise, dims=None):
    dims = dims or (((a.ndim - 1,), (0,)), ((), ()))
    if precise:
        return lax.dot_general(a.astype(F32), b.astype(F32), dims, precision=lax.Precision.HIGHEST,
                               preferred_element_type=F32)
    return lax.dot_general(a.astype(BF16), b.astype(BF16), dims, preferred_element_type=F32)


_NT = (((1,), (1,)), ((), ()))


def _head_sumsq(z, seg, precise):
    return _mm(z * z, seg, precise)


def _head_inv_rms(ssum, segt2, precise):
    inv = lax.rsqrt(ssum * (1.0 / HEAD_DIM) + EPS)
    if precise:
        return _mm(inv, segt2[:LANES], True)
    hi = inv.astype(BF16)
    lo = (inv - hi.astype(F32)).astype(BF16)
    return _mm(jnp.concatenate([hi, lo], axis=1), segt2, False)


def _inproj_kernel(x_ref, gmix_ref, w_ref, gv_ref, gq_ref, gk_ref, seg_ref, segt2_ref,
                   u_ref, vn_ref, q_ref, k_ref, v_ref, sga_ref, sgb_ref,
                   kst_ref, vst_ref, *maybe_vnst_ref, precise):
    act = _act_dtype(precise)
    h = _rms(x_ref[...], gmix_ref[...]).astype(act)

    def section(s):
        return _mm(h, w_ref[:, s * D_MODEL:(s + 1) * D_MODEL], precise)

    z_u = section(0)
    z_v = section(1)
    u_ref[...] = jax.nn.gelu(z_u).astype(act)
    z_q = section(2)
    vn = _rms(jax.nn.gelu(z_v), gv_ref[...])
    vn_ref[...] = vn.astype(act)
    if maybe_vnst_ref:
        maybe_vnst_ref[0][...] = vn
    z_k = section(3)
    ss_q = _head_sumsq(z_q, seg_ref[...], precise)
    v = section(4)
    ss_k = _head_sumsq(z_k, seg_ref[...], precise)
    z_ga = section(5)
    inv_q = _head_inv_rms(ss_q, segt2_ref[...], precise)
    z_gb = section(6)
    inv_k = _head_inv_rms(ss_k, segt2_ref[...], precise)

    v_ref[...] = v.astype(act)
    vst_ref[...] = v
    sga_ref[...] = jax.nn.sigmoid(z_ga).astype(act)
    q_ref[...] = (z_q * inv_q * gq_ref[...] * (HEAD_DIM ** -0.5)).astype(act)
    sgb_ref[...] = jax.nn.sigmoid(z_gb).astype(act)
    kn = z_k * inv_k * gk_ref[...]
    k_ref[...] = kn.astype(act)
    kst_ref[...] = kn


def _inproj(x, gmix, w_in, gv, gq_t, gk_t, seg, segt2, *, tm, tiles_per_batch, emit_vn_state, precise):
    t = x.shape[0]
    n_tiles = t // tm
    n_state = n_tiles // tiles_per_batch
    row = lambda i: (i, 0)
    const = lambda i: (0, 0)
    state = lambda i: (i // tiles_per_batch, 0)
    act = jax.ShapeDtypeStruct((t, D_MODEL), _act_dtype(precise))
    st = jax.ShapeDtypeStruct((n_state * tm, D_MODEL), F32)
    out_shape = [act] * 7 + [st, st] + ([st] if emit_vn_state else [])
    out_specs = [pl.BlockSpec((tm, D_MODEL), row)] * 7 + [pl.BlockSpec((tm, D_MODEL), state)] * (
        3 if emit_vn_state else 2)
    return pl.pallas_call(
        functools.partial(_inproj_kernel, precise=precise),
        grid=(n_tiles,),
        in_specs=[
            pl.BlockSpec((tm, D_MODEL), row),
            pl.BlockSpec((1, D_MODEL), const),
            pl.BlockSpec((D_MODEL, N_SECTIONS * D_MODEL), const, pipeline_mode=pl.Buffered(1)),
            pl.BlockSpec((1, D_MODEL), const),
            pl.BlockSpec((1, D_MODEL), const),
            pl.BlockSpec((1, D_MODEL), const),
            pl.BlockSpec((D_MODEL, LANES), const),
            pl.BlockSpec((2 * LANES, D_MODEL), const),
        ],
        out_specs=out_specs,
        out_shape=out_shape,
        compiler_params=pltpu.CompilerParams(
            dimension_semantics=("arbitrary",), vmem_limit_bytes=VMEM_LIMIT),
        name="inproj",
    )(x, gmix, w_in, gv, gq_t, gk_t, seg, segt2)


def _attn_prompt_kernel(q_ref, k_ref, v_ref, bias_ref, o_ref, *, seq):
    lane = lax.broadcasted_iota(jnp.int32, (1, LANES), 1)
    low_head = lane < HEAD_DIM
    zero = jnp.zeros((), BF16)

    def scores(qb, shift):
        q_start = pl.multiple_of(qb * Q_BLOCK, Q_BLOCK)
        k_start = pl.multiple_of((qb + shift) * Q_BLOCK - (K_WINDOW - Q_BLOCK), Q_BLOCK)
        q = q_ref[pl.ds(q_start, Q_BLOCK), :]
        k = k_ref[pl.ds(k_start, K_WINDOW), :]
        q2 = jnp.concatenate([jnp.where(low_head, q, zero), jnp.where(low_head, zero, q)], axis=0)
        bias = bias_ref[:, shift * Q_BLOCK:shift * Q_BLOCK + K_WINDOW]
        return lax.dot_general(q2, k, _NT, preferred_element_type=F32) + bias, q_start, k_start

    def attend(s, q_start, k_start):
        v = v_ref[pl.ds(k_start, K_WINDOW), :]
        m = jnp.max(s, axis=-1, keepdims=True)
        p = jnp.exp(s - m)
        l = jnp.sum(p, axis=-1, keepdims=True)
        o = jnp.dot(p.astype(BF16), v, preferred_element_type=F32) * (1.0 / l)
        o_ref[pl.ds(q_start, Q_BLOCK), :] = jnp.where(low_head, o[:Q_BLOCK], o[Q_BLOCK:]).astype(BF16)

    def blocks(specs):
        for args in [scores(qb, shift) for qb, shift in specs]:
            attend(*args)

    blocks([(qb, N_SHIFT - 1 - qb) for qb in range(N_SHIFT - 1)])

    def steady(j, carry):
        first = N_SHIFT - 1 + j * ATTN_UNROLL
        blocks([(first + u, 0) for u in range(ATTN_UNROLL)])
        return carry

    lax.fori_loop(0, (seq // Q_BLOCK - (N_SHIFT - 1)) // ATTN_UNROLL, steady, 0)


def _attn_prompt(q, k, v, bias_tab, *, batch, seq):
    q3, k3, v3 = (a.reshape(batch, seq, D_MODEL) for a in (q, k, v))
    head_pair = lambda hp, b: (b, 0, hp)
    spec = pl.BlockSpec((None, seq, LANES), head_pair)
    out = pl.pallas_call(
        functools.partial(_attn_prompt_kernel, seq=seq),
        grid=(ATTN_HEADS // 2, batch),
        in_specs=[spec, spec, spec,
                  pl.BlockSpec((None, 2 * Q_BLOCK, BIAS_SPAN), lambda hp, b: (hp, 0, 0))],
        out_specs=spec,
        out_shape=jax.ShapeDtypeStruct((batch, seq, D_MODEL), BF16),
        compiler_params=pltpu.CompilerParams(
            dimension_semantics=("arbitrary", "arbitrary"), vmem_limit_bytes=VMEM_LIMIT),
        name="attn_prompt",
    )(q3, k3, v3, bias_tab)
    return out.reshape(batch * seq, D_MODEL)


def _rel_bias_rows(rel_bias, q0, k0, rows, ncols):
    dmax = q0 - k0 + rows - 1
    total = rows + ncols - 1
    dmin = dmax - total + 1
    n_lo = min(max(-MAX_REL - dmin, 0), total)
    n_hi = min(max(dmax - MAX_REL, 0), total)
    mid0 = dmin + n_lo + MAX_REL
    asc = jnp.concatenate([jnp.repeat(rel_bias[:, :1], n_lo, axis=1),
                           rel_bias[:, mid0:mid0 + total - n_lo - n_hi],
                           jnp.repeat(rel_bias[:, -1:], n_hi, axis=1)], axis=1).astype(F32)
    desc = asc[:, ::-1]
    heads = desc.shape[0]
    period = jnp.concatenate([desc[:, rows - 1:], jnp.zeros((heads, 1), F32), desc[:, :rows - 1]], axis=1)
    flat = jnp.tile(period, (1, rows))[:, :rows * total]
    return flat.reshape(heads, rows, total)[:, :, :ncols]


def _band_ok(q_pos, k_pos):
    q_pos, k_pos = q_pos[:, None], k_pos[None, :]
    return (k_pos >= 0) & (k_pos // CHUNK <= q_pos // CHUNK) & (k_pos // CHUNK >= q_pos // CHUNK - N_PAST_CHUNKS)


def _prompt_bias_table(rel_bias):
    back = N_PAST_CHUNKS * CHUNK
    ok = _band_ok(back + np.arange(Q_BLOCK), np.arange(BIAS_SPAN))
    ext = jnp.where(ok[None], _rel_bias_rows(rel_bias, back, 0, Q_BLOCK, BIAS_SPAN), NEG)
    return ext.reshape(ATTN_HEADS // 2, 2 * Q_BLOCK, BIAS_SPAN)


def _attn_sample_kernel(q_ref, k_ref, v_ref, ck_ref, cv_ref, bc_ref, bn_ref, o_ref):
    rows = q_ref.shape[0]
    lane = lax.broadcasted_iota(jnp.int32, (1, LANES), 1)
    low_head = lane < HEAD_DIM
    scored = []
    for hp in range(ATTN_HEADS // 2):
        cols = slice(hp * LANES, (hp + 1) * LANES)
        q = q_ref[:, cols]
        q2 = jnp.concatenate([jnp.where(low_head, q, 0.0), jnp.where(low_head, 0.0, q)], axis=0)
        s_c = _mm(q2, ck_ref[:, cols], True, _NT) + bc_ref[hp]
        s_n = _mm(q2, k_ref[:, cols], True, _NT) + bn_ref[hp]
        scored.append((cols, s_c, s_n))
    for cols, s_c, s_n in scored:
        m = jnp.maximum(jnp.max(s_c, axis=-1, keepdims=True), jnp.max(s_n, axis=-1, keepdims=True))
        p_c = jnp.exp(s_c - m)
        p_n = jnp.exp(s_n - m)
        l = jnp.sum(p_c, axis=-1, keepdims=True) + jnp.sum(p_n, axis=-1, keepdims=True)
        o = (_mm(p_c, cv_ref[:, cols], True) + _mm(p_n, v_ref[:, cols], True)) * (1.0 / l)
        o_ref[:, cols] = jnp.where(low_head, o[:rows], o[rows:])


def _attn_sample(q, k, v, cache_k, cache_v, bias_c, bias_n, *, batch, rows):
    lc = cache_k.shape[1]
    new = pl.BlockSpec((rows, D_MODEL), lambda b: (b, 0))
    cache = pl.BlockSpec((None, lc, D_MODEL), lambda b: (b, 0, 0))
    return pl.pallas_call(
        _attn_sample_kernel,
        grid=(batch,),
        in_specs=[new, new, new, cache, cache,
                  pl.BlockSpec((ATTN_HEADS // 2, 2 * rows, lc), lambda b: (0, 0, 0)),
                  pl.BlockSpec((ATTN_HEADS // 2, 2 * rows, rows), lambda b: (0, 0, 0))],
        out_specs=new,
        out_shape=jax.ShapeDtypeStruct((batch * rows, D_MODEL), F32),
        compiler_params=pltpu.CompilerParams(
            dimension_semantics=("arbitrary",), vmem_limit_bytes=VMEM_LIMIT),
        name="attn_sample",
    )(q, k, v, cache_k, cache_v, bias_c, bias_n)


def _sample_bias_tables(rel_bias, rows, lc, past_len):
    q_pos = past_len + np.arange(rows)
    c_pos = past_len - lc + np.arange(lc)
    bias_c = jnp.where(_band_ok(q_pos, c_pos)[None], _rel_bias_rows(rel_bias, past_len, past_len - lc, rows, lc), NEG)
    bias_n = jnp.where(_band_ok(q_pos, q_pos)[None], _rel_bias_rows(rel_bias, past_len, past_len, rows, rows), NEG)
    pair = lambda tab: tab.reshape(ATTN_HEADS // 2, 2 * rows, tab.shape[-1])
    return pair(bias_c), pair(bias_n)


def _route_rows(x1, gffn_ref, wrgt_ref, brgt_ref, upper_ref, gidx_ref, rank_ref, base_ref):
    tm = x1.shape[0]
    h = _rms(x1, gffn_ref[...]).astype(BF16)
    lgt = lax.dot_general(wrgt_ref[...], h, _NT, preferred_element_type=F32) + brgt_ref[...]
    sub = lax.broadcasted_iota(jnp.int32, (ROUTE_ROWS, tm), 0).astype(F32)
    top = jnp.max(lgt, axis=0, keepdims=True)
    gi = jnp.min(jnp.where(lgt == top, sub, float(ROUTE_ROWS)), axis=0, keepdims=True)
    onehot = jnp.where(sub == gi, 1.0, 0.0)
    before = jnp.dot(onehot.astype(BF16), upper_ref[...], preferred_element_type=F32)
    base = base_ref[...]
    rank = jnp.sum(onehot * (before + base[:, :1]), axis=0, keepdims=True)
    base_ref[...] = base + jnp.sum(onehot, axis=1, keepdims=True)
    gidx_ref[...] = gi.astype(jnp.int32)
    rank_ref[...] = rank.astype(jnp.int32)


def _merge_kernel(x_ref, u_ref, vn_ref, sga_ref, sgb_ref, ob_ref, ws_ref, bs_ref, wb_ref, wout_ref,
                  *rest, tm, precise, route):
    if route:
        gffn_ref, wrgt_ref, brgt_ref, upper_ref, x1_ref, gidx_ref, rank_ref, oa_ref, base_ref = rest
    else:
        x1_ref, oa_ref = rest
    act = _act_dtype(precise)
    if route:
        @pl.when(pl.program_id(0) == 0)
        def _():
            base_ref[...] = jnp.zeros_like(base_ref)

    b_b = _mm(ob_ref[...], wb_ref[1], precise)
    for c in range(tm // GMLP_CHUNK):
        rows = slice(c * GMLP_CHUNK, (c + 1) * GMLP_CHUNK)
        for g in range(GMLP_GROUPS):
            cols = slice(g * GMLP_GROUP_DIM, (g + 1) * GMLP_GROUP_DIM)
            mix = _mm(ws_ref[g], vn_ref[rows, cols], precise) + bs_ref[:, cols]
            oa_ref[rows, cols] = (u_ref[rows, cols].astype(F32) * mix).astype(act)
    b_a = _mm(oa_ref[...], wb_ref[0], precise)
    m = sga_ref[...].astype(F32) * b_a + sgb_ref[...].astype(F32) * b_b
    x1 = x_ref[...] + _mm(m, wout_ref[...], precise)
    x1_ref[...] = x1
    if route:
        _route_rows(x1, gffn_ref, wrgt_ref, brgt_ref, upper_ref, gidx_ref, rank_ref, base_ref)


def _merge(x, u, vn, sga, sgb, ob, ws, bs_full, wb, wout, route_params=None, *, tm, precise):
    t = x.shape[0]
    n_tiles = t // tm
    route = route_params is not None
    row = pl.BlockSpec((tm, D_MODEL), lambda i: (i, 0))
    const = lambda i: (0, 0)
    in_specs = [row, row, row, row, row, row,
                pl.BlockSpec((GMLP_GROUPS, GMLP_CHUNK, GMLP_CHUNK), lambda i: (0, 0, 0)),
                pl.BlockSpec((GMLP_CHUNK, D_MODEL), const),
                pl.BlockSpec((2, D_MODEL, D_MODEL), lambda i: (0, 0, 0)),
                pl.BlockSpec((D_MODEL, D_MODEL), const)]
    out_specs = [row]
    out_shape = [jax.ShapeDtypeStruct((t, D_MODEL), F32)]
    scratch = [pltpu.VMEM((tm, D_MODEL), _act_dtype(precise))]
    args = [x, u, vn, sga, sgb, ob, ws, bs_full, wb, wout]
    if route:
        in_specs += [pl.BlockSpec((1, D_MODEL), const), pl.BlockSpec((ROUTE_ROWS, D_MODEL), const),
                     pl.BlockSpec((ROUTE_ROWS, 1), const), pl.BlockSpec((tm, tm), const)]
        tok_row = pl.BlockSpec((None, 1, tm), lambda i: (i, 0, 0))
        out_specs += [tok_row, tok_row]
        out_shape += [jax.ShapeDtypeStruct((n_tiles, 1, tm), jnp.int32)] * 2
        scratch += [pltpu.VMEM((ROUTE_ROWS, LANES), F32)]
        args += list(route_params)
    return pl.pallas_call(
        functools.partial(_merge_kernel, tm=tm, precise=precise, route=route),
        grid=(n_tiles,),
        in_specs=in_specs,
        out_specs=out_specs,
        out_shape=out_shape,
        scratch_shapes=scratch,
        compiler_params=pltpu.CompilerParams(
            dimension_semantics=("arbitrary",), vmem_limit_bytes=VMEM_LIMIT),
        name="merge",
    )(*args)


def _first_index_of_max(vals, lane):
    top = jnp.max(vals, axis=-1, keepdims=True)
    idx = jnp.min(jnp.where(vals == top, lane, float(LANES)), axis=-1, keepdims=True)
    return top, idx


def _top2_weights(le_in_group, lane, pg_top):
    t1, i1 = _first_index_of_max(le_in_group, lane)
    t2, i2 = _first_index_of_max(jnp.where(lane == i1, -jnp.inf, le_in_group), lane)
    e2 = jnp.exp(t2 - t1)
    den = 1.0 + e2
    return jnp.where(lane == i1, pg_top * (1.0 / den), jnp.where(lane == i2, pg_top * (e2 / den), 0.0))


def _group_experts(h, comb, lane_i, g, wg_ref, wu_ref, wd_ref):
    hids = []
    for j in range(EXPERTS_PER_GROUP):
        w_tok = jnp.sum(jnp.where(lane_i == g * EXPERTS_PER_GROUP + j, comb, 0.0), axis=-1, keepdims=True)
        a = jnp.dot(h, wg_ref[j], preferred_element_type=F32)
        b = jnp.dot(h, wu_ref[j], preferred_element_type=F32)
        hids.append((jax.nn.silu(a) * b * w_tok).astype(BF16))
    hid = jnp.concatenate(hids, axis=1)
    return jnp.dot(hid, wd_ref[...].reshape(EXPERTS_PER_GROUP * D_EXPERT, D_MODEL), preferred_element_type=F32)


def _embedding_gate(x2, p_ref, gple_ref, wpg_ref, wpe_ref):
    pe = jnp.dot(p_ref[...].astype(BF16), wpe_ref[...], preferred_element_type=F32)
    gate = jax.nn.sigmoid(jnp.dot(_rms(x2, gple_ref[...]).astype(BF16), wpg_ref[...], preferred_element_type=F32))
    return x2 + gate * pe


def _gather_rows(src_hbm, index_ref, first, buf, slot, sem):
    if isinstance(slot, int):
        for r in range(buf.shape[1]):
            pltpu.make_async_copy(src_hbm.at[index_ref[first + r]], buf.at[slot, r], sem.at[slot]).start(
                priority=r % 2)
        return

    def issue(r, carry):
        pltpu.make_async_copy(src_hbm.at[index_ref[first + r]], buf.at[slot, r], sem.at[slot]).start()
        return carry
    lax.fori_loop(0, buf.shape[1], issue, 0, unroll=DMA_UNROLL)


def _gather_next_tile(src_hbm, index_ref, i, n_tiles, buf, sem):
    for parity in range(2):
        @pl.when((i + 1 < n_tiles) & (i % 2 == parity))
        def _():
            _gather_rows(src_hbm, index_ref, (i + 1) * buf.shape[1], buf, 1 - parity, sem)


def _wait_rows(src_hbm, buf, slot, sem):
    pltpu.make_async_copy(src_hbm.at[pl.ds(0, buf.shape[1])], buf.at[slot], sem.at[slot]).wait()


def _experts_kernel(src_row_ref, tile_group_ref, n_valid_ref, x1_hbm, gffn_ref, wrg_ref, brg_ref, wre_ref, bre_ref,
                    wg_ref, wu_ref, wd_ref, ys_ref, xbuf, sems):
    i = pl.program_id(0)
    n_valid = n_valid_ref[0]

    @pl.when(i == 0)
    def _():
        _gather_rows(x1_hbm, src_row_ref, 0, xbuf, 0, sems)

    _gather_next_tile(x1_hbm, src_row_ref, i, n_valid, xbuf, sems)

    @pl.when(i < n_valid)
    def _():
        slot = i % 2
        _wait_rows(x1_hbm, xbuf, slot, sems)
        g = tile_group_ref[i]
        lane_i = lax.broadcasted_iota(jnp.int32, (1, LANES), 1)
        lane = lane_i.astype(F32)
        h = _rms(xbuf[slot], gffn_ref[...]).astype(BF16)
        lg = _mm(h, wrg_ref[...], False) + brg_ref[...]
        eg = jnp.exp(lg - jnp.max(lg, axis=-1, keepdims=True))
        pg = jnp.sum(jnp.where(lane_i == g, eg, 0.0), axis=-1, keepdims=True) / jnp.sum(eg, axis=-1, keepdims=True)
        le = _mm(h, wre_ref[...], False) + bre_ref[...]
        comb = _top2_weights(jnp.where(lane_i // EXPERTS_PER_GROUP == g, le, -jnp.inf), lane, pg)
        ys_ref[...] = _group_experts(h, comb, lane_i, g, wg_ref, wu_ref, wd_ref)

    @pl.when(i >= n_valid)
    def _():
        ys_ref[...] = jnp.zeros_like(ys_ref)


def _experts(src_row, tile_group, n_valid, x1, gffn, wrg, brg, wre, bre, wg, wu, wd, *, n_tiles):
    const = lambda i, sr, tg, nv: (0, 0)
    group = lambda i, sr, tg, nv: (tg[i], 0, 0)
    return pl.pallas_call(
        _experts_kernel,
        grid_spec=pltpu.PrefetchScalarGridSpec(
            num_scalar_prefetch=3,
            grid=(n_tiles,),
            in_specs=[
                pl.BlockSpec(memory_space=pl.ANY),
                pl.BlockSpec((1, D_MODEL), const),
                pl.BlockSpec((D_MODEL, LANES), const),
                pl.BlockSpec((1, LANES), const),
                pl.BlockSpec((D_MODEL, LANES), const),
                pl.BlockSpec((1, LANES), const),
                pl.BlockSpec((EXPERTS_PER_GROUP, D_MODEL, D_EXPERT), group),
                pl.BlockSpec((EXPERTS_PER_GROUP, D_MODEL, D_EXPERT), group),
                pl.BlockSpec((EXPERTS_PER_GROUP, D_EXPERT, D_MODEL), group),
            ],
            out_specs=pl.BlockSpec((ROUTE_TILE, D_MODEL), lambda i, sr, tg, nv: (i, 0)),
            scratch_shapes=[pltpu.VMEM((2, ROUTE_TILE, D_MODEL), F32), pltpu.SemaphoreType.DMA((2,))]),
        out_shape=jax.ShapeDtypeStruct((n_tiles * ROUTE_TILE, D_MODEL), F32),
        compiler_params=pltpu.CompilerParams(
            dimension_semantics=("arbitrary",), vmem_limit_bytes=VMEM_LIMIT),
        name="experts",
    )(src_row, tile_group, n_valid, x1, gffn, wrg, brg, wre, bre, wg, wu, wd)


def _combine_kernel(pos_ref, x1_ref, p_ref, gple_ref, wpg_ref, wpe_ref, ys_hbm, out_ref, ybuf, sems, *, tm):
    i = pl.program_id(0)

    @pl.when(i == 0)
    def _():
        _gather_rows(ys_hbm, pos_ref, 0, ybuf, 0, sems)

    _gather_next_tile(ys_hbm, pos_ref, i, pl.num_programs(0), ybuf, sems)

    slot = i % 2
    _wait_rows(ys_hbm, ybuf, slot, sems)
    out_ref[...] = _embedding_gate(x1_ref[...] + ybuf[slot], p_ref, gple_ref, wpg_ref, wpe_ref)


def _combine(pos, x1, p, gple, wpg, wpe, ys, *, tm):
    t = x1.shape[0]
    row = lambda i, pos: (i, 0)
    const = lambda i, pos: (0, 0)
    return pl.pallas_call(
        functools.partial(_combine_kernel, tm=tm),
        grid_spec=pltpu.PrefetchScalarGridSpec(
            num_scalar_prefetch=1,
            grid=(t // tm,),
            in_specs=[pl.BlockSpec((tm, D_MODEL), row),
                      pl.BlockSpec((tm, PLE_DIM), row),
                      pl.BlockSpec((1, D_MODEL), const),
                      pl.BlockSpec((D_MODEL, D_MODEL), const),
                      pl.BlockSpec((PLE_DIM, D_MODEL), const),
                      pl.BlockSpec(memory_space=pl.ANY)],
            out_specs=pl.BlockSpec((tm, D_MODEL), row),
            scratch_shapes=[pltpu.VMEM((2, tm, D_MODEL), F32), pltpu.SemaphoreType.DMA((2,))]),
        out_shape=jax.ShapeDtypeStruct((t, D_MODEL), F32),
        compiler_params=pltpu.CompilerParams(
            dimension_semantics=("arbitrary",), vmem_limit_bytes=VMEM_LIMIT),
        name="combine",
    )(pos, x1, p, gple, wpg, wpe, ys)


def _route_plan(gidx, rank, *, n_tiles):
    t = gidx.shape[0]
    groups = jnp.arange(N_GROUPS, dtype=jnp.int32)
    member = gidx[None, :] == groups[:, None]
    counts = jnp.sum(member, axis=1, dtype=jnp.int32)
    padded = (counts + ROUTE_TILE - 1) // ROUTE_TILE * ROUTE_TILE
    ends = jnp.cumsum(padded)
    offsets = ends - padded
    pos = rank + jnp.sum(jnp.where(member, offsets[:, None], 0), axis=0, dtype=jnp.int32)
    src_row = jnp.zeros((n_tiles * ROUTE_TILE,), jnp.int32).at[pos].set(
        jnp.arange(t, dtype=jnp.int32), unique_indices=True, indices_are_sorted=False)
    n_valid = jnp.maximum(ends[-1] // ROUTE_TILE, 1)
    tile_start = jnp.minimum(jnp.arange(n_tiles, dtype=jnp.int32), n_valid - 1) * ROUTE_TILE
    tile_group = jnp.minimum(jnp.sum(tile_start[:, None] >= ends[None, :], axis=1, dtype=jnp.int32), N_GROUPS - 1)
    return pos, src_row, tile_group, n_valid.reshape(1).astype(jnp.int32)


def _channel_kernel(x1_ref, p_ref, gffn_ref, wrg_ref, brg_ref, wre_ref, bre_ref,
                    wg_ref, wu_ref, wd_ref, gple_ref, wpg_ref, wpe_ref,
                    out_ref, h_ref, comb_ref, acc_ref, *, precise_router):
    g = pl.program_id(1)
    lane_i = lax.broadcasted_iota(jnp.int32, (1, LANES), 1)
    lane = lane_i.astype(F32)

    @pl.when(g == 0)
    def _():
        h = _rms(x1_ref[...], gffn_ref[...])
        h_ref[...] = h.astype(BF16)
        lg = _mm(h, wrg_ref[...], precise_router) + brg_ref[...]
        top_g, g_idx = _first_index_of_max(lg, lane)
        pg_top = 1.0 / jnp.sum(jnp.exp(lg - top_g), axis=-1, keepdims=True)
        le = _mm(h, wre_ref[...], precise_router) + bre_ref[...]
        group_of_lane = (lane_i // EXPERTS_PER_GROUP).astype(F32)
        comb_ref[...] = _top2_weights(jnp.where(group_of_lane == g_idx, le, -jnp.inf), lane, pg_top)

    y = _group_experts(h_ref[...], comb_ref[...], lane_i, g, wg_ref, wu_ref, wd_ref)

    @pl.when(g == 0)
    def _():
        acc_ref[...] = y

    @pl.when(g > 0)
    def _():
        acc_ref[...] += y

    @pl.when(g == N_GROUPS - 1)
    def _():
        out_ref[...] = _embedding_gate(x1_ref[...] + acc_ref[...], p_ref, gple_ref, wpg_ref, wpe_ref)


def _channel(x1, p, gffn, wrg, brg, wre, bre, wg, wu, wd, gple, wpg, wpe, *, tm, precise_router):
    t = x1.shape[0]
    row = lambda i, g: (i, 0)
    const2 = lambda i, g: (0, 0)
    group = lambda i, g: (g, 0, 0)
    return pl.pallas_call(
        functools.partial(_channel_kernel, precise_router=precise_router),
        grid=(t // tm, N_GROUPS),
        in_specs=[
            pl.BlockSpec((tm, D_MODEL), row),
            pl.BlockSpec((tm, PLE_DIM), row),
            pl.BlockSpec((1, D_MODEL), const2),
            pl.BlockSpec((D_MODEL, LANES), const2),
            pl.BlockSpec((1, LANES), const2),
            pl.BlockSpec((D_MODEL, LANES), const2),
            pl.BlockSpec((1, LANES), const2),
            pl.BlockSpec((EXPERTS_PER_GROUP, D_MODEL, D_EXPERT), group),
            pl.BlockSpec((EXPERTS_PER_GROUP, D_MODEL, D_EXPERT), group),
            pl.BlockSpec((EXPERTS_PER_GROUP, D_EXPERT, D_MODEL), group),
            pl.BlockSpec((1, D_MODEL), const2),
            pl.BlockSpec((D_MODEL, D_MODEL), const2),
            pl.BlockSpec((PLE_DIM, D_MODEL), const2),
        ],
        out_specs=pl.BlockSpec((tm, D_MODEL), row),
        out_shape=jax.ShapeDtypeStruct((t, D_MODEL), F32),
        scratch_shapes=[pltpu.VMEM((tm, D_MODEL), BF16),
                        pltpu.VMEM((tm, LANES), F32),
                        pltpu.VMEM((tm, D_MODEL), F32)],
        compiler_params=pltpu.CompilerParams(
            dimension_semantics=("arbitrary", "arbitrary"), vmem_limit_bytes=VMEM_LIMIT),
        name="channel",
    )(x1, p, gffn, wrg, brg, wre, bre, wg, wu, wd, gple, wpg, wpe)


def _pad_lanes(a, fill):
    return jnp.pad(a, ((0, 0), (0, LANES - a.shape[1])), constant_values=fill)


def kernel(x_prompt, x_sample, cache_attn_k, cache_attn_v, p_prompt, p_sample, g_mix, w_in, g_gmlp_v, w_gmlp_s, b_gmlp_s, g_q, g_k, rel_bias, w_branch, w_out, g_ffn, w_router_group, b_router_group, w_router_expert, b_router_expert, w_exp_gate, w_exp_up, w_exp_down, g_ple, w_ple_gate, w_ple_proj):
    depth = g_mix.shape[0]
    batch, seq, _ = x_prompt.shape
    dec_batch, dec_seq, _ = x_sample.shape
    lc = cache_attn_k.shape[2]
    keep = min(N_PAST_CHUNKS * CHUNK, seq)
    tm = 512
    assert depth == 1 and keep == tm and seq % tm == 0 and seq % Q_BLOCK == 0
    assert dec_seq <= GMLP_CHUNK and GMLP_CHUNK % dec_seq == 0 and dec_batch * dec_seq == GMLP_CHUNK
    l = 0
    t_s = dec_batch * dec_seq

    row = lambda a: a.reshape(1, -1).astype(F32)
    w_in_b = w_in[l].astype(BF16)
    gq_t = row(jnp.tile(g_q[l], ATTN_HEADS))
    gk_t = row(jnp.tile(g_k[l], ATTN_HEADS))
    head_of_lane = jnp.arange(D_MODEL, dtype=jnp.int32) // HEAD_DIM
    seg = (head_of_lane[:, None] == jnp.arange(LANES, dtype=jnp.int32)[None, :]).astype(BF16)
    segt2 = jnp.concatenate([seg.T, seg.T], axis=0)
    ws_tril = jnp.tril(w_gmlp_s[l])
    ws_p = ws_tril.astype(BF16)
    bs_p = jnp.repeat(b_gmlp_s[l].T, GMLP_GROUP_DIM, axis=1).astype(F32)
    eye = jnp.eye(dec_batch, dtype=F32)
    ws_s = jnp.einsum('ab,gts->gatbs', eye, ws_tril[:, :dec_seq, :dec_seq]).reshape(
        GMLP_GROUPS, t_s, t_s)
    bs_s = jnp.tile(jnp.repeat(b_gmlp_s[l][:, :dec_seq].T, GMLP_GROUP_DIM, axis=1), (dec_batch, 1)).astype(F32)
    wb_b = w_branch[l].astype(BF16)
    wout_b = w_out[l].astype(BF16)
    wrg = _pad_lanes(w_router_group[l], 0.0)
    brg = _pad_lanes(row(b_router_group[l]), NEG)
    wre = _pad_lanes(w_router_expert[l], 0.0)
    bre = _pad_lanes(row(b_router_expert[l]), 0.0)
    wg_b = w_exp_gate[l].astype(BF16)
    wu_b = w_exp_up[l].astype(BF16)
    wd_b = w_exp_down[l].astype(BF16)
    wpg_b = w_ple_gate[l].astype(BF16)
    wpe_b = w_ple_proj[l].astype(BF16)
    bias_prompt = _prompt_bias_table(rel_bias[l])
    bias_c, bias_n = _sample_bias_tables(rel_bias[l], dec_seq, lc, PAST_LEN)

    wrgt = jnp.pad(w_router_group[l].T, ((0, ROUTE_ROWS - N_GROUPS), (0, 0))).astype(BF16)
    brgt = jnp.pad(b_router_group[l].astype(F32), (0, ROUTE_ROWS - N_GROUPS), constant_values=NEG).reshape(ROUTE_ROWS, 1)
    upper = jnp.asarray(np.triu(np.ones((tm, tm), np.float32), 1), BF16)

    xp = x_prompt.reshape(batch * seq, D_MODEL)
    u, vn, q, k, v, sga, sgb, kst, vst = _inproj(
        xp, row(g_mix[l]), w_in_b, row(g_gmlp_v[l]), gq_t, gk_t, seg, segt2,
        tm=tm, tiles_per_batch=seq // tm, emit_vn_state=False, precise=False)
    ob = _attn_prompt(q, k, v, bias_prompt, batch=batch, seq=seq)
    x1, gidx, rank = _merge(xp, u, vn, sga, sgb, ob, ws_p, bs_p, wb_b, wout_b,
                            (row(g_ffn[l]), wrgt, brgt, upper), tm=tm, precise=False)
    n_route_tiles = batch * seq // ROUTE_TILE + N_GROUPS
    pos, src_row, tile_group, n_valid = _route_plan(gidx.reshape(-1), rank.reshape(-1), n_tiles=n_route_tiles)
    y_sorted = _experts(src_row, tile_group, n_valid, x1, row(g_ffn[l]), wrg.astype(BF16), brg, wre.astype(BF16), bre,
                        wg_b, wu_b, wd_b, n_tiles=n_route_tiles)
    yp = _combine(pos, x1, p_prompt[l].reshape(batch * seq, PLE_DIM), row(g_ple[l]), wpg_b, wpe_b, y_sorted, tm=tm)

    xs = x_sample.reshape(t_s, D_MODEL)
    u, vn, q, k, v, sga, sgb, kss, vss, gvs = _inproj(
        xs, row(g_mix[l]), w_in[l], row(g_gmlp_v[l]), gq_t, gk_t, seg, segt2,
        tm=t_s, tiles_per_batch=1, emit_vn_state=True, precise=True)
    ob = _attn_sample(q, k, v, cache_attn_k[l].reshape(dec_batch, lc, D_MODEL),
                      cache_attn_v[l].reshape(dec_batch, lc, D_MODEL), bias_c, bias_n,
                      batch=dec_batch, rows=dec_seq)
    x1, = _merge(xs, u, vn, sga, sgb, ob, ws_s, bs_s, w_branch[l], w_out[l], tm=t_s, precise=True)
    ys = _channel(x1, p_sample[l].reshape(t_s, PLE_DIM), row(g_ffn[l]), wrg, brg, wre, bre, wg_b, wu_b, wd_b,
                  row(g_ple[l]), wpg_b, wpe_b, tm=t_s, precise_router=True)

    return (yp.reshape(batch, seq, D_MODEL),
            ys.reshape(dec_batch, dec_seq, D_MODEL),
            kst.reshape(1, batch, keep, ATTN_HEADS, HEAD_DIM),
            vst.reshape(1, batch, keep, ATTN_HEADS, HEAD_DIM),
            kss.reshape(1, dec_batch, dec_seq, ATTN_HEADS, HEAD_DIM),
            vss.reshape(1, dec_batch, dec_seq, ATTN_HEADS, HEAD_DIM),
            gvs.reshape(1, dec_batch, dec_seq, D_MODEL))
```

```python
import functools

import jax
import jax.numpy as jnp
import numpy as np
from jax import lax
from jax.experimental import pallas as pl
from jax.experimental.pallas import tpu as pltpu

F32 = jnp.float32
BF16 = jnp.bfloat16

D_MODEL = 1024
CHUNK = 64
N_PAST_CHUNKS = 8
HEAD_DIM = 64
ATTN_HEADS = D_MODEL // HEAD_DIM
MAX_REL = 128
GMLP_CHUNK = 128
GMLP_GROUPS = 4
GMLP_GROUP_DIM = D_MODEL // GMLP_GROUPS
N_SECTIONS = 7
N_GROUPS = 4
EXPERTS_PER_GROUP = 8
N_EXPERTS = N_GROUPS * EXPERTS_PER_GROUP
D_EXPERT = D_MODEL // 4
PLE_DIM = 256
EPS = 1e-6
NEG = -1e30
PAST_LEN = 1024
SCORE_SCALE = HEAD_DIM ** -0.5
LOG2_E = 1.4426950408889634

LANES = 128
Q_BLOCK = 2 * CHUNK
K_WINDOW = Q_BLOCK + N_PAST_CHUNKS * CHUNK
N_SHIFT = N_PAST_CHUNKS * CHUNK // Q_BLOCK + 1
BIAS_SPAN = K_WINDOW + (N_SHIFT - 1) * Q_BLOCK
ROUTE_ROWS = 16
ROUTE_TILE = 512
DMA_UNROLL = 8
ATTN_UNROLL = 6
VMEM_LIMIT = 56 * 1024 * 1024


def _rms(x, gain):
    return x * lax.rsqrt(jnp.mean(x * x, axis=-1, keepdims=True) + EPS) * gain


def _act_dtype(precise):
    return F32 if precise else BF16


def _mm(a, b, precise, dims=None):
    dims = dims or (((a.ndim - 1,), (0,)), ((), ()))
    if precise:
        return lax.dot_general(a.astype(F32), b.astype(F32), dims, precision=lax.Precision.HIGHEST,
                               preferred_element_type=F32)
    return lax.dot_general(a.astype(BF16), b.astype(BF16), dims, preferred_element_type=F32)


_NT = (((1,), (1,)), ((), ()))


def _head_sumsq(z, seg, precise):
    return _mm(z * z, seg, precise)


def _head_inv_rms(ssum, segt2, precise):
    inv = lax.rsqrt(ssum * (1.0 / HEAD_DIM) + EPS)
    if precise:
        return _mm(inv, segt2[:LANES], True)
    hi = inv.astype(BF16)
    lo = (inv - hi.astype(F32)).astype(BF16)
    return _mm(jnp.concatenate([hi, lo], axis=1), segt2, False)


def _inproj_kernel(x_ref, gmix_ref, w_ref, gv_ref, gq_ref, gk_ref, seg_ref, segt2_ref,
                   u_ref, vn_ref, q_ref, k_ref, v_ref, sga_ref, sgb_ref,
                   kst_ref, vst_ref, *maybe_vnst_ref, precise, q_scale):
    act = _act_dtype(precise)
    h = _rms(x_ref[...], gmix_ref[...]).astype(act)

    def section(s):
        return _mm(h, w_ref[:, s * D_MODEL:(s + 1) * D_MODEL], precise)

    z_u = section(0)
    z_v = section(1)
    u_ref[...] = jax.nn.gelu(z_u).astype(act)
    z_q = section(2)
    vn = _rms(jax.nn.gelu(z_v), gv_ref[...])
    vn_ref[...] = vn.astype(act)
    if maybe_vnst_ref:
        maybe_vnst_ref[0][...] = vn
    z_k = section(3)
    ss_q = _head_sumsq(z_q, seg_ref[...], precise)
    v = section(4)
    ss_k = _head_sumsq(z_k, seg_ref[...], precise)
    z_ga = section(5)
    inv_q = _head_inv_rms(ss_q, segt2_ref[...], precise)
    z_gb = section(6)
    inv_k = _head_inv_rms(ss_k, segt2_ref[...], precise)

    v_ref[...] = v.astype(act)
    vst_ref[...] = v
    sga_ref[...] = jax.nn.sigmoid(z_ga).astype(act)
    q_ref[...] = (z_q * inv_q * gq_ref[...] * q_scale).astype(act)
    sgb_ref[...] = jax.nn.sigmoid(z_gb).astype(act)
    kn = z_k * inv_k * gk_ref[...]
    k_ref[...] = kn.astype(act)
    kst_ref[...] = kn


def _inproj(x, gmix, w_in, gv, gq_t, gk_t, seg, segt2, *, tm, tiles_per_batch, emit_vn_state, precise, q_scale):
    t = x.shape[0]
    n_tiles = t // tm
    n_state = n_tiles // tiles_per_batch
    row = lambda i: (i, 0)
    const = lambda i: (0, 0)
    state = lambda i: (i // tiles_per_batch, 0)
    act = jax.ShapeDtypeStruct((t, D_MODEL), _act_dtype(precise))
    st = jax.ShapeDtypeStruct((n_state * tm, D_MODEL), F32)
    out_shape = [act] * 7 + [st, st] + ([st] if emit_vn_state else [])
    out_specs = [pl.BlockSpec((tm, D_MODEL), row)] * 7 + [pl.BlockSpec((tm, D_MODEL), state)] * (
        3 if emit_vn_state else 2)
    return pl.pallas_call(
        functools.partial(_inproj_kernel, precise=precise, q_scale=q_scale),
        grid=(n_tiles,),
        in_specs=[
            pl.BlockSpec((tm, D_MODEL), row),
            pl.BlockSpec((1, D_MODEL), const),
            pl.BlockSpec((D_MODEL, N_SECTIONS * D_MODEL), const, pipeline_mode=pl.Buffered(1)),
            pl.BlockSpec((1, D_MODEL), const),
            pl.BlockSpec((1, D_MODEL), const),
            pl.BlockSpec((1, D_MODEL), const),
            pl.BlockSpec((D_MODEL, LANES), const),
            pl.BlockSpec((2 * LANES, D_MODEL), const),
        ],
        out_specs=out_specs,
        out_shape=out_shape,
        compiler_params=pltpu.CompilerParams(
            dimension_semantics=("arbitrary",), vmem_limit_bytes=VMEM_LIMIT),
        name="inproj",
    )(x, gmix, w_in, gv, gq_t, gk_t, seg, segt2)


def _attn_prompt_kernel(q_ref, k_ref, v_ref, bias_ref, o_ref, *, seq):
    lane = lax.broadcasted_iota(jnp.int32, (1, LANES), 1)
    low_head = lane < HEAD_DIM
    zero = jnp.zeros((), BF16)

    def scores(qb, shift):
        q_start = pl.multiple_of(qb * Q_BLOCK, Q_BLOCK)
        k_start = pl.multiple_of((qb + shift) * Q_BLOCK - (K_WINDOW - Q_BLOCK), Q_BLOCK)
        q = q_ref[pl.ds(q_start, Q_BLOCK), :]
        k = k_ref[pl.ds(k_start, K_WINDOW), :]
        q2 = jnp.concatenate([jnp.where(low_head, q, zero), jnp.where(low_head, zero, q)], axis=0)
        bias = bias_ref[:, shift * Q_BLOCK:shift * Q_BLOCK + K_WINDOW]
        return lax.dot_general(q2, k, _NT, preferred_element_type=F32) + bias, q_start, k_start

    ones = jnp.ones((K_WINDOW, LANES), BF16)

    def attend(s, q_start, k_start):
        v = v_ref[pl.ds(k_start, K_WINDOW), :]
        m = jnp.max(s, axis=-1, keepdims=True)
        p = jnp.exp2((s - m).astype(BF16))
        ol = jnp.dot(p, jnp.concatenate([v, ones], axis=1), preferred_element_type=F32)
        o = ol[:, :LANES] * (1.0 / ol[:, LANES:])
        o_ref[pl.ds(q_start, Q_BLOCK), :] = jnp.where(low_head, o[:Q_BLOCK], o[Q_BLOCK:]).astype(BF16)

    def blocks(specs):
        for args in [scores(qb, shift) for qb, shift in specs]:
            attend(*args)

    blocks([(qb, N_SHIFT - 1 - qb) for qb in range(N_SHIFT - 1)])

    def steady(j, carry):
        first = N_SHIFT - 1 + j * ATTN_UNROLL
        blocks([(first + u, 0) for u in range(ATTN_UNROLL)])
        return carry

    lax.fori_loop(0, (seq // Q_BLOCK - (N_SHIFT - 1)) // ATTN_UNROLL, steady, 0)


def _attn_prompt(q, k, v, bias_tab, *, batch, seq):
    q3, k3, v3 = (a.reshape(batch, seq, D_MODEL) for a in (q, k, v))
    head_pair = lambda hp, b: (b, 0, hp)
    spec = pl.BlockSpec((None, seq, LANES), head_pair)
    out = pl.pallas_call(
        functools.partial(_attn_prompt_kernel, seq=seq),
        grid=(ATTN_HEADS // 2, batch),
        in_specs=[spec, spec, spec,
                  pl.BlockSpec((None, 2 * Q_BLOCK, BIAS_SPAN), lambda hp, b: (hp, 0, 0))],
        out_specs=spec,
        out_shape=jax.ShapeDtypeStruct((batch, seq, D_MODEL), BF16),
        compiler_params=pltpu.CompilerParams(
            dimension_semantics=("arbitrary", "arbitrary"), vmem_limit_bytes=VMEM_LIMIT),
        name="attn_prompt",
    )(q3, k3, v3, bias_tab)
    return out.reshape(batch * seq, D_MODEL)


def _rel_bias_rows(rel_bias, q0, k0, rows, ncols):
    dmax = q0 - k0 + rows - 1
    total = rows + ncols - 1
    dmin = dmax - total + 1
    n_lo = min(max(-MAX_REL - dmin, 0), total)
    n_hi = min(max(dmax - MAX_REL, 0), total)
    mid0 = dmin + n_lo + MAX_REL
    asc = jnp.concatenate([jnp.repeat(rel_bias[:, :1], n_lo, axis=1),
                           rel_bias[:, mid0:mid0 + total - n_lo - n_hi],
                           jnp.repeat(rel_bias[:, -1:], n_hi, axis=1)], axis=1).astype(F32)
    desc = asc[:, ::-1]
    heads = desc.shape[0]
    period = jnp.concatenate([desc[:, rows - 1:], jnp.zeros((heads, 1), F32), desc[:, :rows - 1]], axis=1)
    flat = jnp.tile(period, (1, rows))[:, :rows * total]
    return flat.reshape(heads, rows, total)[:, :, :ncols]


def _band_ok(q_pos, k_pos):
    q_pos, k_pos = q_pos[:, None], k_pos[None, :]
    return (k_pos >= 0) & (k_pos // CHUNK <= q_pos // CHUNK) & (k_pos // CHUNK >= q_pos // CHUNK - N_PAST_CHUNKS)


def _prompt_bias_table(rel_bias):
    back = N_PAST_CHUNKS * CHUNK
    ok = _band_ok(back + np.arange(Q_BLOCK), np.arange(BIAS_SPAN))
    ext = jnp.where(ok[None], _rel_bias_rows(rel_bias, back, 0, Q_BLOCK, BIAS_SPAN), NEG)
    return ext.reshape(ATTN_HEADS // 2, 2 * Q_BLOCK, BIAS_SPAN)


def _attn_sample_kernel(q_ref, k_ref, v_ref, ck_ref, cv_ref, bc_ref, bn_ref, o_ref):
    rows = q_ref.shape[0]
    lane = lax.broadcasted_iota(jnp.int32, (1, LANES), 1)
    low_head = lane < HEAD_DIM
    scored = []
    for hp in range(ATTN_HEADS // 2):
        cols = slice(hp * LANES, (hp + 1) * LANES)
        q = q_ref[:, cols]
        q2 = jnp.concatenate([jnp.where(low_head, q, 0.0), jnp.where(low_head, 0.0, q)], axis=0)
        s_c = _mm(q2, ck_ref[:, cols], True, _NT) + bc_ref[hp]
        s_n = _mm(q2, k_ref[:, cols], True, _NT) + bn_ref[hp]
        scored.append((cols, s_c, s_n))
    for cols, s_c, s_n in scored:
        m = jnp.maximum(jnp.max(s_c, axis=-1, keepdims=True), jnp.max(s_n, axis=-1, keepdims=True))
        p_c = jnp.exp(s_c - m)
        p_n = jnp.exp(s_n - m)
        l = jnp.sum(p_c, axis=-1, keepdims=True) + jnp.sum(p_n, axis=-1, keepdims=True)
        o = (_mm(p_c, cv_ref[:, cols], True) + _mm(p_n, v_ref[:, cols], True)) * (1.0 / l)
        o_ref[:, cols] = jnp.where(low_head, o[:rows], o[rows:])


def _attn_sample(q, k, v, cache_k, cache_v, bias_c, bias_n, *, batch, rows):
    lc = cache_k.shape[1]
    new = pl.BlockSpec((rows, D_MODEL), lambda b: (b, 0))
    cache = pl.BlockSpec((None, lc, D_MODEL), lambda b: (b, 0, 0))
    return pl.pallas_call(
        _attn_sample_kernel,
        grid=(batch,),
        in_specs=[new, new, new, cache, cache,
                  pl.BlockSpec((ATTN_HEADS // 2, 2 * rows, lc), lambda b: (0, 0, 0)),
                  pl.BlockSpec((ATTN_HEADS // 2, 2 * rows, rows), lambda b: (0, 0, 0))],
        out_specs=new,
        out_shape=jax.ShapeDtypeStruct((batch * rows, D_MODEL), F32),
        compiler_params=pltpu.CompilerParams(
            dimension_semantics=("arbitrary",), vmem_limit_bytes=VMEM_LIMIT),
        name="attn_sample",
    )(q, k, v, cache_k, cache_v, bias_c, bias_n)


def _sample_bias_tables(rel_bias, rows, lc, past_len):
    q_pos = past_len + np.arange(rows)
    c_pos = past_len - lc + np.arange(lc)
    bias_c = jnp.where(_band_ok(q_pos, c_pos)[None], _rel_bias_rows(rel_bias, past_len, past_len - lc, rows, lc), NEG)
    bias_n = jnp.where(_band_ok(q_pos, q_pos)[None], _rel_bias_rows(rel_bias, past_len, past_len, rows, rows), NEG)
    pair = lambda tab: tab.reshape(ATTN_HEADS // 2, 2 * rows, tab.shape[-1])
    return pair(bias_c), pair(bias_n)


def _route_rows(x1, gffn_ref, wrgt_ref, brgt_ref, upper_ref, gidx_ref, rank_ref, base_ref):
    tm = x1.shape[0]
    h = _rms(x1, gffn_ref[...]).astype(BF16)
    lgt = lax.dot_general(wrgt_ref[...], h, _NT, preferred_element_type=F32) + brgt_ref[...]
    sub = lax.broadcasted_iota(jnp.int32, (ROUTE_ROWS, tm), 0).astype(F32)
    top = jnp.max(lgt, axis=0, keepdims=True)
    gi = jnp.min(jnp.where(lgt == top, sub, float(ROUTE_ROWS)), axis=0, keepdims=True)
    onehot = jnp.where(sub == gi, 1.0, 0.0)
    before = jnp.dot(onehot.astype(BF16), upper_ref[...], preferred_element_type=F32)
    base = base_ref[...]
    rank = jnp.sum(onehot * (before + base[:, :1]), axis=0, keepdims=True)
    base_ref[...] = base + jnp.sum(onehot, axis=1, keepdims=True)
    gidx_ref[...] = gi.astype(jnp.int32)
    rank_ref[...] = rank.astype(jnp.int32)


def _merge_kernel(x_ref, u_ref, vn_ref, sga_ref, sgb_ref, ob_ref, ws_ref, bs_ref, wb_ref, wout_ref,
                  *rest, tm, precise, route):
    if route:
        gffn_ref, wrgt_ref, brgt_ref, upper_ref, x1_ref, gidx_ref, rank_ref, oa_ref, base_ref = rest
    else:
        x1_ref, oa_ref = rest
    act = _act_dtype(precise)
    if route:
        @pl.when(pl.program_id(0) == 0)
        def _():
            base_ref[...] = jnp.zeros_like(base_ref)

    b_b = _mm(ob_ref[...], wb_ref[1], precise)
    for c in range(tm // GMLP_CHUNK):
        rows = slice(c * GMLP_CHUNK, (c + 1) * GMLP_CHUNK)
        for g in range(GMLP_GROUPS):
            cols = slice(g * GMLP_GROUP_DIM, (g + 1) * GMLP_GROUP_DIM)
            mix = _mm(ws_ref[g], vn_ref[rows, cols], precise) + bs_ref[:, cols]
            oa_ref[rows, cols] = (u_ref[rows, cols].astype(F32) * mix).astype(act)
    b_a = _mm(oa_ref[...], wb_ref[0], precise)
    m = sga_ref[...].astype(F32) * b_a + sgb_ref[...].astype(F32) * b_b
    x1 = x_ref[...] + _mm(m, wout_ref[...], precise)
    x1_ref[...] = x1
    if route:
        _route_rows(x1, gffn_ref, wrgt_ref, brgt_ref, upper_ref, gidx_ref, rank_ref, base_ref)


def _merge(x, u, vn, sga, sgb, ob, ws, bs_full, wb, wout, route_params=None, *, tm, precise):
    t = x.shape[0]
    n_tiles = t // tm
    route = route_params is not None
    row = pl.BlockSpec((tm, D_MODEL), lambda i: (i, 0))
    const = lambda i: (0, 0)
    in_specs = [row, row, row, row, row, row,
                pl.BlockSpec((GMLP_GROUPS, GMLP_CHUNK, GMLP_CHUNK), lambda i: (0, 0, 0)),
                pl.BlockSpec((GMLP_CHUNK, D_MODEL), const),
                pl.BlockSpec((2, D_MODEL, D_MODEL), lambda i: (0, 0, 0)),
                pl.BlockSpec((D_MODEL, D_MODEL), const)]
    out_specs = [row]
    out_shape = [jax.ShapeDtypeStruct((t, D_MODEL), F32)]
    scratch = [pltpu.VMEM((tm, D_MODEL), _act_dtype(precise))]
    args = [x, u, vn, sga, sgb, ob, ws, bs_full, wb, wout]
    if route:
        in_specs += [pl.BlockSpec((1, D_MODEL), const), pl.BlockSpec((ROUTE_ROWS, D_MODEL), const),
                     pl.BlockSpec((ROUTE_ROWS, 1), const), pl.BlockSpec((tm, tm), const)]
        tok_row = pl.BlockSpec((None, 1, tm), lambda i: (i, 0, 0))
        out_specs += [tok_row, tok_row]
        out_shape += [jax.ShapeDtypeStruct((n_tiles, 1, tm), jnp.int32)] * 2
        scratch += [pltpu.VMEM((ROUTE_ROWS, LANES), F32)]
        args += list(route_params)
    return pl.pallas_call(
        functools.partial(_merge_kernel, tm=tm, precise=precise, route=route),
        grid=(n_tiles,),
        in_specs=in_specs,
        out_specs=out_specs,
        out_shape=out_shape,
        scratch_shapes=scratch,
        compiler_params=pltpu.CompilerParams(
            dimension_semantics=("arbitrary",), vmem_limit_bytes=VMEM_LIMIT),
        name="merge",
    )(*args)


def _first_index_of_max(vals, lane):
    top = jnp.max(vals, axis=-1, keepdims=True)
    idx = jnp.min(jnp.where(vals == top, lane, float(LANES)), axis=-1, keepdims=True)
    return top, idx


def _top2_weights(le_in_group, lane, pg_top):
    t1, i1 = _first_index_of_max(le_in_group, lane)
    t2, i2 = _first_index_of_max(jnp.where(lane == i1, -jnp.inf, le_in_group), lane)
    e2 = jnp.exp(t2 - t1)
    den = 1.0 + e2
    return jnp.where(lane == i1, pg_top * (1.0 / den), jnp.where(lane == i2, pg_top * (e2 / den), 0.0))


def _group_experts(h, comb, lane_i, g, wg_ref, wu_ref, wd_ref):
    hids = []
    for j in range(EXPERTS_PER_GROUP):
        w_tok = jnp.sum(jnp.where(lane_i == g * EXPERTS_PER_GROUP + j, comb, 0.0), axis=-1, keepdims=True)
        a = jnp.dot(h, wg_ref[j], preferred_element_type=F32)
        b = jnp.dot(h, wu_ref[j], preferred_element_type=F32)
        hids.append((jax.nn.silu(a) * b * w_tok).astype(BF16))
    hid = jnp.concatenate(hids, axis=1)
    return jnp.dot(hid, wd_ref[...].reshape(EXPERTS_PER_GROUP * D_EXPERT, D_MODEL), preferred_element_type=F32)


def _embedding_gate(x2, p_ref, gple_ref, wpg_ref, wpe_ref):
    pe = jnp.dot(p_ref[...].astype(BF16), wpe_ref[...], preferred_element_type=F32)
    gate = jax.nn.sigmoid(jnp.dot(_rms(x2, gple_ref[...]).astype(BF16), wpg_ref[...], preferred_element_type=F32))
    return x2 + gate * pe


def _gather_rows(src_hbm, index_ref, first, buf, slot, sem):
    if isinstance(slot, int):
        for r in range(buf.shape[1]):
            pltpu.make_async_copy(src_hbm.at[index_ref[first + r]], buf.at[slot, r], sem.at[slot]).start(
                priority=r % 2)
        return

    def issue(r, carry):
        pltpu.make_async_copy(src_hbm.at[index_ref[first + r]], buf.at[slot, r], sem.at[slot]).start()
        return carry
    lax.fori_loop(0, buf.shape[1], issue, 0, unroll=DMA_UNROLL)


def _gather_next_tile(src_hbm, index_ref, i, n_tiles, buf, sem):
    for parity in range(2):
        @pl.when((i + 1 < n_tiles) & (i % 2 == parity))
        def _():
            _gather_rows(src_hbm, index_ref, (i + 1) * buf.shape[1], buf, 1 - parity, sem)


def _wait_rows(src_hbm, buf, slot, sem):
    pltpu.make_async_copy(src_hbm.at[pl.ds(0, buf.shape[1])], buf.at[slot], sem.at[slot]).wait()


def _experts_kernel(src_row_ref, tile_group_ref, n_valid_ref, x1_hbm, gffn_ref, wrg_ref, brg_ref, wre_ref, bre_ref,
                    wg_ref, wu_ref, wd_ref, ys_ref, xbuf, sems):
    i = pl.program_id(0)
    n_valid = n_valid_ref[0]

    @pl.when(i == 0)
    def _():
        _gather_rows(x1_hbm, src_row_ref, 0, xbuf, 0, sems)

    _gather_next_tile(x1_hbm, src_row_ref, i, n_valid, xbuf, sems)

    @pl.when(i < n_valid)
    def _():
        slot = i % 2
        _wait_rows(x1_hbm, xbuf, slot, sems)
        g = tile_group_ref[i]
        lane_i = lax.broadcasted_iota(jnp.int32, (1, LANES), 1)
        lane = lane_i.astype(F32)
        h = _rms(xbuf[slot], gffn_ref[...]).astype(BF16)
        lg = _mm(h, wrg_ref[...], False) + brg_ref[...]
        eg = jnp.exp(lg - jnp.max(lg, axis=-1, keepdims=True))
        pg = jnp.sum(jnp.where(lane_i == g, eg, 0.0), axis=-1, keepdims=True) / jnp.sum(eg, axis=-1, keepdims=True)
        le = _mm(h, wre_ref[...], False) + bre_ref[...]
        comb = _top2_weights(jnp.where(lane_i // EXPERTS_PER_GROUP == g, le, -jnp.inf), lane, pg)
        ys_ref[...] = _group_experts(h, comb, lane_i, g, wg_ref, wu_ref, wd_ref)

    @pl.when(i >= n_valid)
    def _():
        ys_ref[...] = jnp.zeros_like(ys_ref)


def _experts(src_row, tile_group, n_valid, x1, gffn, wrg, brg, wre, bre, wg, wu, wd, *, n_tiles):
    const = lambda i, sr, tg, nv: (0, 0)
    group = lambda i, sr, tg, nv: (tg[i], 0, 0)
    return pl.pallas_call(
        _experts_kernel,
        grid_spec=pltpu.PrefetchScalarGridSpec(
            num_scalar_prefetch=3,
            grid=(n_tiles,),
            in_specs=[
                pl.BlockSpec(memory_space=pl.ANY),
                pl.BlockSpec((1, D_MODEL), const),
                pl.BlockSpec((D_MODEL, LANES), const),
                pl.BlockSpec((1, LANES), const),
                pl.BlockSpec((D_MODEL, LANES), const),
                pl.BlockSpec((1, LANES), const),
                pl.BlockSpec((EXPERTS_PER_GROUP, D_MODEL, D_EXPERT), group),
                pl.BlockSpec((EXPERTS_PER_GROUP, D_MODEL, D_EXPERT), group),
                pl.BlockSpec((EXPERTS_PER_GROUP, D_EXPERT, D_MODEL), group),
            ],
            out_specs=pl.BlockSpec((ROUTE_TILE, D_MODEL), lambda i, sr, tg, nv: (i, 0)),
            scratch_shapes=[pltpu.VMEM((2, ROUTE_TILE, D_MODEL), F32), pltpu.SemaphoreType.DMA((2,))]),
        out_shape=jax.ShapeDtypeStruct((n_tiles * ROUTE_TILE, D_MODEL), F32),
        compiler_params=pltpu.CompilerParams(
            dimension_semantics=("arbitrary",), vmem_limit_bytes=VMEM_LIMIT),
        name="experts",
    )(src_row, tile_group, n_valid, x1, gffn, wrg, brg, wre, bre, wg, wu, wd)


def _combine_kernel(pos_ref, x1_ref, p_ref, gple_ref, wpg_ref, wpe_ref, ys_hbm, out_ref, ybuf, sems, *, tm):
    i = pl.program_id(0)

    @pl.when(i == 0)
    def _():
        _gather_rows(ys_hbm, pos_ref, 0, ybuf, 0, sems)

    _gather_next_tile(ys_hbm, pos_ref, i, pl.num_programs(0), ybuf, sems)

    slot = i % 2
    _wait_rows(ys_hbm, ybuf, slot, sems)
    out_ref[...] = _embedding_gate(x1_ref[...] + ybuf[slot], p_ref, gple_ref, wpg_ref, wpe_ref)


def _combine(pos, x1, p, gple, wpg, wpe, ys, *, tm):
    t = x1.shape[0]
    row = lambda i, pos: (i, 0)
    const = lambda i, pos: (0, 0)
    return pl.pallas_call(
        functools.partial(_combine_kernel, tm=tm),
        grid_spec=pltpu.PrefetchScalarGridSpec(
            num_scalar_prefetch=1,
            grid=(t // tm,),
            in_specs=[pl.BlockSpec((tm, D_MODEL), row),
                      pl.BlockSpec((tm, PLE_DIM), row),
                      pl.BlockSpec((1, D_MODEL), const),
                      pl.BlockSpec((D_MODEL, D_MODEL), const),
                      pl.BlockSpec((PLE_DIM, D_MODEL), const),
                      pl.BlockSpec(memory_space=pl.ANY)],
            out_specs=pl.BlockSpec((tm, D_MODEL), row),
            scratch_shapes=[pltpu.VMEM((2, tm, D_MODEL), F32), pltpu.SemaphoreType.DMA((2,))]),
        out_shape=jax.ShapeDtypeStruct((t, D_MODEL), F32),
        compiler_params=pltpu.CompilerParams(
            dimension_semantics=("arbitrary",), vmem_limit_bytes=VMEM_LIMIT),
        name="combine",
    )(pos, x1, p, gple, wpg, wpe, ys)


def _route_plan(gidx, rank, *, n_tiles):
    t = gidx.shape[0]
    groups = jnp.arange(N_GROUPS, dtype=jnp.int32)
    member = gidx[None, :] == groups[:, None]
    counts = jnp.sum(member, axis=1, dtype=jnp.int32)
    padded = (counts + ROUTE_TILE - 1) // ROUTE_TILE * ROUTE_TILE
    ends = jnp.cumsum(padded)
    offsets = ends - padded
    pos = rank + jnp.sum(jnp.where(member, offsets[:, None], 0), axis=0, dtype=jnp.int32)
    src_row = jnp.zeros((n_tiles * ROUTE_TILE,), jnp.int32).at[pos].set(
        jnp.arange(t, dtype=jnp.int32), unique_indices=True, indices_are_sorted=False)
    n_valid = jnp.maximum(ends[-1] // ROUTE_TILE, 1)
    tile_start = jnp.minimum(jnp.arange(n_tiles, dtype=jnp.int32), n_valid - 1) * ROUTE_TILE
    tile_group = jnp.minimum(jnp.sum(tile_start[:, None] >= ends[None, :], axis=1, dtype=jnp.int32), N_GROUPS - 1)
    return pos, src_row, tile_group, n_valid.reshape(1).astype(jnp.int32)


def _channel_kernel(x1_ref, p_ref, gffn_ref, wrg_ref, brg_ref, wre_ref, bre_ref,
                    wg_ref, wu_ref, wd_ref, gple_ref, wpg_ref, wpe_ref,
                    out_ref, h_ref, comb_ref, acc_ref, *, precise_router):
    g = pl.program_id(1)
    lane_i = lax.broadcasted_iota(jnp.int32, (1, LANES), 1)
    lane = lane_i.astype(F32)

    @pl.when(g == 0)
    def _():
        h = _rms(x1_ref[...], gffn_ref[...])
        h_ref[...] = h.astype(BF16)
        lg = _mm(h, wrg_ref[...], precise_router) + brg_ref[...]
        top_g, g_idx = _first_index_of_max(lg, lane)
        pg_top = 1.0 / jnp.sum(jnp.exp(lg - top_g), axis=-1, keepdims=True)
        le = _mm(h, wre_ref[...], precise_router) + bre_ref[...]
        group_of_lane = (lane_i // EXPERTS_PER_GROUP).astype(F32)
        comb_ref[...] = _top2_weights(jnp.where(group_of_lane == g_idx, le, -jnp.inf), lane, pg_top)

    y = _group_experts(h_ref[...], comb_ref[...], lane_i, g, wg_ref, wu_ref, wd_ref)

    @pl.when(g == 0)
    def _():
        acc_ref[...] = y

    @pl.when(g > 0)
    def _():
        acc_ref[...] += y

    @pl.when(g == N_GROUPS - 1)
    def _():
        out_ref[...] = _embedding_gate(x1_ref[...] + acc_ref[...], p_ref, gple_ref, wpg_ref, wpe_ref)


def _channel(x1, p, gffn, wrg, brg, wre, bre, wg, wu, wd, gple, wpg, wpe, *, tm, precise_router):
    t = x1.shape[0]
    row = lambda i, g: (i, 0)
    const2 = lambda i, g: (0, 0)
    group = lambda i, g: (g, 0, 0)
    return pl.pallas_call(
        functools.partial(_channel_kernel, precise_router=precise_router),
        grid=(t // tm, N_GROUPS),
        in_specs=[
            pl.BlockSpec((tm, D_MODEL), row),
            pl.BlockSpec((tm, PLE_DIM), row),
            pl.BlockSpec((1, D_MODEL), const2),
            pl.BlockSpec((D_MODEL, LANES), const2),
            pl.BlockSpec((1, LANES), const2),
            pl.BlockSpec((D_MODEL, LANES), const2),
            pl.BlockSpec((1, LANES), const2),
            pl.BlockSpec((EXPERTS_PER_GROUP, D_MODEL, D_EXPERT), group),
            pl.BlockSpec((EXPERTS_PER_GROUP, D_MODEL, D_EXPERT), group),
            pl.BlockSpec((EXPERTS_PER_GROUP, D_EXPERT, D_MODEL), group),
            pl.BlockSpec((1, D_MODEL), const2),
            pl.BlockSpec((D_MODEL, D_MODEL), const2),
            pl.BlockSpec((PLE_DIM, D_MODEL), const2),
        ],
        out_specs=pl.BlockSpec((tm, D_MODEL), row),
        out_shape=jax.ShapeDtypeStruct((t, D_MODEL), F32),
        scratch_shapes=[pltpu.VMEM((tm, D_MODEL), BF16),
                        pltpu.VMEM((tm, LANES), F32),
                        pltpu.VMEM((tm, D_MODEL), F32)],
        compiler_params=pltpu.CompilerParams(
            dimension_semantics=("arbitrary", "arbitrary"), vmem_limit_bytes=VMEM_LIMIT),
        name="channel",
    )(x1, p, gffn, wrg, brg, wre, bre, wg, wu, wd, gple, wpg, wpe)


def _pad_lanes(a, fill):
    return jnp.pad(a, ((0, 0), (0, LANES - a.shape[1])), constant_values=fill)


def kernel(x_prompt, x_sample, cache_attn_k, cache_attn_v, p_prompt, p_sample, g_mix, w_in, g_gmlp_v, w_gmlp_s, b_gmlp_s, g_q, g_k, rel_bias, w_branch, w_out, g_ffn, w_router_group, b_router_group, w_router_expert, b_router_expert, w_exp_gate, w_exp_up, w_exp_down, g_ple, w_ple_gate, w_ple_proj):
    depth = g_mix.shape[0]
    batch, seq, _ = x_prompt.shape
    dec_batch, dec_seq, _ = x_sample.shape
    lc = cache_attn_k.shape[2]
    keep = min(N_PAST_CHUNKS * CHUNK, seq)
    tm = 512
    assert depth == 1 and keep == tm and seq % tm == 0 and seq % Q_BLOCK == 0
    assert dec_seq <= GMLP_CHUNK and GMLP_CHUNK % dec_seq == 0 and dec_batch * dec_seq == GMLP_CHUNK
    l = 0
    t_s = dec_batch * dec_seq

    row = lambda a: a.reshape(1, -1).astype(F32)
    w_in_b = w_in[l].astype(BF16)
    gq_t = row(jnp.tile(g_q[l], ATTN_HEADS))
    gk_t = row(jnp.tile(g_k[l], ATTN_HEADS))
    head_of_lane = jnp.arange(D_MODEL, dtype=jnp.int32) // HEAD_DIM
    seg = (head_of_lane[:, None] == jnp.arange(LANES, dtype=jnp.int32)[None, :]).astype(BF16)
    segt2 = jnp.concatenate([seg.T, seg.T], axis=0)
    ws_tril = jnp.tril(w_gmlp_s[l])
    ws_p = ws_tril.astype(BF16)
    bs_p = jnp.repeat(b_gmlp_s[l].T, GMLP_GROUP_DIM, axis=1).astype(F32)
    eye = jnp.eye(dec_batch, dtype=F32)
    ws_s = jnp.einsum('ab,gts->gatbs', eye, ws_tril[:, :dec_seq, :dec_seq]).reshape(
        GMLP_GROUPS, t_s, t_s)
    bs_s = jnp.tile(jnp.repeat(b_gmlp_s[l][:, :dec_seq].T, GMLP_GROUP_DIM, axis=1), (dec_batch, 1)).astype(F32)
    wb_b = w_branch[l].astype(BF16)
    wout_b = w_out[l].astype(BF16)
    wrg = _pad_lanes(w_router_group[l], 0.0)
    brg = _pad_lanes(row(b_router_group[l]), NEG)
    wre = _pad_lanes(w_router_expert[l], 0.0)
    bre = _pad_lanes(row(b_router_expert[l]), 0.0)
    wg_b = w_exp_gate[l].astype(BF16)
    wu_b = w_exp_up[l].astype(BF16)
    wd_b = w_exp_down[l].astype(BF16)
    wpg_b = w_ple_gate[l].astype(BF16)
    wpe_b = w_ple_proj[l].astype(BF16)
    bias_prompt = _prompt_bias_table(rel_bias[l] * LOG2_E)
    bias_c, bias_n = _sample_bias_tables(rel_bias[l], dec_seq, lc, PAST_LEN)

    wrgt = jnp.pad(w_router_group[l].T, ((0, ROUTE_ROWS - N_GROUPS), (0, 0))).astype(BF16)
    brgt = jnp.pad(b_router_group[l].astype(F32), (0, ROUTE_ROWS - N_GROUPS), constant_values=NEG).reshape(ROUTE_ROWS, 1)
    upper = jnp.asarray(np.triu(np.ones((tm, tm), np.float32), 1), BF16)

    xp = x_prompt.reshape(batch * seq, D_MODEL)
    u, vn, q, k, v, sga, sgb, kst, vst = _inproj(
        xp, row(g_mix[l]), w_in_b, row(g_gmlp_v[l]), gq_t, gk_t, seg, segt2,
        tm=tm, tiles_per_batch=seq // tm, emit_vn_state=False, precise=False, q_scale=SCORE_SCALE * LOG2_E)
    ob = _attn_prompt(q, k, v, bias_prompt, batch=batch, seq=seq)
    x1, gidx, rank = _merge(xp, u, vn, sga, sgb, ob, ws_p, bs_p, wb_b, wout_b,
                            (row(g_ffn[l]), wrgt, brgt, upper), tm=tm, precise=False)
    n_route_tiles = batch * seq // ROUTE_TILE + N_GROUPS
    pos, src_row, tile_group, n_valid = _route_plan(gidx.reshape(-1), rank.reshape(-1), n_tiles=n_route_tiles)
    y_sorted = _experts(src_row, tile_group, n_valid, x1, row(g_ffn[l]), wrg.astype(BF16), brg, wre.astype(BF16), bre,
                        wg_b, wu_b, wd_b, n_tiles=n_route_tiles)
    yp = _combine(pos, x1, p_prompt[l].reshape(batch * seq, PLE_DIM), row(g_ple[l]), wpg_b, wpe_b, y_sorted, tm=tm)

    xs = x_sample.reshape(t_s, D_MODEL)
    u, vn, q, k, v, sga, sgb, kss, vss, gvs = _inproj(
        xs, row(g_mix[l]), w_in[l], row(g_gmlp_v[l]), gq_t, gk_t, seg, segt2,
        tm=t_s, tiles_per_batch=1, emit_vn_state=True, precise=True, q_scale=SCORE_SCALE)
    ob = _attn_sample(q, k, v, cache_attn_k[l].reshape(dec_batch, lc, D_MODEL),
                      cache_attn_v[l].reshape(dec_batch, lc, D_MODEL), bias_c, bias_n,
                      batch=dec_batch, rows=dec_seq)
    x1, = _merge(xs, u, vn, sga, sgb, ob, ws_s, bs_s, w_branch[l], w_out[l], tm=t_s, precise=True)
    ys = _channel(x1, p_sample[l].reshape(t_s, PLE_DIM), row(g_ffn[l]), wrg, brg, wre, bre, wg_b, wu_b, wd_b,
                  row(g_ple[l]), wpg_b, wpe_b, tm=t_s, precise_router=True)

    return (yp.reshape(batch, seq, D_MODEL),
            ys.reshape(dec_batch, dec_seq, D_MODEL),
            kst.reshape(1, batch, keep, ATTN_HEADS, HEAD_DIM),
            vst.reshape(1, batch, keep, ATTN_HEADS, HEAD_DIM),
            kss.reshape(1, dec_batch, dec_seq, ATTN_HEADS, HEAD_DIM),
            vss.reshape(1, dec_batch, dec_seq, ATTN_HEADS, HEAD_DIM),
            gvs.reshape(1, dec_batch, dec_seq, D_MODEL))
```

```python
import functools

import jax
import jax.numpy as jnp
import numpy as np
from jax import lax
from jax.experimental import pallas as pl
from jax.experimental.pallas import tpu as pltpu

F32 = jnp.float32
BF16 = jnp.bfloat16

D_MODEL = 1024
CHUNK = 64
N_PAST_CHUNKS = 8
HEAD_DIM = 64
ATTN_HEADS = D_MODEL // HEAD_DIM
MAX_REL = 128
GMLP_CHUNK = 128
GMLP_GROUPS = 4
GMLP_GROUP_DIM = D_MODEL // GMLP_GROUPS
N_SECTIONS = 7
N_GROUPS = 4
EXPERTS_PER_GROUP = 8
N_EXPERTS = N_GROUPS * EXPERTS_PER_GROUP
D_EXPERT = D_MODEL // 4
PLE_DIM = 256
EPS = 1e-6
NEG = -1e30
PAST_LEN = 1024
SCORE_SCALE = HEAD_DIM ** -0.5
LOG2_E = 1.4426950408889634

LANES = 128
Q_BLOCK = 2 * CHUNK
K_WINDOW = Q_BLOCK + N_PAST_CHUNKS * CHUNK
N_SHIFT = N_PAST_CHUNKS * CHUNK // Q_BLOCK + 1
BIAS_SPAN = K_WINDOW + (N_SHIFT - 1) * Q_BLOCK
BIAS_PERIOD = Q_BLOCK + BIAS_SPAN
ROUTE_ROWS = 16
ROUTE_TILE = 512
DMA_UNROLL = 8
ATTN_UNROLL = 6
VMEM_LIMIT = 56 * 1024 * 1024


def _rms(x, gain):
    return x * lax.rsqrt(jnp.mean(x * x, axis=-1, keepdims=True) + EPS) * gain


def _act_dtype(precise):
    return F32 if precise else BF16


def _mm(a, b, precise, dims=None):
    dims = dims or (((a.ndim - 1,), (0,)), ((), ()))
    if precise:
        return lax.dot_general(a.astype(F32), b.astype(F32), dims, precision=lax.Precision.HIGHEST,
                               preferred_element_type=F32)
    return lax.dot_general(a.astype(BF16), b.astype(BF16), dims, preferred_element_type=F32)


_NT = (((1,), (1,)), ((), ()))


def _head_sumsq(z, seg, precise):
    return _mm(z * z, seg, precise)


def _head_inv_rms(ssum, segt2, precise):
    inv = lax.rsqrt(ssum * (1.0 / HEAD_DIM) + EPS)
    if precise:
        return _mm(inv, segt2[:LANES], True)
    hi = inv.astype(BF16)
    lo = (inv - hi.astype(F32)).astype(BF16)
    return _mm(jnp.concatenate([hi, lo], axis=1), segt2, False)


def _inproj_kernel(x_ref, gmix_ref, w_ref, gv_ref, gq_ref, gk_ref, seg_ref, segt2_ref,
                   u_ref, vn_ref, q_ref, k_ref, v_ref, sga_ref, sgb_ref,
                   kst_ref, vst_ref, *maybe_vnst_ref, precise, q_scale):
    act = _act_dtype(precise)
    h = _rms(x_ref[...], gmix_ref[...]).astype(act)

    def section(s):
        return _mm(h, w_ref[:, s * D_MODEL:(s + 1) * D_MODEL], precise)

    z_u = section(0)
    z_v = section(1)
    u_ref[...] = jax.nn.gelu(z_u).astype(act)
    z_q = section(2)
    vn = _rms(jax.nn.gelu(z_v), gv_ref[...])
    vn_ref[...] = vn.astype(act)
    if maybe_vnst_ref:
        maybe_vnst_ref[0][...] = vn
    z_k = section(3)
    ss_q = _head_sumsq(z_q, seg_ref[...], precise)
    v = section(4)
    ss_k = _head_sumsq(z_k, seg_ref[...], precise)
    z_ga = section(5)
    inv_q = _head_inv_rms(ss_q, segt2_ref[...], precise)
    z_gb = section(6)
    inv_k = _head_inv_rms(ss_k, segt2_ref[...], precise)

    v_ref[...] = v.astype(act)
    vst_ref[...] = v
    sga_ref[...] = jax.nn.sigmoid(z_ga).astype(act)
    q_ref[...] = (z_q * inv_q * gq_ref[...] * q_scale).astype(act)
    sgb_ref[...] = jax.nn.sigmoid(z_gb).astype(act)
    kn = z_k * inv_k * gk_ref[...]
    k_ref[...] = kn.astype(act)
    kst_ref[...] = kn


def _inproj(x, gmix, w_in, gv, gq_t, gk_t, seg, segt2, *, tm, tiles_per_batch, emit_vn_state, precise, q_scale):
    t = x.shape[0]
    n_tiles = t // tm
    n_state = n_tiles // tiles_per_batch
    row = lambda i: (i, 0)
    const = lambda i: (0, 0)
    state = lambda i: (i // tiles_per_batch, 0)
    act = jax.ShapeDtypeStruct((t, D_MODEL), _act_dtype(precise))
    st = jax.ShapeDtypeStruct((n_state * tm, D_MODEL), F32)
    out_shape = [act] * 7 + [st, st] + ([st] if emit_vn_state else [])
    out_specs = [pl.BlockSpec((tm, D_MODEL), row)] * 7 + [pl.BlockSpec((tm, D_MODEL), state)] * (
        3 if emit_vn_state else 2)
    return pl.pallas_call(
        functools.partial(_inproj_kernel, precise=precise, q_scale=q_scale),
        grid=(n_tiles,),
        in_specs=[
            pl.BlockSpec((tm, D_MODEL), row),
            pl.BlockSpec((1, D_MODEL), const),
            pl.BlockSpec((D_MODEL, N_SECTIONS * D_MODEL), const, pipeline_mode=pl.Buffered(1)),
            pl.BlockSpec((1, D_MODEL), const),
            pl.BlockSpec((1, D_MODEL), const),
            pl.BlockSpec((1, D_MODEL), const),
            pl.BlockSpec((D_MODEL, LANES), const),
            pl.BlockSpec((2 * LANES, D_MODEL), const),
        ],
        out_specs=out_specs,
        out_shape=out_shape,
        compiler_params=pltpu.CompilerParams(
            dimension_semantics=("arbitrary",), vmem_limit_bytes=VMEM_LIMIT),
        name="inproj",
    )(x, gmix, w_in, gv, gq_t, gk_t, seg, segt2)


def _attn_prompt_kernel(q_ref, k_ref, v_ref, period_ref, o_ref, bias_ref, *, seq):
    lane = lax.broadcasted_iota(jnp.int32, (1, LANES), 1)
    low_head = lane < HEAD_DIM
    zero = jnp.zeros((), BF16)

    @pl.when(pl.program_id(1) == 0)
    def _():
        row = lax.broadcasted_iota(jnp.int32, (Q_BLOCK, BIAS_SPAN), 0)
        col = lax.broadcasted_iota(jnp.int32, (Q_BLOCK, BIAS_SPAN), 1)
        chunk_gap = col // CHUNK - row // CHUNK
        in_band = (chunk_gap >= 0) & (chunk_gap <= N_PAST_CHUNKS)
        for e in range(2):
            period = jnp.broadcast_to(period_ref[e:e + 1, :], (Q_BLOCK, BIAS_PERIOD))
            toeplitz = pltpu.roll(period, 0, 1, stride=1, stride_axis=0)
            bias_ref[e * Q_BLOCK:(e + 1) * Q_BLOCK, :] = jnp.where(in_band, toeplitz[:, :BIAS_SPAN], NEG)

    def scores(qb, shift):
        q_start = pl.multiple_of(qb * Q_BLOCK, Q_BLOCK)
        k_start = pl.multiple_of((qb + shift) * Q_BLOCK - (K_WINDOW - Q_BLOCK), Q_BLOCK)
        q = q_ref[pl.ds(q_start, Q_BLOCK), :]
        k = k_ref[pl.ds(k_start, K_WINDOW), :]
        q2 = jnp.concatenate([jnp.where(low_head, q, zero), jnp.where(low_head, zero, q)], axis=0)
        bias = bias_ref[:, shift * Q_BLOCK:shift * Q_BLOCK + K_WINDOW]
        return lax.dot_general(q2, k, _NT, preferred_element_type=F32) + bias, q_start, k_start

    ones = jnp.ones((K_WINDOW, LANES), BF16)

    def attend(s, q_start, k_start):
        v = v_ref[pl.ds(k_start, K_WINDOW), :]
        m = jnp.max(s, axis=-1, keepdims=True)
        p = jnp.exp2((s - m).astype(BF16))
        ol = jnp.dot(p, jnp.concatenate([v, ones], axis=1), preferred_element_type=F32)
        o = ol[:, :LANES] * (1.0 / ol[:, LANES:])
        o_ref[pl.ds(q_start, Q_BLOCK), :] = jnp.where(low_head, o[:Q_BLOCK], o[Q_BLOCK:]).astype(BF16)

    def blocks(specs):
        for args in [scores(qb, shift) for qb, shift in specs]:
            attend(*args)

    blocks([(qb, N_SHIFT - 1 - qb) for qb in range(N_SHIFT - 1)])

    def steady(j, carry):
        first = N_SHIFT - 1 + j * ATTN_UNROLL
        blocks([(first + u, 0) for u in range(ATTN_UNROLL)])
        return carry

    lax.fori_loop(0, (seq // Q_BLOCK - (N_SHIFT - 1)) // ATTN_UNROLL, steady, 0)


def _attn_prompt(q, k, v, bias_tab, *, batch, seq):
    q3, k3, v3 = (a.reshape(batch, seq, D_MODEL) for a in (q, k, v))
    head_pair = lambda hp, b: (b, 0, hp)
    spec = pl.BlockSpec((None, seq, LANES), head_pair)
    out = pl.pallas_call(
        functools.partial(_attn_prompt_kernel, seq=seq),
        grid=(ATTN_HEADS // 2, batch),
        in_specs=[spec, spec, spec,
                  pl.BlockSpec((None, 2, BIAS_PERIOD), lambda hp, b: (hp, 0, 0))],
        out_specs=spec,
        out_shape=jax.ShapeDtypeStruct((batch, seq, D_MODEL), BF16),
        scratch_shapes=[pltpu.VMEM((2 * Q_BLOCK, BIAS_SPAN), F32)],
        compiler_params=pltpu.CompilerParams(
            dimension_semantics=("arbitrary", "arbitrary"), vmem_limit_bytes=VMEM_LIMIT),
        name="attn_prompt",
    )(q3, k3, v3, bias_tab)
    return out.reshape(batch * seq, D_MODEL)


def _rel_bias_period(rel_bias, q0, k0, rows, ncols):
    dmax = q0 - k0 + rows - 1
    total = rows + ncols - 1
    dmin = dmax - total + 1
    n_lo = min(max(-MAX_REL - dmin, 0), total)
    n_hi = min(max(dmax - MAX_REL, 0), total)
    mid0 = dmin + n_lo + MAX_REL
    asc = jnp.concatenate([jnp.repeat(rel_bias[:, :1], n_lo, axis=1),
                           rel_bias[:, mid0:mid0 + total - n_lo - n_hi],
                           jnp.repeat(rel_bias[:, -1:], n_hi, axis=1)], axis=1).astype(F32)
    desc = asc[:, ::-1]
    heads = desc.shape[0]
    return jnp.concatenate([desc[:, rows - 1:], jnp.zeros((heads, 1), F32), desc[:, :rows - 1]], axis=1)


def _rel_bias_rows(rel_bias, q0, k0, rows, ncols):
    period = _rel_bias_period(rel_bias, q0, k0, rows, ncols)
    total = rows + ncols - 1
    flat = jnp.tile(period, (1, rows))[:, :rows * total]
    return flat.reshape(period.shape[0], rows, total)[:, :, :ncols]


def _band_ok(q_pos, k_pos):
    q_pos, k_pos = q_pos[:, None], k_pos[None, :]
    return (k_pos >= 0) & (k_pos // CHUNK <= q_pos // CHUNK) & (k_pos // CHUNK >= q_pos // CHUNK - N_PAST_CHUNKS)


def _prompt_bias_period(rel_bias):
    period = _rel_bias_period(rel_bias, N_PAST_CHUNKS * CHUNK, 0, Q_BLOCK, BIAS_SPAN)
    return period.reshape(ATTN_HEADS // 2, 2, BIAS_PERIOD)


def _attn_sample_kernel(q_ref, k_ref, v_ref, ck_ref, cv_ref, bc_ref, bn_ref, o_ref):
    rows = q_ref.shape[0]
    lane = lax.broadcasted_iota(jnp.int32, (1, LANES), 1)
    low_head = lane < HEAD_DIM
    scored = []
    for hp in range(ATTN_HEADS // 2):
        cols = slice(hp * LANES, (hp + 1) * LANES)
        q = q_ref[:, cols]
        q2 = jnp.concatenate([jnp.where(low_head, q, 0.0), jnp.where(low_head, 0.0, q)], axis=0)
        s_c = _mm(q2, ck_ref[:, cols], True, _NT) + bc_ref[hp]
        s_n = _mm(q2, k_ref[:, cols], True, _NT) + bn_ref[hp]
        scored.append((cols, s_c, s_n))
    for cols, s_c, s_n in scored:
        m = jnp.maximum(jnp.max(s_c, axis=-1, keepdims=True), jnp.max(s_n, axis=-1, keepdims=True))
        p_c = jnp.exp(s_c - m)
        p_n = jnp.exp(s_n - m)
        l = jnp.sum(p_c, axis=-1, keepdims=True) + jnp.sum(p_n, axis=-1, keepdims=True)
        o = (_mm(p_c, cv_ref[:, cols], True) + _mm(p_n, v_ref[:, cols], True)) * (1.0 / l)
        o_ref[:, cols] = jnp.where(low_head, o[:rows], o[rows:])


def _attn_sample(q, k, v, cache_k, cache_v, bias_c, bias_n, *, batch, rows):
    lc = cache_k.shape[1]
    new = pl.BlockSpec((rows, D_MODEL), lambda b: (b, 0))
    cache = pl.BlockSpec((None, lc, D_MODEL), lambda b: (b, 0, 0))
    return pl.pallas_call(
        _attn_sample_kernel,
        grid=(batch,),
        in_specs=[new, new, new, cache, cache,
                  pl.BlockSpec((ATTN_HEADS // 2, 2 * rows, lc), lambda b: (0, 0, 0)),
                  pl.BlockSpec((ATTN_HEADS // 2, 2 * rows, rows), lambda b: (0, 0, 0))],
        out_specs=new,
        out_shape=jax.ShapeDtypeStruct((batch * rows, D_MODEL), F32),
        compiler_params=pltpu.CompilerParams(
            dimension_semantics=("arbitrary",), vmem_limit_bytes=VMEM_LIMIT),
        name="attn_sample",
    )(q, k, v, cache_k, cache_v, bias_c, bias_n)


def _sample_bias_tables(rel_bias, rows, lc, past_len):
    q_pos = past_len + np.arange(rows)
    c_pos = past_len - lc + np.arange(lc)
    bias_c = jnp.where(_band_ok(q_pos, c_pos)[None], _rel_bias_rows(rel_bias, past_len, past_len - lc, rows, lc), NEG)
    bias_n = jnp.where(_band_ok(q_pos, q_pos)[None], _rel_bias_rows(rel_bias, past_len, past_len, rows, rows), NEG)
    pair = lambda tab: tab.reshape(ATTN_HEADS // 2, 2 * rows, tab.shape[-1])
    return pair(bias_c), pair(bias_n)


def _route_rows(x1, gffn_ref, wrgt_ref, brgt_ref, upper_ref, gidx_ref, rank_ref, base_ref):
    tm = x1.shape[0]
    h = _rms(x1, gffn_ref[...]).astype(BF16)
    lgt = lax.dot_general(wrgt_ref[...], h, _NT, preferred_element_type=F32) + brgt_ref[...]
    sub = lax.broadcasted_iota(jnp.int32, (ROUTE_ROWS, tm), 0).astype(F32)
    top = jnp.max(lgt, axis=0, keepdims=True)
    gi = jnp.min(jnp.where(lgt == top, sub, float(ROUTE_ROWS)), axis=0, keepdims=True)
    onehot = jnp.where(sub == gi, 1.0, 0.0)
    before = jnp.dot(onehot.astype(BF16), upper_ref[...], preferred_element_type=F32)
    base = base_ref[...]
    rank = jnp.sum(onehot * (before + base[:, :1]), axis=0, keepdims=True)
    base_ref[...] = base + jnp.sum(onehot, axis=1, keepdims=True)
    gidx_ref[...] = gi.astype(jnp.int32)
    rank_ref[...] = rank.astype(jnp.int32)


def _merge_kernel(x_ref, u_ref, vn_ref, sga_ref, sgb_ref, ob_ref, ws_ref, bs_ref, wb_ref, wout_ref,
                  *rest, tm, precise, route):
    if route:
        gffn_ref, wrgt_ref, brgt_ref, upper_ref, x1_ref, gidx_ref, rank_ref, oa_ref, base_ref = rest
    else:
        x1_ref, oa_ref = rest
    act = _act_dtype(precise)
    if route:
        @pl.when(pl.program_id(0) == 0)
        def _():
            base_ref[...] = jnp.zeros_like(base_ref)

    b_b = _mm(ob_ref[...], wb_ref[1], precise)
    for c in range(tm // GMLP_CHUNK):
        rows = slice(c * GMLP_CHUNK, (c + 1) * GMLP_CHUNK)
        for g in range(GMLP_GROUPS):
            cols = slice(g * GMLP_GROUP_DIM, (g + 1) * GMLP_GROUP_DIM)
            mix = _mm(ws_ref[g], vn_ref[rows, cols], precise) + bs_ref[:, cols]
            oa_ref[rows, cols] = (u_ref[rows, cols].astype(F32) * mix).astype(act)
    b_a = _mm(oa_ref[...], wb_ref[0], precise)
    m = sga_ref[...].astype(F32) * b_a + sgb_ref[...].astype(F32) * b_b
    x1 = x_ref[...] + _mm(m, wout_ref[...], precise)
    x1_ref[...] = x1
    if route:
        _route_rows(x1, gffn_ref, wrgt_ref, brgt_ref, upper_ref, gidx_ref, rank_ref, base_ref)


def _merge(x, u, vn, sga, sgb, ob, ws, bs_full, wb, wout, route_params=None, *, tm, precise):
    t = x.shape[0]
    n_tiles = t // tm
    route = route_params is not None
    row = pl.BlockSpec((tm, D_MODEL), lambda i: (i, 0))
    const = lambda i: (0, 0)
    in_specs = [row, row, row, row, row, row,
                pl.BlockSpec((GMLP_GROUPS, GMLP_CHUNK, GMLP_CHUNK), lambda i: (0, 0, 0)),
                pl.BlockSpec((GMLP_CHUNK, D_MODEL), const),
                pl.BlockSpec((2, D_MODEL, D_MODEL), lambda i: (0, 0, 0)),
                pl.BlockSpec((D_MODEL, D_MODEL), const)]
    out_specs = [row]
    out_shape = [jax.ShapeDtypeStruct((t, D_MODEL), F32)]
    scratch = [pltpu.VMEM((tm, D_MODEL), _act_dtype(precise))]
    args = [x, u, vn, sga, sgb, ob, ws, bs_full, wb, wout]
    if route:
        in_specs += [pl.BlockSpec((1, D_MODEL), const), pl.BlockSpec((ROUTE_ROWS, D_MODEL), const),
                     pl.BlockSpec((ROUTE_ROWS, 1), const), pl.BlockSpec((tm, tm), const)]
        tok_row = pl.BlockSpec((None, 1, tm), lambda i: (i, 0, 0))
        out_specs += [tok_row, tok_row]
        out_shape += [jax.ShapeDtypeStruct((n_tiles, 1, tm), jnp.int32)] * 2
        scratch += [pltpu.VMEM((ROUTE_ROWS, LANES), F32)]
        args += list(route_params)
    return pl.pallas_call(
        functools.partial(_merge_kernel, tm=tm, precise=precise, route=route),
        grid=(n_tiles,),
        in_specs=in_specs,
        out_specs=out_specs,
        out_shape=out_shape,
        scratch_shapes=scratch,
        compiler_params=pltpu.CompilerParams(
            dimension_semantics=("arbitrary",), vmem_limit_bytes=VMEM_LIMIT),
        name="merge",
    )(*args)


def _first_index_of_max(vals, lane):
    top = jnp.max(vals, axis=-1, keepdims=True)
    idx = jnp.min(jnp.where(vals == top, lane, float(LANES)), axis=-1, keepdims=True)
    return top, idx


def _top2_weights(le_in_group, lane, pg_top):
    t1, i1 = _first_index_of_max(le_in_group, lane)
    t2, i2 = _first_index_of_max(jnp.where(lane == i1, -jnp.inf, le_in_group), lane)
    e2 = jnp.exp(t2 - t1)
    den = 1.0 + e2
    return jnp.where(lane == i1, pg_top * (1.0 / den), jnp.where(lane == i2, pg_top * (e2 / den), 0.0))


def _group_experts(h, comb, lane_i, g, wg_ref, wu_ref, wd_ref):
    hids = []
    for j in range(EXPERTS_PER_GROUP):
        w_tok = jnp.sum(jnp.where(lane_i == g * EXPERTS_PER_GROUP + j, comb, 0.0), axis=-1, keepdims=True)
        a = jnp.dot(h, wg_ref[j], preferred_element_type=F32)
        b = jnp.dot(h, wu_ref[j], preferred_element_type=F32)
        hids.append((jax.nn.silu(a) * b * w_tok).astype(BF16))
    hid = jnp.concatenate(hids, axis=1)
    return jnp.dot(hid, wd_ref[...].reshape(EXPERTS_PER_GROUP * D_EXPERT, D_MODEL), preferred_element_type=F32)


def _embedding_gate(x2, p_ref, gple_ref, wpg_ref, wpe_ref):
    pe = jnp.dot(p_ref[...].astype(BF16), wpe_ref[...], preferred_element_type=F32)
    gate = jax.nn.sigmoid(jnp.dot(_rms(x2, gple_ref[...]).astype(BF16), wpg_ref[...], preferred_element_type=F32))
    return x2 + gate * pe


def _gather_rows(src_hbm, index_ref, first, buf, slot, sem):
    if isinstance(slot, int):
        for r in range(buf.shape[1]):
            pltpu.make_async_copy(src_hbm.at[index_ref[first + r]], buf.at[slot, r], sem.at[slot]).start(
                priority=r % 2)
        return

    def issue(r, carry):
        pltpu.make_async_copy(src_hbm.at[index_ref[first + r]], buf.at[slot, r], sem.at[slot]).start()
        return carry
    lax.fori_loop(0, buf.shape[1], issue, 0, unroll=DMA_UNROLL)


def _gather_next_tile(src_hbm, index_ref, i, n_tiles, buf, sem):
    for parity in range(2):
        @pl.when((i + 1 < n_tiles) & (i % 2 == parity))
        def _():
            _gather_rows(src_hbm, index_ref, (i + 1) * buf.shape[1], buf, 1 - parity, sem)


def _wait_rows(src_hbm, buf, slot, sem):
    pltpu.make_async_copy(src_hbm.at[pl.ds(0, buf.shape[1])], buf.at[slot], sem.at[slot]).wait()


def _experts_kernel(src_row_ref, tile_group_ref, n_valid_ref, x1_hbm, gffn_ref, wrg_ref, brg_ref, wre_ref, bre_ref,
                    wg_ref, wu_ref, wd_ref, ys_ref, xbuf, sems):
    i = pl.program_id(0)
    n_valid = n_valid_ref[0]

    @pl.when(i == 0)
    def _():
        _gather_rows(x1_hbm, src_row_ref, 0, xbuf, 0, sems)

    _gather_next_tile(x1_hbm, src_row_ref, i, n_valid, xbuf, sems)

    @pl.when(i < n_valid)
    def _():
        slot = i % 2
        _wait_rows(x1_hbm, xbuf, slot, sems)
        g = tile_group_ref[i]
        lane_i = lax.broadcasted_iota(jnp.int32, (1, LANES), 1)
        lane = lane_i.astype(F32)
        h = _rms(xbuf[slot], gffn_ref[...]).astype(BF16)
        lg = _mm(h, wrg_ref[...], False) + brg_ref[...]
        eg = jnp.exp(lg - jnp.max(lg, axis=-1, keepdims=True))
        pg = jnp.sum(jnp.where(lane_i == g, eg, 0.0), axis=-1, keepdims=True) / jnp.sum(eg, axis=-1, keepdims=True)
        le = _mm(h, wre_ref[...], False) + bre_ref[...]
        comb = _top2_weights(jnp.where(lane_i // EXPERTS_PER_GROUP == g, le, -jnp.inf), lane, pg)
        ys_ref[...] = _group_experts(h, comb, lane_i, g, wg_ref, wu_ref, wd_ref)

    @pl.when(i >= n_valid)
    def _():
        ys_ref[...] = jnp.zeros_like(ys_ref)


def _experts(src_row, tile_group, n_valid, x1, gffn, wrg, brg, wre, bre, wg, wu, wd, *, n_tiles):
    const = lambda i, sr, tg, nv: (0, 0)
    group = lambda i, sr, tg, nv: (tg[i], 0, 0)
    return pl.pallas_call(
        _experts_kernel,
        grid_spec=pltpu.PrefetchScalarGridSpec(
            num_scalar_prefetch=3,
            grid=(n_tiles,),
            in_specs=[
                pl.BlockSpec(memory_space=pl.ANY),
                pl.BlockSpec((1, D_MODEL), const),
                pl.BlockSpec((D_MODEL, LANES), const),
                pl.BlockSpec((1, LANES), const),
                pl.BlockSpec((D_MODEL, LANES), const),
                pl.BlockSpec((1, LANES), const),
                pl.BlockSpec((EXPERTS_PER_GROUP, D_MODEL, D_EXPERT), group),
                pl.BlockSpec((EXPERTS_PER_GROUP, D_MODEL, D_EXPERT), group),
                pl.BlockSpec((EXPERTS_PER_GROUP, D_EXPERT, D_MODEL), group),
            ],
            out_specs=pl.BlockSpec((ROUTE_TILE, D_MODEL), lambda i, sr, tg, nv: (i, 0)),
            scratch_shapes=[pltpu.VMEM((2, ROUTE_TILE, D_MODEL), F32), pltpu.SemaphoreType.DMA((2,))]),
        out_shape=jax.ShapeDtypeStruct((n_tiles * ROUTE_TILE, D_MODEL), F32),
        compiler_params=pltpu.CompilerParams(
            dimension_semantics=("arbitrary",), vmem_limit_bytes=VMEM_LIMIT),
        name="experts",
    )(src_row, tile_group, n_valid, x1, gffn, wrg, brg, wre, bre, wg, wu, wd)


def _combine_kernel(pos_ref, x1_ref, p_ref, gple_ref, wpg_ref, wpe_ref, ys_hbm, out_ref, ybuf, sems, *, tm):
    i = pl.program_id(0)

    @pl.when(i == 0)
    def _():
        _gather_rows(ys_hbm, pos_ref, 0, ybuf, 0, sems)

    _gather_next_tile(ys_hbm, pos_ref, i, pl.num_programs(0), ybuf, sems)

    slot = i % 2
    _wait_rows(ys_hbm, ybuf, slot, sems)
    out_ref[...] = _embedding_gate(x1_ref[...] + ybuf[slot], p_ref, gple_ref, wpg_ref, wpe_ref)


def _combine(pos, x1, p, gple, wpg, wpe, ys, *, tm):
    t = x1.shape[0]
    row = lambda i, pos: (i, 0)
    const = lambda i, pos: (0, 0)
    return pl.pallas_call(
        functools.partial(_combine_kernel, tm=tm),
        grid_spec=pltpu.PrefetchScalarGridSpec(
            num_scalar_prefetch=1,
            grid=(t // tm,),
            in_specs=[pl.BlockSpec((tm, D_MODEL), row),
                      pl.BlockSpec((tm, PLE_DIM), row),
                      pl.BlockSpec((1, D_MODEL), const),
                      pl.BlockSpec((D_MODEL, D_MODEL), const),
                      pl.BlockSpec((PLE_DIM, D_MODEL), const),
                      pl.BlockSpec(memory_space=pl.ANY)],
            out_specs=pl.BlockSpec((tm, D_MODEL), row),
            scratch_shapes=[pltpu.VMEM((2, tm, D_MODEL), F32), pltpu.SemaphoreType.DMA((2,))]),
        out_shape=jax.ShapeDtypeStruct((t, D_MODEL), F32),
        compiler_params=pltpu.CompilerParams(
            dimension_semantics=("arbitrary",), vmem_limit_bytes=VMEM_LIMIT),
        name="combine",
    )(pos, x1, p, gple, wpg, wpe, ys)


def _route_plan(gidx, rank, *, n_tiles):
    t = gidx.shape[0]
    groups = jnp.arange(N_GROUPS, dtype=jnp.int32)
    member = gidx[None, :] == groups[:, None]
    counts = jnp.sum(member, axis=1, dtype=jnp.int32)
    padded = (counts + ROUTE_TILE - 1) // ROUTE_TILE * ROUTE_TILE
    ends = jnp.cumsum(padded)
    offsets = ends - padded
    pos = rank + jnp.sum(jnp.where(member, offsets[:, None], 0), axis=0, dtype=jnp.int32)
    src_row = jnp.zeros((n_tiles * ROUTE_TILE,), jnp.int32).at[pos].set(
        jnp.arange(t, dtype=jnp.int32), unique_indices=True, indices_are_sorted=False)
    n_valid = jnp.maximum(ends[-1] // ROUTE_TILE, 1)
    tile_start = jnp.minimum(jnp.arange(n_tiles, dtype=jnp.int32), n_valid - 1) * ROUTE_TILE
    tile_group = jnp.minimum(jnp.sum(tile_start[:, None] >= ends[None, :], axis=1, dtype=jnp.int32), N_GROUPS - 1)
    return pos, src_row, tile_group, n_valid.reshape(1).astype(jnp.int32)


def _channel_kernel(x1_ref, p_ref, gffn_ref, wrg_ref, brg_ref, wre_ref, bre_ref,
                    wg_ref, wu_ref, wd_ref, gple_ref, wpg_ref, wpe_ref,
                    out_ref, h_ref, comb_ref, acc_ref, *, precise_router):
    g = pl.program_id(1)
    lane_i = lax.broadcasted_iota(jnp.int32, (1, LANES), 1)
    lane = lane_i.astype(F32)

    @pl.when(g == 0)
    def _():
        h = _rms(x1_ref[...], gffn_ref[...])
        h_ref[...] = h.astype(BF16)
        lg = _mm(h, wrg_ref[...], precise_router) + brg_ref[...]
        top_g, g_idx = _first_index_of_max(lg, lane)
        pg_top = 1.0 / jnp.sum(jnp.exp(lg - top_g), axis=-1, keepdims=True)
        le = _mm(h, wre_ref[...], precise_router) + bre_ref[...]
        group_of_lane = (lane_i // EXPERTS_PER_GROUP).astype(F32)
        comb_ref[...] = _top2_weights(jnp.where(group_of_lane == g_idx, le, -jnp.inf), lane, pg_top)

    y = _group_experts(h_ref[...], comb_ref[...], lane_i, g, wg_ref, wu_ref, wd_ref)

    @pl.when(g == 0)
    def _():
        acc_ref[...] = y

    @pl.when(g > 0)
    def _():
        acc_ref[...] += y

    @pl.when(g == N_GROUPS - 1)
    def _():
        out_ref[...] = _embedding_gate(x1_ref[...] + acc_ref[...], p_ref, gple_ref, wpg_ref, wpe_ref)


def _channel(x1, p, gffn, wrg, brg, wre, bre, wg, wu, wd, gple, wpg, wpe, *, tm, precise_router):
    t = x1.shape[0]
    row = lambda i, g: (i, 0)
    const2 = lambda i, g: (0, 0)
    group = lambda i, g: (g, 0, 0)
    return pl.pallas_call(
        functools.partial(_channel_kernel, precise_router=precise_router),
        grid=(t // tm, N_GROUPS),
        in_specs=[
            pl.BlockSpec((tm, D_MODEL), row),
            pl.BlockSpec((tm, PLE_DIM), row),
            pl.BlockSpec((1, D_MODEL), const2),
            pl.BlockSpec((D_MODEL, LANES), const2),
            pl.BlockSpec((1, LANES), const2),
            pl.BlockSpec((D_MODEL, LANES), const2),
            pl.BlockSpec((1, LANES), const2),
            pl.BlockSpec((EXPERTS_PER_GROUP, D_MODEL, D_EXPERT), group),
            pl.BlockSpec((EXPERTS_PER_GROUP, D_MODEL, D_EXPERT), group),
            pl.BlockSpec((EXPERTS_PER_GROUP, D_EXPERT, D_MODEL), group),
            pl.BlockSpec((1, D_MODEL), const2),
            pl.BlockSpec((D_MODEL, D_MODEL), const2),
            pl.BlockSpec((PLE_DIM, D_MODEL), const2),
        ],
        out_specs=pl.BlockSpec((tm, D_MODEL), row),
        out_shape=jax.ShapeDtypeStruct((t, D_MODEL), F32),
        scratch_shapes=[pltpu.VMEM((tm, D_MODEL), BF16),
                        pltpu.VMEM((tm, LANES), F32),
                        pltpu.VMEM((tm, D_MODEL), F32)],
        compiler_params=pltpu.CompilerParams(
            dimension_semantics=("arbitrary", "arbitrary"), vmem_limit_bytes=VMEM_LIMIT),
        name="channel",
    )(x1, p, gffn, wrg, brg, wre, bre, wg, wu, wd, gple, wpg, wpe)


def _pad_lanes(a, fill):
    return jnp.pad(a, ((0, 0), (0, LANES - a.shape[1])), constant_values=fill)


def kernel(x_prompt, x_sample, cache_attn_k, cache_attn_v, p_prompt, p_sample, g_mix, w_in, g_gmlp_v, w_gmlp_s, b_gmlp_s, g_q, g_k, rel_bias, w_branch, w_out, g_ffn, w_router_group, b_router_group, w_router_expert, b_router_expert, w_exp_gate, w_exp_up, w_exp_down, g_ple, w_ple_gate, w_ple_proj):
    depth = g_mix.shape[0]
    batch, seq, _ = x_prompt.shape
    dec_batch, dec_seq, _ = x_sample.shape
    lc = cache_attn_k.shape[2]
    keep = min(N_PAST_CHUNKS * CHUNK, seq)
    tm = 512
    assert depth == 1 and keep == tm and seq % tm == 0 and seq % Q_BLOCK == 0
    assert dec_seq <= GMLP_CHUNK and GMLP_CHUNK % dec_seq == 0 and dec_batch * dec_seq == GMLP_CHUNK
    l = 0
    t_s = dec_batch * dec_seq

    row = lambda a: a.reshape(1, -1).astype(F32)
    w_in_b = w_in[l].astype(BF16)
    gq_t = row(jnp.tile(g_q[l], ATTN_HEADS))
    gk_t = row(jnp.tile(g_k[l], ATTN_HEADS))
    head_of_lane = jnp.arange(D_MODEL, dtype=jnp.int32) // HEAD_DIM
    seg = (head_of_lane[:, None] == jnp.arange(LANES, dtype=jnp.int32)[None, :]).astype(BF16)
    segt2 = jnp.concatenate([seg.T, seg.T], axis=0)
    ws_tril = jnp.tril(w_gmlp_s[l])
    ws_p = ws_tril.astype(BF16)
    bs_p = jnp.repeat(b_gmlp_s[l].T, GMLP_GROUP_DIM, axis=1).astype(F32)
    eye = jnp.eye(dec_batch, dtype=F32)
    ws_s = jnp.einsum('ab,gts->gatbs', eye, ws_tril[:, :dec_seq, :dec_seq]).reshape(
        GMLP_GROUPS, t_s, t_s)
    bs_s = jnp.tile(jnp.repeat(b_gmlp_s[l][:, :dec_seq].T, GMLP_GROUP_DIM, axis=1), (dec_batch, 1)).astype(F32)
    wb_b = w_branch[l].astype(BF16)
    wout_b = w_out[l].astype(BF16)
    wrg = _pad_lanes(w_router_group[l], 0.0)
    brg = _pad_lanes(row(b_router_group[l]), NEG)
    wre = _pad_lanes(w_router_expert[l], 0.0)
    bre = _pad_lanes(row(b_router_expert[l]), 0.0)
    wg_b = w_exp_gate[l].astype(BF16)
    wu_b = w_exp_up[l].astype(BF16)
    wd_b = w_exp_down[l].astype(BF16)
    wpg_b = w_ple_gate[l].astype(BF16)
    wpe_b = w_ple_proj[l].astype(BF16)
    bias_prompt = _prompt_bias_period(rel_bias[l] * LOG2_E)
    bias_c, bias_n = _sample_bias_tables(rel_bias[l], dec_seq, lc, PAST_LEN)

    wrgt = jnp.pad(w_router_group[l].T, ((0, ROUTE_ROWS - N_GROUPS), (0, 0))).astype(BF16)
    brgt = jnp.pad(b_router_group[l].astype(F32), (0, ROUTE_ROWS - N_GROUPS), constant_values=NEG).reshape(ROUTE_ROWS, 1)
    upper = jnp.asarray(np.triu(np.ones((tm, tm), np.float32), 1), BF16)

    xp = x_prompt.reshape(batch * seq, D_MODEL)
    u, vn, q, k, v, sga, sgb, kst, vst = _inproj(
        xp, row(g_mix[l]), w_in_b, row(g_gmlp_v[l]), gq_t, gk_t, seg, segt2,
        tm=tm, tiles_per_batch=seq // tm, emit_vn_state=False, precise=False, q_scale=SCORE_SCALE * LOG2_E)
    ob = _attn_prompt(q, k, v, bias_prompt, batch=batch, seq=seq)
    x1, gidx, rank = _merge(xp, u, vn, sga, sgb, ob, ws_p, bs_p, wb_b, wout_b,
                            (row(g_ffn[l]), wrgt, brgt, upper), tm=tm, precise=False)
    n_route_tiles = batch * seq // ROUTE_TILE + N_GROUPS
    pos, src_row, tile_group, n_valid = _route_plan(gidx.reshape(-1), rank.reshape(-1), n_tiles=n_route_tiles)
    y_sorted = _experts(src_row, tile_group, n_valid, x1, row(g_ffn[l]), wrg.astype(BF16), brg, wre.astype(BF16), bre,
                        wg_b, wu_b, wd_b, n_tiles=n_route_tiles)
    yp = _combine(pos, x1, p_prompt[l].reshape(batch * seq, PLE_DIM), row(g_ple[l]), wpg_b, wpe_b, y_sorted, tm=tm)

    xs = x_sample.reshape(t_s, D_MODEL)
    u, vn, q, k, v, sga, sgb, kss, vss, gvs = _inproj(
        xs, row(g_mix[l]), w_in[l], row(g_gmlp_v[l]), gq_t, gk_t, seg, segt2,
        tm=t_s, tiles_per_batch=1, emit_vn_state=True, precise=True, q_scale=SCORE_SCALE)
    ob = _attn_sample(q, k, v, cache_attn_k[l].reshape(dec_batch, lc, D_MODEL),
                      cache_attn_v[l].reshape(dec_batch, lc, D_MODEL), bias_c, bias_n,
                      batch=dec_batch, rows=dec_seq)
    x1, = _merge(xs, u, vn, sga, sgb, ob, ws_s, bs_s, w_branch[l], w_out[l], tm=t_s, precise=True)
    ys = _channel(x1, p_sample[l].reshape(t_s, PLE_DIM), row(g_ffn[l]), wrg, brg, wre, bre, wg_b, wu_b, wd_b,
                  row(g_ple[l]), wpg_b, wpe_b, tm=t_s, precise_router=True)

    return (yp.reshape(batch, seq, D_MODEL),
            ys.reshape(dec_batch, dec_seq, D_MODEL),
            kst.reshape(1, batch, keep, ATTN_HEADS, HEAD_DIM),
            vst.reshape(1, batch, keep, ATTN_HEADS, HEAD_DIM),
            kss.reshape(1, dec_batch, dec_seq, ATTN_HEADS, HEAD_DIM),
            vss.reshape(1, dec_batch, dec_seq, ATTN_HEADS, HEAD_DIM),
            gvs.reshape(1, dec_batch, dec_seq, D_MODEL))
```

```python
import functools

import jax
import jax.numpy as jnp
import numpy as np
from jax import lax
from jax.experimental import pallas as pl
from jax.experimental.pallas import tpu as pltpu

F32 = jnp.float32
BF16 = jnp.bfloat16

D_MODEL = 1024
CHUNK = 64
N_PAST_CHUNKS = 8
HEAD_DIM = 64
ATTN_HEADS = D_MODEL // HEAD_DIM
MAX_REL = 128
GMLP_CHUNK = 128
GMLP_GROUPS = 4
GMLP_GROUP_DIM = D_MODEL // GMLP_GROUPS
N_SECTIONS = 7
N_GROUPS = 4
EXPERTS_PER_GROUP = 8
N_EXPERTS = N_GROUPS * EXPERTS_PER_GROUP
D_EXPERT = D_MODEL // 4
PLE_DIM = 256
EPS = 1e-6
NEG = -1e30
PAST_LEN = 1024
SCORE_SCALE = HEAD_DIM ** -0.5
LOG2_E = 1.4426950408889634

LANES = 128
Q_BLOCK = 2 * CHUNK
K_WINDOW = Q_BLOCK + N_PAST_CHUNKS * CHUNK
N_SHIFT = N_PAST_CHUNKS * CHUNK // Q_BLOCK + 1
BIAS_SPAN = K_WINDOW + (N_SHIFT - 1) * Q_BLOCK
BIAS_PERIOD = Q_BLOCK + BIAS_SPAN
W_STAGE_COLS = 512
ROUTE_ROWS = 16
ROUTE_TILE = 512
DMA_UNROLL = 8
ATTN_UNROLL = 6
VMEM_LIMIT = 56 * 1024 * 1024


def _rms(x, gain):
    return x * lax.rsqrt(jnp.mean(x * x, axis=-1, keepdims=True) + EPS) * gain


def _act_dtype(precise):
    return F32 if precise else BF16


def _mm(a, b, precise, dims=None):
    dims = dims or (((a.ndim - 1,), (0,)), ((), ()))
    if precise:
        return lax.dot_general(a.astype(F32), b.astype(F32), dims, precision=lax.Precision.HIGHEST,
                               preferred_element_type=F32)
    return lax.dot_general(a.astype(BF16), b.astype(BF16), dims, preferred_element_type=F32)


_NT = (((1,), (1,)), ((), ()))


def _head_sumsq(z, seg, precise):
    return _mm(z * z, seg, precise)


def _head_inv_rms(ssum, segt2, precise):
    inv = lax.rsqrt(ssum * (1.0 / HEAD_DIM) + EPS)
    if precise:
        return _mm(inv, segt2[:LANES], True)
    hi = inv.astype(BF16)
    lo = (inv - hi.astype(F32)).astype(BF16)
    return _mm(jnp.concatenate([hi, lo], axis=1), segt2, False)


def _inproj_kernel(x_ref, gmix_ref, w_ref, gv_ref, gq_ref, gk_ref, seg_ref, segt2_ref,
                   u_ref, vn_ref, q_ref, k_ref, v_ref, sga_ref, sgb_ref,
                   kst_ref, vst_ref, *rest, precise, q_scale):
    act = _act_dtype(precise)
    if precise:
        maybe_vnst_ref = rest
    else:
        *maybe_vnst_ref, w_bf_ref, stage_ref, sems = rest
        w_hbm, w_ref = w_ref, w_bf_ref

        @pl.when(pl.program_id(0) == 0)
        def _():
            width = stage_ref.shape[2]
            n_chunks = w_bf_ref.shape[1] // width

            def chunk_copy(c):
                return pltpu.make_async_copy(w_hbm.at[:, pl.ds(c * width, width)], stage_ref.at[c % 2], sems.at[c % 2])
            chunk_copy(0).start()
            for c in range(n_chunks):
                if c + 1 < n_chunks:
                    chunk_copy(c + 1).start()
                chunk_copy(c).wait()
                w_bf_ref[:, c * width:(c + 1) * width] = stage_ref[c % 2].astype(BF16)

    h = _rms(x_ref[...], gmix_ref[...]).astype(act)

    def section(s):
        return _mm(h, w_ref[:, s * D_MODEL:(s + 1) * D_MODEL], precise)

    z_u = section(0)
    z_v = section(1)
    u_ref[...] = jax.nn.gelu(z_u).astype(act)
    z_q = section(2)
    vn = _rms(jax.nn.gelu(z_v), gv_ref[...])
    vn_ref[...] = vn.astype(act)
    if maybe_vnst_ref:
        maybe_vnst_ref[0][...] = vn
    z_k = section(3)
    ss_q = _head_sumsq(z_q, seg_ref[...], precise)
    v = section(4)
    ss_k = _head_sumsq(z_k, seg_ref[...], precise)
    z_ga = section(5)
    inv_q = _head_inv_rms(ss_q, segt2_ref[...], precise)
    z_gb = section(6)
    inv_k = _head_inv_rms(ss_k, segt2_ref[...], precise)

    v_ref[...] = v.astype(act)
    vst_ref[...] = v
    sga_ref[...] = jax.nn.sigmoid(z_ga).astype(act)
    q_ref[...] = (z_q * inv_q * gq_ref[...] * q_scale).astype(act)
    sgb_ref[...] = jax.nn.sigmoid(z_gb).astype(act)
    kn = z_k * inv_k * gk_ref[...]
    k_ref[...] = kn.astype(act)
    kst_ref[...] = kn


def _inproj(x, gmix, w_in, gv, gq_t, gk_t, seg, segt2, *, tm, tiles_per_batch, emit_vn_state, precise, q_scale):
    t = x.shape[0]
    n_tiles = t // tm
    n_state = n_tiles // tiles_per_batch
    row = lambda i: (i, 0)
    const = lambda i: (0, 0)
    state = lambda i: (i // tiles_per_batch, 0)
    act = jax.ShapeDtypeStruct((t, D_MODEL), _act_dtype(precise))
    st = jax.ShapeDtypeStruct((n_state * tm, D_MODEL), F32)
    out_shape = [act] * 7 + [st, st] + ([st] if emit_vn_state else [])
    out_specs = [pl.BlockSpec((tm, D_MODEL), row)] * 7 + [pl.BlockSpec((tm, D_MODEL), state)] * (
        3 if emit_vn_state else 2)
    if precise:
        w_spec = pl.BlockSpec((D_MODEL, N_SECTIONS * D_MODEL), const, pipeline_mode=pl.Buffered(1))
        scratch = []
    else:
        w_spec = pl.BlockSpec(memory_space=pl.ANY)
        scratch = [pltpu.VMEM((D_MODEL, N_SECTIONS * D_MODEL), BF16), pltpu.VMEM((2, D_MODEL, W_STAGE_COLS), F32),
                   pltpu.SemaphoreType.DMA((2,))]
    return pl.pallas_call(
        functools.partial(_inproj_kernel, precise=precise, q_scale=q_scale),
        grid=(n_tiles,),
        scratch_shapes=scratch,
        in_specs=[
            pl.BlockSpec((tm, D_MODEL), row),
            pl.BlockSpec((1, D_MODEL), const),
            w_spec,
            pl.BlockSpec((1, D_MODEL), const),
            pl.BlockSpec((1, D_MODEL), const),
            pl.BlockSpec((1, D_MODEL), const),
            pl.BlockSpec((D_MODEL, LANES), const),
            pl.BlockSpec((2 * LANES, D_MODEL), const),
        ],
        out_specs=out_specs,
        out_shape=out_shape,
        compiler_params=pltpu.CompilerParams(
            dimension_semantics=("arbitrary",), vmem_limit_bytes=VMEM_LIMIT),
        name="inproj",
    )(x, gmix, w_in, gv, gq_t, gk_t, seg, segt2)


def _attn_prompt_kernel(q_ref, k_ref, v_ref, period_ref, o_ref, bias_ref, *, seq):
    lane = lax.broadcasted_iota(jnp.int32, (1, LANES), 1)
    low_head = lane < HEAD_DIM
    zero = jnp.zeros((), BF16)

    @pl.when(pl.program_id(1) == 0)
    def _():
        row = lax.broadcasted_iota(jnp.int32, (Q_BLOCK, BIAS_SPAN), 0)
        col = lax.broadcasted_iota(jnp.int32, (Q_BLOCK, BIAS_SPAN), 1)
        chunk_gap = col // CHUNK - row // CHUNK
        in_band = (chunk_gap >= 0) & (chunk_gap <= N_PAST_CHUNKS)
        for e in range(2):
            period = jnp.broadcast_to(period_ref[e:e + 1, :], (Q_BLOCK, BIAS_PERIOD))
            toeplitz = pltpu.roll(period, 0, 1, stride=1, stride_axis=0)
            bias_ref[e * Q_BLOCK:(e + 1) * Q_BLOCK, :] = jnp.where(in_band, toeplitz[:, :BIAS_SPAN], NEG)

    def scores(qb, shift):
        q_start = pl.multiple_of(qb * Q_BLOCK, Q_BLOCK)
        k_start = pl.multiple_of((qb + shift) * Q_BLOCK - (K_WINDOW - Q_BLOCK), Q_BLOCK)
        q = q_ref[pl.ds(q_start, Q_BLOCK), :]
        k = k_ref[pl.ds(k_start, K_WINDOW), :]
        q2 = jnp.concatenate([jnp.where(low_head, q, zero), jnp.where(low_head, zero, q)], axis=0)
        bias = bias_ref[:, shift * Q_BLOCK:shift * Q_BLOCK + K_WINDOW]
        return lax.dot_general(q2, k, _NT, preferred_element_type=F32) + bias, q_start, k_start

    ones = jnp.ones((K_WINDOW, LANES), BF16)

    def attend(s, q_start, k_start):
        v = v_ref[pl.ds(k_start, K_WINDOW), :]
        m = jnp.max(s, axis=-1, keepdims=True)
        p = jnp.exp2((s - m).astype(BF16))
        ol = jnp.dot(p, jnp.concatenate([v, ones], axis=1), preferred_element_type=F32)
        o = ol[:, :LANES] * (1.0 / ol[:, LANES:])
        o_ref[pl.ds(q_start, Q_BLOCK), :] = jnp.where(low_head, o[:Q_BLOCK], o[Q_BLOCK:]).astype(BF16)

    def blocks(specs):
        for args in [scores(qb, shift) for qb, shift in specs]:
            attend(*args)

    blocks([(qb, N_SHIFT - 1 - qb) for qb in range(N_SHIFT - 1)])

    def steady(j, carry):
        first = N_SHIFT - 1 + j * ATTN_UNROLL
        blocks([(first + u, 0) for u in range(ATTN_UNROLL)])
        return carry

    lax.fori_loop(0, (seq // Q_BLOCK - (N_SHIFT - 1)) // ATTN_UNROLL, steady, 0)


def _attn_prompt(q, k, v, bias_tab, *, batch, seq):
    q3, k3, v3 = (a.reshape(batch, seq, D_MODEL) for a in (q, k, v))
    head_pair = lambda hp, b: (b, 0, hp)
    spec = pl.BlockSpec((None, seq, LANES), head_pair)
    out = pl.pallas_call(
        functools.partial(_attn_prompt_kernel, seq=seq),
        grid=(ATTN_HEADS // 2, batch),
        in_specs=[spec, spec, spec,
                  pl.BlockSpec((None, 2, BIAS_PERIOD), lambda hp, b: (hp, 0, 0))],
        out_specs=spec,
        out_shape=jax.ShapeDtypeStruct((batch, seq, D_MODEL), BF16),
        scratch_shapes=[pltpu.VMEM((2 * Q_BLOCK, BIAS_SPAN), F32)],
        compiler_params=pltpu.CompilerParams(
            dimension_semantics=("arbitrary", "arbitrary"), vmem_limit_bytes=VMEM_LIMIT),
        name="attn_prompt",
    )(q3, k3, v3, bias_tab)
    return out.reshape(batch * seq, D_MODEL)


def _rel_bias_period(rel_bias, q0, k0, rows, ncols):
    dmax = q0 - k0 + rows - 1
    total = rows + ncols - 1
    dmin = dmax - total + 1
    n_lo = min(max(-MAX_REL - dmin, 0), total)
    n_hi = min(max(dmax - MAX_REL, 0), total)
    mid0 = dmin + n_lo + MAX_REL
    asc = jnp.concatenate([jnp.repeat(rel_bias[:, :1], n_lo, axis=1),
                           rel_bias[:, mid0:mid0 + total - n_lo - n_hi],
                           jnp.repeat(rel_bias[:, -1:], n_hi, axis=1)], axis=1).astype(F32)
    desc = asc[:, ::-1]
    heads = desc.shape[0]
    return jnp.concatenate([desc[:, rows - 1:], jnp.zeros((heads, 1), F32), desc[:, :rows - 1]], axis=1)


def _rel_bias_rows(rel_bias, q0, k0, rows, ncols):
    period = _rel_bias_period(rel_bias, q0, k0, rows, ncols)
    total = rows + ncols - 1
    flat = jnp.tile(period, (1, rows))[:, :rows * total]
    return flat.reshape(period.shape[0], rows, total)[:, :, :ncols]


def _band_ok(q_pos, k_pos):
    q_pos, k_pos = q_pos[:, None], k_pos[None, :]
    return (k_pos >= 0) & (k_pos // CHUNK <= q_pos // CHUNK) & (k_pos // CHUNK >= q_pos // CHUNK - N_PAST_CHUNKS)


def _prompt_bias_period(rel_bias):
    period = _rel_bias_period(rel_bias, N_PAST_CHUNKS * CHUNK, 0, Q_BLOCK, BIAS_SPAN)
    return period.reshape(ATTN_HEADS // 2, 2, BIAS_PERIOD)


def _attn_sample_kernel(q_ref, k_ref, v_ref, ck_ref, cv_ref, bc_ref, bn_ref, o_ref):
    rows = q_ref.shape[0]
    lane = lax.broadcasted_iota(jnp.int32, (1, LANES), 1)
    low_head = lane < HEAD_DIM
    scored = []
    for hp in range(ATTN_HEADS // 2):
        cols = slice(hp * LANES, (hp + 1) * LANES)
        q = q_ref[:, cols]
        q2 = jnp.concatenate([jnp.where(low_head, q, 0.0), jnp.where(low_head, 0.0, q)], axis=0)
        s_c = _mm(q2, ck_ref[:, cols], True, _NT) + bc_ref[hp]
        s_n = _mm(q2, k_ref[:, cols], True, _NT) + bn_ref[hp]
        scored.append((cols, s_c, s_n))
    for cols, s_c, s_n in scored:
        m = jnp.maximum(jnp.max(s_c, axis=-1, keepdims=True), jnp.max(s_n, axis=-1, keepdims=True))
        p_c = jnp.exp(s_c - m)
        p_n = jnp.exp(s_n - m)
        l = jnp.sum(p_c, axis=-1, keepdims=True) + jnp.sum(p_n, axis=-1, keepdims=True)
        o = (_mm(p_c, cv_ref[:, cols], True) + _mm(p_n, v_ref[:, cols], True)) * (1.0 / l)
        o_ref[:, cols] = jnp.where(low_head, o[:rows], o[rows:])


def _attn_sample(q, k, v, cache_k, cache_v, bias_c, bias_n, *, batch, rows):
    lc = cache_k.shape[1]
    new = pl.BlockSpec((rows, D_MODEL), lambda b: (b, 0))
    cache = pl.BlockSpec((None, lc, D_MODEL), lambda b: (b, 0, 0))
    return pl.pallas_call(
        _attn_sample_kernel,
        grid=(batch,),
        in_specs=[new, new, new, cache, cache,
                  pl.BlockSpec((ATTN_HEADS // 2, 2 * rows, lc), lambda b: (0, 0, 0)),
                  pl.BlockSpec((ATTN_HEADS // 2, 2 * rows, rows), lambda b: (0, 0, 0))],
        out_specs=new,
        out_shape=jax.ShapeDtypeStruct((batch * rows, D_MODEL), F32),
        compiler_params=pltpu.CompilerParams(
            dimension_semantics=("arbitrary",), vmem_limit_bytes=VMEM_LIMIT),
        name="attn_sample",
    )(q, k, v, cache_k, cache_v, bias_c, bias_n)


def _sample_bias_tables(rel_bias, rows, lc, past_len):
    q_pos = past_len + np.arange(rows)
    c_pos = past_len - lc + np.arange(lc)
    bias_c = jnp.where(_band_ok(q_pos, c_pos)[None], _rel_bias_rows(rel_bias, past_len, past_len - lc, rows, lc), NEG)
    bias_n = jnp.where(_band_ok(q_pos, q_pos)[None], _rel_bias_rows(rel_bias, past_len, past_len, rows, rows), NEG)
    pair = lambda tab: tab.reshape(ATTN_HEADS // 2, 2 * rows, tab.shape[-1])
    return pair(bias_c), pair(bias_n)


def _route_rows(x1, gffn_ref, wrgt_ref, brgt_ref, upper_ref, gidx_ref, rank_ref, base_ref):
    tm = x1.shape[0]
    h = _rms(x1, gffn_ref[...]).astype(BF16)
    lgt = lax.dot_general(wrgt_ref[...], h, _NT, preferred_element_type=F32) + brgt_ref[...]
    sub = lax.broadcasted_iota(jnp.int32, (ROUTE_ROWS, tm), 0).astype(F32)
    top = jnp.max(lgt, axis=0, keepdims=True)
    gi = jnp.min(jnp.where(lgt == top, sub, float(ROUTE_ROWS)), axis=0, keepdims=True)
    onehot = jnp.where(sub == gi, 1.0, 0.0)
    before = jnp.dot(onehot.astype(BF16), upper_ref[...], preferred_element_type=F32)
    base = base_ref[...]
    rank = jnp.sum(onehot * (before + base[:, :1]), axis=0, keepdims=True)
    base_ref[...] = base + jnp.sum(onehot, axis=1, keepdims=True)
    gidx_ref[...] = gi.astype(jnp.int32)
    rank_ref[...] = rank.astype(jnp.int32)


def _merge_kernel(x_ref, u_ref, vn_ref, sga_ref, sgb_ref, ob_ref, ws_ref, bs_ref, wb_ref, wout_ref,
                  *rest, tm, precise, route):
    if route:
        gffn_ref, wrgt_ref, brgt_ref, upper_ref, x1_ref, gidx_ref, rank_ref, oa_ref, base_ref = rest
    else:
        x1_ref, oa_ref = rest
    act = _act_dtype(precise)
    if route:
        @pl.when(pl.program_id(0) == 0)
        def _():
            base_ref[...] = jnp.zeros_like(base_ref)

    b_b = _mm(ob_ref[...], wb_ref[1], precise)
    for c in range(tm // GMLP_CHUNK):
        rows = slice(c * GMLP_CHUNK, (c + 1) * GMLP_CHUNK)
        for g in range(GMLP_GROUPS):
            cols = slice(g * GMLP_GROUP_DIM, (g + 1) * GMLP_GROUP_DIM)
            mix = _mm(ws_ref[g], vn_ref[rows, cols], precise) + bs_ref[:, cols]
            oa_ref[rows, cols] = (u_ref[rows, cols].astype(F32) * mix).astype(act)
    b_a = _mm(oa_ref[...], wb_ref[0], precise)
    m = sga_ref[...].astype(F32) * b_a + sgb_ref[...].astype(F32) * b_b
    x1 = x_ref[...] + _mm(m, wout_ref[...], precise)
    x1_ref[...] = x1
    if route:
        _route_rows(x1, gffn_ref, wrgt_ref, brgt_ref, upper_ref, gidx_ref, rank_ref, base_ref)


def _merge(x, u, vn, sga, sgb, ob, ws, bs_full, wb, wout, route_params=None, *, tm, precise):
    t = x.shape[0]
    n_tiles = t // tm
    route = route_params is not None
    row = pl.BlockSpec((tm, D_MODEL), lambda i: (i, 0))
    const = lambda i: (0, 0)
    in_specs = [row, row, row, row, row, row,
                pl.BlockSpec((GMLP_GROUPS, GMLP_CHUNK, GMLP_CHUNK), lambda i: (0, 0, 0)),
                pl.BlockSpec((GMLP_CHUNK, D_MODEL), const),
                pl.BlockSpec((2, D_MODEL, D_MODEL), lambda i: (0, 0, 0)),
                pl.BlockSpec((D_MODEL, D_MODEL), const)]
    out_specs = [row]
    out_shape = [jax.ShapeDtypeStruct((t, D_MODEL), F32)]
    scratch = [pltpu.VMEM((tm, D_MODEL), _act_dtype(precise))]
    args = [x, u, vn, sga, sgb, ob, ws, bs_full, wb, wout]
    if route:
        in_specs += [pl.BlockSpec((1, D_MODEL), const), pl.BlockSpec((ROUTE_ROWS, D_MODEL), const),
                     pl.BlockSpec((ROUTE_ROWS, 1), const), pl.BlockSpec((tm, tm), const)]
        tok_row = pl.BlockSpec((None, 1, tm), lambda i: (i, 0, 0))
        out_specs += [tok_row, tok_row]
        out_shape += [jax.ShapeDtypeStruct((n_tiles, 1, tm), jnp.int32)] * 2
        scratch += [pltpu.VMEM((ROUTE_ROWS, LANES), F32)]
        args += list(route_params)
    return pl.pallas_call(
        functools.partial(_merge_kernel, tm=tm, precise=precise, route=route),
        grid=(n_tiles,),
        in_specs=in_specs,
        out_specs=out_specs,
        out_shape=out_shape,
        scratch_shapes=scratch,
        compiler_params=pltpu.CompilerParams(
            dimension_semantics=("arbitrary",), vmem_limit_bytes=VMEM_LIMIT),
        name="merge",
    )(*args)


def _first_index_of_max(vals, lane):
    top = jnp.max(vals, axis=-1, keepdims=True)
    idx = jnp.min(jnp.where(vals == top, lane, float(LANES)), axis=-1, keepdims=True)
    return top, idx


def _top2_weights(le_in_group, lane, pg_top):
    t1, i1 = _first_index_of_max(le_in_group, lane)
    t2, i2 = _first_index_of_max(jnp.where(lane == i1, -jnp.inf, le_in_group), lane)
    e2 = jnp.exp(t2 - t1)
    den = 1.0 + e2
    return jnp.where(lane == i1, pg_top * (1.0 / den), jnp.where(lane == i2, pg_top * (e2 / den), 0.0))


def _group_experts(h, comb, lane_i, g, wg_ref, wu_ref, wd_ref):
    hids = []
    for j in range(EXPERTS_PER_GROUP):
        w_tok = jnp.sum(jnp.where(lane_i == g * EXPERTS_PER_GROUP + j, comb, 0.0), axis=-1, keepdims=True)
        a = jnp.dot(h, wg_ref[j], preferred_element_type=F32)
        b = jnp.dot(h, wu_ref[j], preferred_element_type=F32)
        hids.append((jax.nn.silu(a) * b * w_tok).astype(BF16))
    hid = jnp.concatenate(hids, axis=1)
    return jnp.dot(hid, wd_ref[...].reshape(EXPERTS_PER_GROUP * D_EXPERT, D_MODEL), preferred_element_type=F32)


def _embedding_gate(x2, p_ref, gple_ref, wpg_ref, wpe_ref):
    pe = jnp.dot(p_ref[...].astype(BF16), wpe_ref[...], preferred_element_type=F32)
    gate = jax.nn.sigmoid(jnp.dot(_rms(x2, gple_ref[...]).astype(BF16), wpg_ref[...], preferred_element_type=F32))
    return x2 + gate * pe


def _gather_rows(src_hbm, index_ref, first, buf, slot, sem):
    if isinstance(slot, int):
        for r in range(buf.shape[1]):
            pltpu.make_async_copy(src_hbm.at[index_ref[first + r]], buf.at[slot, r], sem.at[slot]).start(
                priority=r % 2)
        return

    def issue(r, carry):
        pltpu.make_async_copy(src_hbm.at[index_ref[first + r]], buf.at[slot, r], sem.at[slot]).start()
        return carry
    lax.fori_loop(0, buf.shape[1], issue, 0, unroll=DMA_UNROLL)


def _gather_next_tile(src_hbm, index_ref, i, n_tiles, buf, sem):
    for parity in range(2):
        @pl.when((i + 1 < n_tiles) & (i % 2 == parity))
        def _():
            _gather_rows(src_hbm, index_ref, (i + 1) * buf.shape[1], buf, 1 - parity, sem)


def _wait_rows(src_hbm, buf, slot, sem):
    pltpu.make_async_copy(src_hbm.at[pl.ds(0, buf.shape[1])], buf.at[slot], sem.at[slot]).wait()


def _experts_kernel(src_row_ref, tile_group_ref, n_valid_ref, x1_hbm, gffn_ref, wrg_ref, brg_ref, wre_ref, bre_ref,
                    wg_ref, wu_ref, wd_ref, ys_ref, xbuf, sems):
    i = pl.program_id(0)
    n_valid = n_valid_ref[0]

    @pl.when(i == 0)
    def _():
        _gather_rows(x1_hbm, src_row_ref, 0, xbuf, 0, sems)

    _gather_next_tile(x1_hbm, src_row_ref, i, n_valid, xbuf, sems)

    @pl.when(i < n_valid)
    def _():
        slot = i % 2
        _wait_rows(x1_hbm, xbuf, slot, sems)
        g = tile_group_ref[i]
        lane_i = lax.broadcasted_iota(jnp.int32, (1, LANES), 1)
        lane = lane_i.astype(F32)
        h = _rms(xbuf[slot], gffn_ref[...]).astype(BF16)
        lg = _mm(h, wrg_ref[...], False) + brg_ref[...]
        eg = jnp.exp(lg - jnp.max(lg, axis=-1, keepdims=True))
        pg = jnp.sum(jnp.where(lane_i == g, eg, 0.0), axis=-1, keepdims=True) / jnp.sum(eg, axis=-1, keepdims=True)
        le = _mm(h, wre_ref[...], False) + bre_ref[...]
        comb = _top2_weights(jnp.where(lane_i // EXPERTS_PER_GROUP == g, le, -jnp.inf), lane, pg)
        ys_ref[...] = _group_experts(h, comb, lane_i, g, wg_ref, wu_ref, wd_ref)

    @pl.when(i >= n_valid)
    def _():
        ys_ref[...] = jnp.zeros_like(ys_ref)


def _experts(src_row, tile_group, n_valid, x1, gffn, wrg, brg, wre, bre, wg, wu, wd, *, n_tiles):
    const = lambda i, sr, tg, nv: (0, 0)
    group = lambda i, sr, tg, nv: (tg[i], 0, 0)
    return pl.pallas_call(
        _experts_kernel,
        grid_spec=pltpu.PrefetchScalarGridSpec(
            num_scalar_prefetch=3,
            grid=(n_tiles,),
            in_specs=[
                pl.BlockSpec(memory_space=pl.ANY),
                pl.BlockSpec((1, D_MODEL), const),
                pl.BlockSpec((D_MODEL, LANES), const),
                pl.BlockSpec((1, LANES), const),
                pl.BlockSpec((D_MODEL, LANES), const),
                pl.BlockSpec((1, LANES), const),
                pl.BlockSpec((EXPERTS_PER_GROUP, D_MODEL, D_EXPERT), group),
                pl.BlockSpec((EXPERTS_PER_GROUP, D_MODEL, D_EXPERT), group),
                pl.BlockSpec((EXPERTS_PER_GROUP, D_EXPERT, D_MODEL), group),
            ],
            out_specs=pl.BlockSpec((ROUTE_TILE, D_MODEL), lambda i, sr, tg, nv: (i, 0)),
            scratch_shapes=[pltpu.VMEM((2, ROUTE_TILE, D_MODEL), F32), pltpu.SemaphoreType.DMA((2,))]),
        out_shape=jax.ShapeDtypeStruct((n_tiles * ROUTE_TILE, D_MODEL), F32),
        compiler_params=pltpu.CompilerParams(
            dimension_semantics=("arbitrary",), vmem_limit_bytes=VMEM_LIMIT),
        name="experts",
    )(src_row, tile_group, n_valid, x1, gffn, wrg, brg, wre, bre, wg, wu, wd)


def _combine_kernel(pos_ref, x1_ref, p_ref, gple_ref, wpg_ref, wpe_ref, ys_hbm, out_ref, ybuf, sems, *, tm):
    i = pl.program_id(0)

    @pl.when(i == 0)
    def _():
        _gather_rows(ys_hbm, pos_ref, 0, ybuf, 0, sems)

    _gather_next_tile(ys_hbm, pos_ref, i, pl.num_programs(0), ybuf, sems)

    slot = i % 2
    _wait_rows(ys_hbm, ybuf, slot, sems)
    out_ref[...] = _embedding_gate(x1_ref[...] + ybuf[slot], p_ref, gple_ref, wpg_ref, wpe_ref)


def _combine(pos, x1, p, gple, wpg, wpe, ys, *, tm):
    t = x1.shape[0]
    row = lambda i, pos: (i, 0)
    const = lambda i, pos: (0, 0)
    return pl.pallas_call(
        functools.partial(_combine_kernel, tm=tm),
        grid_spec=pltpu.PrefetchScalarGridSpec(
            num_scalar_prefetch=1,
            grid=(t // tm,),
            in_specs=[pl.BlockSpec((tm, D_MODEL), row),
                      pl.BlockSpec((tm, PLE_DIM), row),
                      pl.BlockSpec((1, D_MODEL), const),
                      pl.BlockSpec((D_MODEL, D_MODEL), const),
                      pl.BlockSpec((PLE_DIM, D_MODEL), const),
                      pl.BlockSpec(memory_space=pl.ANY)],
            out_specs=pl.BlockSpec((tm, D_MODEL), row),
            scratch_shapes=[pltpu.VMEM((2, tm, D_MODEL), F32), pltpu.SemaphoreType.DMA((2,))]),
        out_shape=jax.ShapeDtypeStruct((t, D_MODEL), F32),
        compiler_params=pltpu.CompilerParams(
            dimension_semantics=("arbitrary",), vmem_limit_bytes=VMEM_LIMIT),
        name="combine",
    )(pos, x1, p, gple, wpg, wpe, ys)


def _route_plan(gidx, rank, *, n_tiles):
    t = gidx.shape[0]
    groups = jnp.arange(N_GROUPS, dtype=jnp.int32)
    member = gidx[None, :] == groups[:, None]
    counts = jnp.sum(member, axis=1, dtype=jnp.int32)
    padded = (counts + ROUTE_TILE - 1) // ROUTE_TILE * ROUTE_TILE
    ends = jnp.cumsum(padded)
    offsets = ends - padded
    pos = rank + jnp.sum(jnp.where(member, offsets[:, None], 0), axis=0, dtype=jnp.int32)
    src_row = jnp.zeros((n_tiles * ROUTE_TILE,), jnp.int32).at[pos].set(
        jnp.arange(t, dtype=jnp.int32), unique_indices=True, indices_are_sorted=False)
    n_valid = jnp.maximum(ends[-1] // ROUTE_TILE, 1)
    tile_start = jnp.minimum(jnp.arange(n_tiles, dtype=jnp.int32), n_valid - 1) * ROUTE_TILE
    tile_group = jnp.minimum(jnp.sum(tile_start[:, None] >= ends[None, :], axis=1, dtype=jnp.int32), N_GROUPS - 1)
    return pos, src_row, tile_group, n_valid.reshape(1).astype(jnp.int32)


def _channel_kernel(x1_ref, p_ref, gffn_ref, wrg_ref, brg_ref, wre_ref, bre_ref,
                    wg_ref, wu_ref, wd_ref, gple_ref, wpg_ref, wpe_ref,
                    out_ref, h_ref, comb_ref, acc_ref, *, precise_router):
    g = pl.program_id(1)
    lane_i = lax.broadcasted_iota(jnp.int32, (1, LANES), 1)
    lane = lane_i.astype(F32)

    @pl.when(g == 0)
    def _():
        h = _rms(x1_ref[...], gffn_ref[...])
        h_ref[...] = h.astype(BF16)
        lg = _mm(h, wrg_ref[...], precise_router) + brg_ref[...]
        top_g, g_idx = _first_index_of_max(lg, lane)
        pg_top = 1.0 / jnp.sum(jnp.exp(lg - top_g), axis=-1, keepdims=True)
        le = _mm(h, wre_ref[...], precise_router) + bre_ref[...]
        group_of_lane = (lane_i // EXPERTS_PER_GROUP).astype(F32)
        comb_ref[...] = _top2_weights(jnp.where(group_of_lane == g_idx, le, -jnp.inf), lane, pg_top)

    y = _group_experts(h_ref[...], comb_ref[...], lane_i, g, wg_ref, wu_ref, wd_ref)

    @pl.when(g == 0)
    def _():
        acc_ref[...] = y

    @pl.when(g > 0)
    def _():
        acc_ref[...] += y

    @pl.when(g == N_GROUPS - 1)
    def _():
        out_ref[...] = _embedding_gate(x1_ref[...] + acc_ref[...], p_ref, gple_ref, wpg_ref, wpe_ref)


def _channel(x1, p, gffn, wrg, brg, wre, bre, wg, wu, wd, gple, wpg, wpe, *, tm, precise_router):
    t = x1.shape[0]
    row = lambda i, g: (i, 0)
    const2 = lambda i, g: (0, 0)
    group = lambda i, g: (g, 0, 0)
    return pl.pallas_call(
        functools.partial(_channel_kernel, precise_router=precise_router),
        grid=(t // tm, N_GROUPS),
        in_specs=[
            pl.BlockSpec((tm, D_MODEL), row),
            pl.BlockSpec((tm, PLE_DIM), row),
            pl.BlockSpec((1, D_MODEL), const2),
            pl.BlockSpec((D_MODEL, LANES), const2),
            pl.BlockSpec((1, LANES), const2),
            pl.BlockSpec((D_MODEL, LANES), const2),
            pl.BlockSpec((1, LANES), const2),
            pl.BlockSpec((EXPERTS_PER_GROUP, D_MODEL, D_EXPERT), group),
            pl.BlockSpec((EXPERTS_PER_GROUP, D_MODEL, D_EXPERT), group),
            pl.BlockSpec((EXPERTS_PER_GROUP, D_EXPERT, D_MODEL), group),
            pl.BlockSpec((1, D_MODEL), const2),
            pl.BlockSpec((D_MODEL, D_MODEL), const2),
            pl.BlockSpec((PLE_DIM, D_MODEL), const2),
        ],
        out_specs=pl.BlockSpec((tm, D_MODEL), row),
        out_shape=jax.ShapeDtypeStruct((t, D_MODEL), F32),
        scratch_shapes=[pltpu.VMEM((tm, D_MODEL), BF16),
                        pltpu.VMEM((tm, LANES), F32),
                        pltpu.VMEM((tm, D_MODEL), F32)],
        compiler_params=pltpu.CompilerParams(
            dimension_semantics=("arbitrary", "arbitrary"), vmem_limit_bytes=VMEM_LIMIT),
        name="channel",
    )(x1, p, gffn, wrg, brg, wre, bre, wg, wu, wd, gple, wpg, wpe)


def _pad_lanes(a, fill):
    return jnp.pad(a, ((0, 0), (0, LANES - a.shape[1])), constant_values=fill)


def kernel(x_prompt, x_sample, cache_attn_k, cache_attn_v, p_prompt, p_sample, g_mix, w_in, g_gmlp_v, w_gmlp_s, b_gmlp_s, g_q, g_k, rel_bias, w_branch, w_out, g_ffn, w_router_group, b_router_group, w_router_expert, b_router_expert, w_exp_gate, w_exp_up, w_exp_down, g_ple, w_ple_gate, w_ple_proj):
    depth = g_mix.shape[0]
    batch, seq, _ = x_prompt.shape
    dec_batch, dec_seq, _ = x_sample.shape
    lc = cache_attn_k.shape[2]
    keep = min(N_PAST_CHUNKS * CHUNK, seq)
    tm = 512
    assert depth == 1 and keep == tm and seq % tm == 0 and seq % Q_BLOCK == 0
    assert dec_seq <= GMLP_CHUNK and GMLP_CHUNK % dec_seq == 0 and dec_batch * dec_seq == GMLP_CHUNK
    l = 0
    t_s = dec_batch * dec_seq

    row = lambda a: a.reshape(1, -1).astype(F32)
    gq_t = row(jnp.tile(g_q[l], ATTN_HEADS))
    gk_t = row(jnp.tile(g_k[l], ATTN_HEADS))
    head_of_lane = jnp.arange(D_MODEL, dtype=jnp.int32) // HEAD_DIM
    seg = (head_of_lane[:, None] == jnp.arange(LANES, dtype=jnp.int32)[None, :]).astype(BF16)
    segt2 = jnp.concatenate([seg.T, seg.T], axis=0)
    ws_tril = jnp.tril(w_gmlp_s[l])
    ws_p = ws_tril.astype(BF16)
    bs_p = jnp.repeat(b_gmlp_s[l].T, GMLP_GROUP_DIM, axis=1).astype(F32)
    eye = jnp.eye(dec_batch, dtype=F32)
    ws_s = jnp.einsum('ab,gts->gatbs', eye, ws_tril[:, :dec_seq, :dec_seq]).reshape(
        GMLP_GROUPS, t_s, t_s)
    bs_s = jnp.tile(jnp.repeat(b_gmlp_s[l][:, :dec_seq].T, GMLP_GROUP_DIM, axis=1), (dec_batch, 1)).astype(F32)
    wb_b = w_branch[l].astype(BF16)
    wout_b = w_out[l].astype(BF16)
    wrg = _pad_lanes(w_router_group[l], 0.0)
    brg = _pad_lanes(row(b_router_group[l]), NEG)
    wre = _pad_lanes(w_router_expert[l], 0.0)
    bre = _pad_lanes(row(b_router_expert[l]), 0.0)
    wg_b = w_exp_gate[l].astype(BF16)
    wu_b = w_exp_up[l].astype(BF16)
    wd_b = w_exp_down[l].astype(BF16)
    wpg_b = w_ple_gate[l].astype(BF16)
    wpe_b = w_ple_proj[l].astype(BF16)
    bias_prompt = _prompt_bias_period(rel_bias[l] * LOG2_E)
    bias_c, bias_n = _sample_bias_tables(rel_bias[l], dec_seq, lc, PAST_LEN)

    wrgt = jnp.pad(w_router_group[l].T, ((0, ROUTE_ROWS - N_GROUPS), (0, 0))).astype(BF16)
    brgt = jnp.pad(b_router_group[l].astype(F32), (0, ROUTE_ROWS - N_GROUPS), constant_values=NEG).reshape(ROUTE_ROWS, 1)
    upper = jnp.asarray(np.triu(np.ones((tm, tm), np.float32), 1), BF16)

    xp = x_prompt.reshape(batch * seq, D_MODEL)
    u, vn, q, k, v, sga, sgb, kst, vst = _inproj(
        xp, row(g_mix[l]), w_in[l], row(g_gmlp_v[l]), gq_t, gk_t, seg, segt2,
        tm=tm, tiles_per_batch=seq // tm, emit_vn_state=False, precise=False, q_scale=SCORE_SCALE * LOG2_E)
    ob = _attn_prompt(q, k, v, bias_prompt, batch=batch, seq=seq)
    x1, gidx, rank = _merge(xp, u, vn, sga, sgb, ob, ws_p, bs_p, wb_b, wout_b,
                            (row(g_ffn[l]), wrgt, brgt, upper), tm=tm, precise=False)
    n_route_tiles = batch * seq // ROUTE_TILE + N_GROUPS
    pos, src_row, tile_group, n_valid = _route_plan(gidx.reshape(-1), rank.reshape(-1), n_tiles=n_route_tiles)
    y_sorted = _experts(src_row, tile_group, n_valid, x1, row(g_ffn[l]), wrg.astype(BF16), brg, wre.astype(BF16), bre,
                        wg_b, wu_b, wd_b, n_tiles=n_route_tiles)
    yp = _combine(pos, x1, p_prompt[l].reshape(batch * seq, PLE_DIM), row(g_ple[l]), wpg_b, wpe_b, y_sorted, tm=tm)

    xs = x_sample.reshape(t_s, D_MODEL)
    u, vn, q, k, v, sga, sgb, kss, vss, gvs = _inproj(
        xs, row(g_mix[l]), w_in[l], row(g_gmlp_v[l]), gq_t, gk_t, seg, segt2,
        tm=t_s, tiles_per_batch=1, emit_vn_state=True, precise=True, q_scale=SCORE_SCALE)
    ob = _attn_sample(q, k, v, cache_attn_k[l].reshape(dec_batch, lc, D_MODEL),
                      cache_attn_v[l].reshape(dec_batch, lc, D_MODEL), bias_c, bias_n,
                      batch=dec_batch, rows=dec_seq)
    x1, = _merge(xs, u, vn, sga, sgb, ob, ws_s, bs_s, w_branch[l], w_out[l], tm=t_s, precise=True)
    ys = _channel(x1, p_sample[l].reshape(t_s, PLE_DIM), row(g_ffn[l]), wrg, brg, wre, bre, wg_b, wu_b, wd_b,
                  row(g_ple[l]), wpg_b, wpe_b, tm=t_s, precise_router=True)

    return (yp.reshape(batch, seq, D_MODEL),
            ys.reshape(dec_batch, dec_seq, D_MODEL),
            kst.reshape(1, batch, keep, ATTN_HEADS, HEAD_DIM),
            vst.reshape(1, batch, keep, ATTN_HEADS, HEAD_DIM),
            kss.reshape(1, dec_batch, dec_seq, ATTN_HEADS, HEAD_DIM),
            vss.reshape(1, dec_batch, dec_seq, ATTN_HEADS, HEAD_DIM),
            gvs.reshape(1, dec_batch, dec_seq, D_MODEL))
```

```python
import functools

import jax
import jax.numpy as jnp
import numpy as np
from jax import lax
from jax.experimental import pallas as pl
from jax.experimental.pallas import tpu as pltpu

F32 = jnp.float32
BF16 = jnp.bfloat16

D_MODEL = 1024
CHUNK = 64
N_PAST_CHUNKS = 8
HEAD_DIM = 64
ATTN_HEADS = D_MODEL // HEAD_DIM
MAX_REL = 128
GMLP_CHUNK = 128
GMLP_GROUPS = 4
GMLP_GROUP_DIM = D_MODEL // GMLP_GROUPS
N_SECTIONS = 7
N_GROUPS = 4
EXPERTS_PER_GROUP = 8
N_EXPERTS = N_GROUPS * EXPERTS_PER_GROUP
D_EXPERT = D_MODEL // 4
PLE_DIM = 256
EPS = 1e-6
NEG = -1e30
PAST_LEN = 1024
SCORE_SCALE = HEAD_DIM ** -0.5
LOG2_E = 1.4426950408889634

LANES = 128
Q_BLOCK = 2 * CHUNK
K_WINDOW = Q_BLOCK + N_PAST_CHUNKS * CHUNK
N_SHIFT = N_PAST_CHUNKS * CHUNK // Q_BLOCK + 1
BIAS_SPAN = K_WINDOW + (N_SHIFT - 1) * Q_BLOCK
BIAS_PERIOD = Q_BLOCK + BIAS_SPAN
W_STAGE_COLS = 512
ROUTE_ROWS = 16
ROUTE_TILE = 512
DMA_UNROLL = 8
ATTN_UNROLL = 6
VMEM_LIMIT = 56 * 1024 * 1024


def _rms(x, gain):
    return x * lax.rsqrt(jnp.mean(x * x, axis=-1, keepdims=True) + EPS) * gain


def _act_dtype(precise):
    return F32 if precise else BF16


def _mm(a, b, precise, dims=None):
    dims = dims or (((a.ndim - 1,), (0,)), ((), ()))
    if precise:
        return lax.dot_general(a.astype(F32), b.astype(F32), dims, precision=lax.Precision.HIGHEST,
                               preferred_element_type=F32)
    return lax.dot_general(a.astype(BF16), b.astype(BF16), dims, preferred_element_type=F32)


_NT = (((1,), (1,)), ((), ()))


def _head_sumsq(z, seg, precise):
    return _mm(z * z, seg, precise)


def _head_inv_rms(ssum, segt2, precise):
    inv = lax.rsqrt(ssum * (1.0 / HEAD_DIM) + EPS)
    if precise:
        return _mm(inv, segt2[:LANES], True)
    hi = inv.astype(BF16)
    lo = (inv - hi.astype(F32)).astype(BF16)
    return _mm(jnp.concatenate([hi, lo], axis=1), segt2, False)


def _inproj_kernel(x_ref, gmix_ref, w_ref, gv_ref, gq_ref, gk_ref, seg_ref, segt2_ref,
                   u_ref, vn_ref, q_ref, k_ref, v_ref, sga_ref, sgb_ref,
                   kst_ref, vst_ref, *rest, precise, q_scale):
    act = _act_dtype(precise)
    if precise:
        maybe_vnst_ref = rest
    else:
        *maybe_vnst_ref, w_bf_ref, stage_ref, sems = rest
        w_hbm, w_ref = w_ref, w_bf_ref

        @pl.when(pl.program_id(0) == 0)
        def _():
            width = stage_ref.shape[2]
            n_chunks = w_bf_ref.shape[1] // width

            def chunk_copy(c):
                return pltpu.make_async_copy(w_hbm.at[:, pl.ds(c * width, width)], stage_ref.at[c % 2], sems.at[c % 2])
            chunk_copy(0).start()
            for c in range(n_chunks):
                if c + 1 < n_chunks:
                    chunk_copy(c + 1).start()
                chunk_copy(c).wait()
                w_bf_ref[:, c * width:(c + 1) * width] = stage_ref[c % 2].astype(BF16)

    h = _rms(x_ref[...], gmix_ref[...]).astype(act)

    def section(s):
        return _mm(h, w_ref[:, s * D_MODEL:(s + 1) * D_MODEL], precise)

    z_u = section(0)
    z_v = section(1)
    u_ref[...] = jax.nn.gelu(z_u).astype(act)
    z_q = section(2)
    vn = _rms(jax.nn.gelu(z_v), gv_ref[...])
    vn_ref[...] = vn.astype(act)
    if maybe_vnst_ref:
        maybe_vnst_ref[0][...] = vn
    z_k = section(3)
    ss_q = _head_sumsq(z_q, seg_ref[...], precise)
    v = section(4)
    ss_k = _head_sumsq(z_k, seg_ref[...], precise)
    z_ga = section(5)
    inv_q = _head_inv_rms(ss_q, segt2_ref[...], precise)
    z_gb = section(6)
    inv_k = _head_inv_rms(ss_k, segt2_ref[...], precise)

    v_ref[...] = v.astype(act)
    vst_ref[...] = v
    sga_ref[...] = jax.nn.sigmoid(z_ga).astype(act)
    q_ref[...] = (z_q * inv_q * gq_ref[...] * q_scale).astype(act)
    sgb_ref[...] = jax.nn.sigmoid(z_gb).astype(act)
    kn = z_k * inv_k * gk_ref[...]
    k_ref[...] = kn.astype(act)
    kst_ref[...] = kn


def _inproj(x, gmix, w_in, gv, gq_t, gk_t, seg, segt2, *, tm, tiles_per_batch, emit_vn_state, precise, q_scale):
    t = x.shape[0]
    n_tiles = t // tm
    n_state = n_tiles // tiles_per_batch
    row = lambda i: (i, 0)
    const = lambda i: (0, 0)
    state = lambda i: (i // tiles_per_batch, 0)
    act = jax.ShapeDtypeStruct((t, D_MODEL), _act_dtype(precise))
    st = jax.ShapeDtypeStruct((n_state * tm, D_MODEL), F32)
    out_shape = [act] * 7 + [st, st] + ([st] if emit_vn_state else [])
    out_specs = [pl.BlockSpec((tm, D_MODEL), row)] * 7 + [pl.BlockSpec((tm, D_MODEL), state)] * (
        3 if emit_vn_state else 2)
    if precise:
        w_spec = pl.BlockSpec((D_MODEL, N_SECTIONS * D_MODEL), const, pipeline_mode=pl.Buffered(1))
        scratch = []
    else:
        w_spec = pl.BlockSpec(memory_space=pl.ANY)
        scratch = [pltpu.VMEM((D_MODEL, N_SECTIONS * D_MODEL), BF16), pltpu.VMEM((2, D_MODEL, W_STAGE_COLS), F32),
                   pltpu.SemaphoreType.DMA((2,))]
    return pl.pallas_call(
        functools.partial(_inproj_kernel, precise=precise, q_scale=q_scale),
        grid=(n_tiles,),
        scratch_shapes=scratch,
        in_specs=[
            pl.BlockSpec((tm, D_MODEL), row),
            pl.BlockSpec((1, D_MODEL), const),
            w_spec,
            pl.BlockSpec((1, D_MODEL), const),
            pl.BlockSpec((1, D_MODEL), const),
            pl.BlockSpec((1, D_MODEL), const),
            pl.BlockSpec((D_MODEL, LANES), const),
            pl.BlockSpec((2 * LANES, D_MODEL), const),
        ],
        out_specs=out_specs,
        out_shape=out_shape,
        compiler_params=pltpu.CompilerParams(
            dimension_semantics=("arbitrary",), vmem_limit_bytes=VMEM_LIMIT),
        name="inproj",
    )(x, gmix, w_in, gv, gq_t, gk_t, seg, segt2)


def _attn_prompt_kernel(q_ref, k_ref, v_ref, period_ref, o_ref, bias_ref, *, seq):
    lane = lax.broadcasted_iota(jnp.int32, (1, LANES), 1)
    low_head = lane < HEAD_DIM
    zero = jnp.zeros((), BF16)

    @pl.when(pl.program_id(1) == 0)
    def _():
        row = lax.broadcasted_iota(jnp.int32, (Q_BLOCK, BIAS_SPAN), 0)
        col = lax.broadcasted_iota(jnp.int32, (Q_BLOCK, BIAS_SPAN), 1)
        chunk_gap = col // CHUNK - row // CHUNK
        in_band = (chunk_gap >= 0) & (chunk_gap <= N_PAST_CHUNKS)
        for e in range(2):
            period = jnp.broadcast_to(period_ref[e:e + 1, :], (Q_BLOCK, BIAS_PERIOD))
            toeplitz = pltpu.roll(period, 0, 1, stride=1, stride_axis=0)
            bias_ref[e * Q_BLOCK:(e + 1) * Q_BLOCK, :] = jnp.where(in_band, toeplitz[:, :BIAS_SPAN], NEG)

    def scores(qb, shift):
        q_start = pl.multiple_of(qb * Q_BLOCK, Q_BLOCK)
        k_start = pl.multiple_of((qb + shift) * Q_BLOCK - (K_WINDOW - Q_BLOCK), Q_BLOCK)
        q = q_ref[pl.ds(q_start, Q_BLOCK), :]
        k = k_ref[pl.ds(k_start, K_WINDOW), :]
        q2 = jnp.concatenate([jnp.where(low_head, q, zero), jnp.where(low_head, zero, q)], axis=0)
        bias = bias_ref[:, shift * Q_BLOCK:shift * Q_BLOCK + K_WINDOW]
        return lax.dot_general(q2, k, _NT, preferred_element_type=F32) + bias, q_start, k_start

    ones = jnp.ones((K_WINDOW, LANES), BF16)

    def attend(s, q_start, k_start):
        v = v_ref[pl.ds(k_start, K_WINDOW), :]
        m = jnp.max(s, axis=-1, keepdims=True)
        p = jnp.exp2((s - m).astype(BF16))
        ol = jnp.dot(p, jnp.concatenate([v, ones], axis=1), preferred_element_type=F32)
        o = ol[:, :LANES] * (1.0 / ol[:, LANES:])
        o_ref[pl.ds(q_start, Q_BLOCK), :] = jnp.where(low_head, o[:Q_BLOCK], o[Q_BLOCK:]).astype(BF16)

    def blocks(specs):
        for args in [scores(qb, shift) for qb, shift in specs]:
            attend(*args)

    blocks([(qb, N_SHIFT - 1 - qb) for qb in range(N_SHIFT - 1)])

    def steady(j, carry):
        first = N_SHIFT - 1 + j * ATTN_UNROLL
        blocks([(first + u, 0) for u in range(ATTN_UNROLL)])
        return carry

    lax.fori_loop(0, (seq // Q_BLOCK - (N_SHIFT - 1)) // ATTN_UNROLL, steady, 0)


def _attn_prompt(q, k, v, bias_tab, *, batch, seq):
    q3, k3, v3 = (a.reshape(batch, seq, D_MODEL) for a in (q, k, v))
    head_pair = lambda hp, b: (b, 0, hp)
    spec = pl.BlockSpec((None, seq, LANES), head_pair)
    out = pl.pallas_call(
        functools.partial(_attn_prompt_kernel, seq=seq),
        grid=(ATTN_HEADS // 2, batch),
        in_specs=[spec, spec, spec,
                  pl.BlockSpec((None, 2, BIAS_PERIOD), lambda hp, b: (hp, 0, 0))],
        out_specs=spec,
        out_shape=jax.ShapeDtypeStruct((batch, seq, D_MODEL), BF16),
        scratch_shapes=[pltpu.VMEM((2 * Q_BLOCK, BIAS_SPAN), F32)],
        compiler_params=pltpu.CompilerParams(
            dimension_semantics=("arbitrary", "arbitrary"), vmem_limit_bytes=VMEM_LIMIT),
        name="attn_prompt",
    )(q3, k3, v3, bias_tab)
    return out.reshape(batch * seq, D_MODEL)


def _rel_bias_period(rel_bias, q0, k0, rows, ncols):
    dmax = q0 - k0 + rows - 1
    total = rows + ncols - 1
    dmin = dmax - total + 1
    n_lo = min(max(-MAX_REL - dmin, 0), total)
    n_hi = min(max(dmax - MAX_REL, 0), total)
    mid0 = dmin + n_lo + MAX_REL
    asc = jnp.concatenate([jnp.repeat(rel_bias[:, :1], n_lo, axis=1),
                           rel_bias[:, mid0:mid0 + total - n_lo - n_hi],
                           jnp.repeat(rel_bias[:, -1:], n_hi, axis=1)], axis=1).astype(F32)
    desc = asc[:, ::-1]
    heads = desc.shape[0]
    return jnp.concatenate([desc[:, rows - 1:], jnp.zeros((heads, 1), F32), desc[:, :rows - 1]], axis=1)


def _rel_bias_rows(rel_bias, q0, k0, rows, ncols):
    period = _rel_bias_period(rel_bias, q0, k0, rows, ncols)
    total = rows + ncols - 1
    flat = jnp.tile(period, (1, rows))[:, :rows * total]
    return flat.reshape(period.shape[0], rows, total)[:, :, :ncols]


def _band_ok(q_pos, k_pos):
    q_pos, k_pos = q_pos[:, None], k_pos[None, :]
    return (k_pos >= 0) & (k_pos // CHUNK <= q_pos // CHUNK) & (k_pos // CHUNK >= q_pos // CHUNK - N_PAST_CHUNKS)


def _prompt_bias_period(rel_bias):
    period = _rel_bias_period(rel_bias, N_PAST_CHUNKS * CHUNK, 0, Q_BLOCK, BIAS_SPAN)
    return period.reshape(ATTN_HEADS // 2, 2, BIAS_PERIOD)


def _attn_sample_kernel(q_ref, k_ref, v_ref, ck_ref, cv_ref, bc_ref, bn_ref, o_ref):
    rows = q_ref.shape[0]
    lane = lax.broadcasted_iota(jnp.int32, (1, LANES), 1)
    low_head = lane < HEAD_DIM
    scored = []
    for hp in range(ATTN_HEADS // 2):
        cols = slice(hp * LANES, (hp + 1) * LANES)
        q = q_ref[:, cols]
        q2 = jnp.concatenate([jnp.where(low_head, q, 0.0), jnp.where(low_head, 0.0, q)], axis=0)
        s_c = _mm(q2, ck_ref[:, cols], True, _NT) + bc_ref[hp]
        s_n = _mm(q2, k_ref[:, cols], True, _NT) + bn_ref[hp]
        scored.append((cols, s_c, s_n))
    for cols, s_c, s_n in scored:
        m = jnp.maximum(jnp.max(s_c, axis=-1, keepdims=True), jnp.max(s_n, axis=-1, keepdims=True))
        p_c = jnp.exp(s_c - m)
        p_n = jnp.exp(s_n - m)
        l = jnp.sum(p_c, axis=-1, keepdims=True) + jnp.sum(p_n, axis=-1, keepdims=True)
        o = (_mm(p_c, cv_ref[:, cols], True) + _mm(p_n, v_ref[:, cols], True)) * (1.0 / l)
        o_ref[:, cols] = jnp.where(low_head, o[:rows], o[rows:])


def _attn_sample(q, k, v, cache_k, cache_v, bias_c, bias_n, *, batch, rows):
    lc = cache_k.shape[1]
    new = pl.BlockSpec((rows, D_MODEL), lambda b: (b, 0))
    cache = pl.BlockSpec((None, lc, D_MODEL), lambda b: (b, 0, 0))
    return pl.pallas_call(
        _attn_sample_kernel,
        grid=(batch,),
        in_specs=[new, new, new, cache, cache,
                  pl.BlockSpec((ATTN_HEADS // 2, 2 * rows, lc), lambda b: (0, 0, 0)),
                  pl.BlockSpec((ATTN_HEADS // 2, 2 * rows, rows), lambda b: (0, 0, 0))],
        out_specs=new,
        out_shape=jax.ShapeDtypeStruct((batch * rows, D_MODEL), F32),
        compiler_params=pltpu.CompilerParams(
            dimension_semantics=("arbitrary",), vmem_limit_bytes=VMEM_LIMIT),
        name="attn_sample",
    )(q, k, v, cache_k, cache_v, bias_c, bias_n)


def _sample_bias_tables(rel_bias, rows, lc, past_len):
    q_pos = past_len + np.arange(rows)
    c_pos = past_len - lc + np.arange(lc)
    bias_c = jnp.where(_band_ok(q_pos, c_pos)[None], _rel_bias_rows(rel_bias, past_len, past_len - lc, rows, lc), NEG)
    bias_n = jnp.where(_band_ok(q_pos, q_pos)[None], _rel_bias_rows(rel_bias, past_len, past_len, rows, rows), NEG)
    pair = lambda tab: tab.reshape(ATTN_HEADS // 2, 2 * rows, tab.shape[-1])
    return pair(bias_c), pair(bias_n)


def _route_rows(x1, gffn_ref, wrgt_ref, brgt_ref, upper_ref, gidx_ref, rank_ref, base_ref):
    tm = x1.shape[0]
    h = _rms(x1, gffn_ref[...]).astype(BF16)
    lgt = lax.dot_general(wrgt_ref[...], h, _NT, preferred_element_type=F32) + brgt_ref[...]
    sub = lax.broadcasted_iota(jnp.int32, (ROUTE_ROWS, tm), 0).astype(F32)
    top = jnp.max(lgt, axis=0, keepdims=True)
    gi = jnp.min(jnp.where(lgt == top, sub, float(ROUTE_ROWS)), axis=0, keepdims=True)
    onehot = jnp.where(sub == gi, 1.0, 0.0)
    before = jnp.dot(onehot.astype(BF16), upper_ref[...], preferred_element_type=F32)
    base = base_ref[...]
    rank = jnp.sum(onehot * (before + base[:, :1]), axis=0, keepdims=True)
    base_ref[...] = base + jnp.sum(onehot, axis=1, keepdims=True)
    gidx_ref[...] = gi.astype(jnp.int32)
    rank_ref[...] = rank.astype(jnp.int32)


def _merge_kernel(x_ref, u_ref, vn_ref, sga_ref, sgb_ref, ob_ref, ws_ref, bs_ref, wb_ref, wout_ref,
                  *rest, tm, precise, route):
    if route:
        gffn_ref, wrgt_ref, brgt_ref, upper_ref, x1_ref, gidx_ref, rank_ref, oa_ref, base_ref = rest
    else:
        x1_ref, oa_ref = rest
    act = _act_dtype(precise)
    if route:
        @pl.when(pl.program_id(0) == 0)
        def _():
            base_ref[...] = jnp.zeros_like(base_ref)

    b_b = _mm(ob_ref[...], wb_ref[1], precise)
    for c in range(tm // GMLP_CHUNK):
        rows = slice(c * GMLP_CHUNK, (c + 1) * GMLP_CHUNK)
        for g in range(GMLP_GROUPS):
            cols = slice(g * GMLP_GROUP_DIM, (g + 1) * GMLP_GROUP_DIM)
            mix = _mm(ws_ref[g], vn_ref[rows, cols], precise) + bs_ref[:, cols]
            oa_ref[rows, cols] = (u_ref[rows, cols].astype(F32) * mix).astype(act)
    b_a = _mm(oa_ref[...], wb_ref[0], precise)
    m = sga_ref[...].astype(F32) * b_a + sgb_ref[...].astype(F32) * b_b
    x1 = x_ref[...] + _mm(m, wout_ref[...], precise)
    x1_ref[...] = x1
    if route:
        _route_rows(x1, gffn_ref, wrgt_ref, brgt_ref, upper_ref, gidx_ref, rank_ref, base_ref)


def _merge(x, u, vn, sga, sgb, ob, ws, bs_full, wb, wout, route_params=None, *, tm, precise):
    t = x.shape[0]
    n_tiles = t // tm
    route = route_params is not None
    row = pl.BlockSpec((tm, D_MODEL), lambda i: (i, 0))
    const = lambda i: (0, 0)
    in_specs = [row, row, row, row, row, row,
                pl.BlockSpec((GMLP_GROUPS, GMLP_CHUNK, GMLP_CHUNK), lambda i: (0, 0, 0)),
                pl.BlockSpec((GMLP_CHUNK, D_MODEL), const),
                pl.BlockSpec((2, D_MODEL, D_MODEL), lambda i: (0, 0, 0)),
                pl.BlockSpec((D_MODEL, D_MODEL), const)]
    out_specs = [row]
    out_shape = [jax.ShapeDtypeStruct((t, D_MODEL), F32)]
    scratch = [pltpu.VMEM((tm, D_MODEL), _act_dtype(precise))]
    args = [x, u, vn, sga, sgb, ob, ws, bs_full, wb, wout]
    if route:
        in_specs += [pl.BlockSpec((1, D_MODEL), const), pl.BlockSpec((ROUTE_ROWS, D_MODEL), const),
                     pl.BlockSpec((ROUTE_ROWS, 1), const), pl.BlockSpec((tm, tm), const)]
        tok_row = pl.BlockSpec((None, 1, tm), lambda i: (i, 0, 0))
        out_specs += [tok_row, tok_row]
        out_shape += [jax.ShapeDtypeStruct((n_tiles, 1, tm), jnp.int32)] * 2
        scratch += [pltpu.VMEM((ROUTE_ROWS, LANES), F32)]
        args += list(route_params)
    return pl.pallas_call(
        functools.partial(_merge_kernel, tm=tm, precise=precise, route=route),
        grid=(n_tiles,),
        in_specs=in_specs,
        out_specs=out_specs,
        out_shape=out_shape,
        scratch_shapes=scratch,
        compiler_params=pltpu.CompilerParams(
            dimension_semantics=("arbitrary",), vmem_limit_bytes=VMEM_LIMIT),
        name="merge",
    )(*args)


def _first_index_of_max(vals, lane):
    top = jnp.max(vals, axis=-1, keepdims=True)
    idx = jnp.min(jnp.where(vals == top, lane, float(LANES)), axis=-1, keepdims=True)
    return top, idx


def _top2_weights(le_in_group, lane, pg_top):
    t1, i1 = _first_index_of_max(le_in_group, lane)
    t2, i2 = _first_index_of_max(jnp.where(lane == i1, -jnp.inf, le_in_group), lane)
    e2 = jnp.exp(t2 - t1)
    den = 1.0 + e2
    return jnp.where(lane == i1, pg_top * (1.0 / den), jnp.where(lane == i2, pg_top * (e2 / den), 0.0))


def _group_experts(h, comb, lane_i, g, wg_ref, wu_ref, wd_ref):
    hids = []
    for j in range(EXPERTS_PER_GROUP):
        w_tok = jnp.sum(jnp.where(lane_i == g * EXPERTS_PER_GROUP + j, comb, 0.0), axis=-1, keepdims=True)
        a = jnp.dot(h, wg_ref[j], preferred_element_type=F32)
        b = jnp.dot(h, wu_ref[j], preferred_element_type=F32)
        hids.append((jax.nn.silu(a) * b * w_tok).astype(BF16))
    hid = jnp.concatenate(hids, axis=1)
    return jnp.dot(hid, wd_ref[...].reshape(EXPERTS_PER_GROUP * D_EXPERT, D_MODEL), preferred_element_type=F32)


def _embedding_gate(x2, p_ref, gple_ref, wpg_ref, wpe_ref):
    pe = jnp.dot(p_ref[...].astype(BF16), wpe_ref[...], preferred_element_type=F32)
    gate = jax.nn.sigmoid(jnp.dot(_rms(x2, gple_ref[...]).astype(BF16), wpg_ref[...], preferred_element_type=F32))
    return x2 + gate * pe


def _gather_rows(src_hbm, index_ref, first, buf, slot, sem):
    if isinstance(slot, int):
        for r in range(buf.shape[1]):
            pltpu.make_async_copy(src_hbm.at[index_ref[first + r]], buf.at[slot, r], sem.at[slot]).start(
                priority=r % 2)
        return

    def issue(r, carry):
        pltpu.make_async_copy(src_hbm.at[index_ref[first + r]], buf.at[slot, r], sem.at[slot]).start()
        return carry
    lax.fori_loop(0, buf.shape[1], issue, 0, unroll=DMA_UNROLL)


def _gather_next_tile(src_hbm, index_ref, i, n_tiles, buf, sem):
    for parity in range(2):
        @pl.when((i + 1 < n_tiles) & (i % 2 == parity))
        def _():
            _gather_rows(src_hbm, index_ref, (i + 1) * buf.shape[1], buf, 1 - parity, sem)


def _wait_rows(src_hbm, buf, slot, sem):
    pltpu.make_async_copy(src_hbm.at[pl.ds(0, buf.shape[1])], buf.at[slot], sem.at[slot]).wait()


def _experts_kernel(src_row_ref, tile_group_ref, n_valid_ref, x1_hbm, gffn_ref, wroute_ref, broute_ref,
                    wg_ref, wu_ref, wd_ref, ys_ref, xbuf, sems):
    i = pl.program_id(0)
    n_valid = n_valid_ref[0]

    @pl.when(i == 0)
    def _():
        _gather_rows(x1_hbm, src_row_ref, 0, xbuf, 0, sems)

    _gather_next_tile(x1_hbm, src_row_ref, i, n_valid, xbuf, sems)

    @pl.when(i < n_valid)
    def _():
        slot = i % 2
        _wait_rows(x1_hbm, xbuf, slot, sems)
        g = tile_group_ref[i]
        lane_i = lax.broadcasted_iota(jnp.int32, (1, LANES), 1)
        lane = lane_i.astype(F32)
        h = _rms(xbuf[slot], gffn_ref[...]).astype(BF16)
        logits = _mm(h, wroute_ref[...], False) + broute_ref[...]
        is_group = (lane_i >= N_EXPERTS) & (lane_i < N_EXPERTS + N_GROUPS)
        lg = jnp.where(is_group, logits, NEG)
        eg = jnp.exp(lg - jnp.max(lg, axis=-1, keepdims=True))
        pg = (jnp.sum(jnp.where(lane_i == N_EXPERTS + g, eg, 0.0), axis=-1, keepdims=True)
              / jnp.sum(eg, axis=-1, keepdims=True))
        comb = _top2_weights(jnp.where(lane_i // EXPERTS_PER_GROUP == g, logits, -jnp.inf), lane, pg)
        ys_ref[...] = _group_experts(h, comb, lane_i, g, wg_ref, wu_ref, wd_ref)

    @pl.when(i >= n_valid)
    def _():
        ys_ref[...] = jnp.zeros_like(ys_ref)


def _experts(src_row, tile_group, n_valid, x1, gffn, wroute, broute, wg, wu, wd, *, n_tiles):
    const = lambda i, sr, tg, nv: (0, 0)
    group = lambda i, sr, tg, nv: (tg[i], 0, 0)
    return pl.pallas_call(
        _experts_kernel,
        grid_spec=pltpu.PrefetchScalarGridSpec(
            num_scalar_prefetch=3,
            grid=(n_tiles,),
            in_specs=[
                pl.BlockSpec(memory_space=pl.ANY),
                pl.BlockSpec((1, D_MODEL), const),
                pl.BlockSpec((D_MODEL, LANES), const),
                pl.BlockSpec((1, LANES), const),
                pl.BlockSpec((EXPERTS_PER_GROUP, D_MODEL, D_EXPERT), group),
                pl.BlockSpec((EXPERTS_PER_GROUP, D_MODEL, D_EXPERT), group),
                pl.BlockSpec((EXPERTS_PER_GROUP, D_EXPERT, D_MODEL), group),
            ],
            out_specs=pl.BlockSpec((ROUTE_TILE, D_MODEL), lambda i, sr, tg, nv: (i, 0)),
            scratch_shapes=[pltpu.VMEM((2, ROUTE_TILE, D_MODEL), F32), pltpu.SemaphoreType.DMA((2,))]),
        out_shape=jax.ShapeDtypeStruct((n_tiles * ROUTE_TILE, D_MODEL), F32),
        compiler_params=pltpu.CompilerParams(
            dimension_semantics=("arbitrary",), vmem_limit_bytes=VMEM_LIMIT),
        name="experts",
    )(src_row, tile_group, n_valid, x1, gffn, wroute, broute, wg, wu, wd)


def _combine_kernel(pos_ref, x1_ref, p_ref, gple_ref, wpg_ref, wpe_ref, ys_hbm, out_ref, ybuf, sems, *, tm):
    i = pl.program_id(0)

    @pl.when(i == 0)
    def _():
        _gather_rows(ys_hbm, pos_ref, 0, ybuf, 0, sems)

    _gather_next_tile(ys_hbm, pos_ref, i, pl.num_programs(0), ybuf, sems)

    slot = i % 2
    _wait_rows(ys_hbm, ybuf, slot, sems)
    out_ref[...] = _embedding_gate(x1_ref[...] + ybuf[slot], p_ref, gple_ref, wpg_ref, wpe_ref)


def _combine(pos, x1, p, gple, wpg, wpe, ys, *, tm):
    t = x1.shape[0]
    row = lambda i, pos: (i, 0)
    const = lambda i, pos: (0, 0)
    return pl.pallas_call(
        functools.partial(_combine_kernel, tm=tm),
        grid_spec=pltpu.PrefetchScalarGridSpec(
            num_scalar_prefetch=1,
            grid=(t // tm,),
            in_specs=[pl.BlockSpec((tm, D_MODEL), row),
                      pl.BlockSpec((tm, PLE_DIM), row),
                      pl.BlockSpec((1, D_MODEL), const),
                      pl.BlockSpec((D_MODEL, D_MODEL), const),
                      pl.BlockSpec((PLE_DIM, D_MODEL), const),
                      pl.BlockSpec(memory_space=pl.ANY)],
            out_specs=pl.BlockSpec((tm, D_MODEL), row),
            scratch_shapes=[pltpu.VMEM((2, tm, D_MODEL), F32), pltpu.SemaphoreType.DMA((2,))]),
        out_shape=jax.ShapeDtypeStruct((t, D_MODEL), F32),
        compiler_params=pltpu.CompilerParams(
            dimension_semantics=("arbitrary",), vmem_limit_bytes=VMEM_LIMIT),
        name="combine",
    )(pos, x1, p, gple, wpg, wpe, ys)


def _route_plan(gidx, rank, *, n_tiles):
    t = gidx.shape[0]
    groups = jnp.arange(N_GROUPS, dtype=jnp.int32)
    member = gidx[None, :] == groups[:, None]
    counts = jnp.sum(member, axis=1, dtype=jnp.int32)
    padded = (counts + ROUTE_TILE - 1) // ROUTE_TILE * ROUTE_TILE
    ends = jnp.cumsum(padded)
    offsets = ends - padded
    pos = rank + jnp.sum(jnp.where(member, offsets[:, None], 0), axis=0, dtype=jnp.int32)
    src_row = jnp.zeros((n_tiles * ROUTE_TILE,), jnp.int32).at[pos].set(
        jnp.arange(t, dtype=jnp.int32), unique_indices=True, indices_are_sorted=False)
    n_valid = jnp.maximum(ends[-1] // ROUTE_TILE, 1)
    tile_start = jnp.minimum(jnp.arange(n_tiles, dtype=jnp.int32), n_valid - 1) * ROUTE_TILE
    tile_group = jnp.minimum(jnp.sum(tile_start[:, None] >= ends[None, :], axis=1, dtype=jnp.int32), N_GROUPS - 1)
    return pos, src_row, tile_group, n_valid.reshape(1).astype(jnp.int32)


def _channel_kernel(x1_ref, p_ref, gffn_ref, wrg_ref, brg_ref, wre_ref, bre_ref,
                    wg_ref, wu_ref, wd_ref, gple_ref, wpg_ref, wpe_ref,
                    out_ref, h_ref, comb_ref, acc_ref, *, precise_router):
    g = pl.program_id(1)
    lane_i = lax.broadcasted_iota(jnp.int32, (1, LANES), 1)
    lane = lane_i.astype(F32)

    @pl.when(g == 0)
    def _():
        h = _rms(x1_ref[...], gffn_ref[...])
        h_ref[...] = h.astype(BF16)
        lg = _mm(h, wrg_ref[...], precise_router) + brg_ref[...]
        top_g, g_idx = _first_index_of_max(lg, lane)
        pg_top = 1.0 / jnp.sum(jnp.exp(lg - top_g), axis=-1, keepdims=True)
        le = _mm(h, wre_ref[...], precise_router) + bre_ref[...]
        group_of_lane = (lane_i // EXPERTS_PER_GROUP).astype(F32)
        comb_ref[...] = _top2_weights(jnp.where(group_of_lane == g_idx, le, -jnp.inf), lane, pg_top)

    y = _group_experts(h_ref[...], comb_ref[...], lane_i, g, wg_ref, wu_ref, wd_ref)

    @pl.when(g == 0)
    def _():
        acc_ref[...] = y

    @pl.when(g > 0)
    def _():
        acc_ref[...] += y

    @pl.when(g == N_GROUPS - 1)
    def _():
        out_ref[...] = _embedding_gate(x1_ref[...] + acc_ref[...], p_ref, gple_ref, wpg_ref, wpe_ref)


def _channel(x1, p, gffn, wrg, brg, wre, bre, wg, wu, wd, gple, wpg, wpe, *, tm, precise_router):
    t = x1.shape[0]
    row = lambda i, g: (i, 0)
    const2 = lambda i, g: (0, 0)
    group = lambda i, g: (g, 0, 0)
    return pl.pallas_call(
        functools.partial(_channel_kernel, precise_router=precise_router),
        grid=(t // tm, N_GROUPS),
        in_specs=[
            pl.BlockSpec((tm, D_MODEL), row),
            pl.BlockSpec((tm, PLE_DIM), row),
            pl.BlockSpec((1, D_MODEL), const2),
            pl.BlockSpec((D_MODEL, LANES), const2),
            pl.BlockSpec((1, LANES), const2),
            pl.BlockSpec((D_MODEL, LANES), const2),
            pl.BlockSpec((1, LANES), const2),
            pl.BlockSpec((EXPERTS_PER_GROUP, D_MODEL, D_EXPERT), group),
            pl.BlockSpec((EXPERTS_PER_GROUP, D_MODEL, D_EXPERT), group),
            pl.BlockSpec((EXPERTS_PER_GROUP, D_EXPERT, D_MODEL), group),
            pl.BlockSpec((1, D_MODEL), const2),
            pl.BlockSpec((D_MODEL, D_MODEL), const2),
            pl.BlockSpec((PLE_DIM, D_MODEL), const2),
        ],
        out_specs=pl.BlockSpec((tm, D_MODEL), row),
        out_shape=jax.ShapeDtypeStruct((t, D_MODEL), F32),
        scratch_shapes=[pltpu.VMEM((tm, D_MODEL), BF16),
                        pltpu.VMEM((tm, LANES), F32),
                        pltpu.VMEM((tm, D_MODEL), F32)],
        compiler_params=pltpu.CompilerParams(
            dimension_semantics=("arbitrary", "arbitrary"), vmem_limit_bytes=VMEM_LIMIT),
        name="channel",
    )(x1, p, gffn, wrg, brg, wre, bre, wg, wu, wd, gple, wpg, wpe)


def _pad_lanes(a, fill):
    return jnp.pad(a, ((0, 0), (0, LANES - a.shape[1])), constant_values=fill)


def kernel(x_prompt, x_sample, cache_attn_k, cache_attn_v, p_prompt, p_sample, g_mix, w_in, g_gmlp_v, w_gmlp_s, b_gmlp_s, g_q, g_k, rel_bias, w_branch, w_out, g_ffn, w_router_group, b_router_group, w_router_expert, b_router_expert, w_exp_gate, w_exp_up, w_exp_down, g_ple, w_ple_gate, w_ple_proj):
    depth = g_mix.shape[0]
    batch, seq, _ = x_prompt.shape
    dec_batch, dec_seq, _ = x_sample.shape
    lc = cache_attn_k.shape[2]
    keep = min(N_PAST_CHUNKS * CHUNK, seq)
    tm = 512
    assert depth == 1 and keep == tm and seq % tm == 0 and seq % Q_BLOCK == 0
    assert dec_seq <= GMLP_CHUNK and GMLP_CHUNK % dec_seq == 0 and dec_batch * dec_seq == GMLP_CHUNK
    l = 0
    t_s = dec_batch * dec_seq

    row = lambda a: a.reshape(1, -1).astype(F32)
    gq_t = row(jnp.tile(g_q[l], ATTN_HEADS))
    gk_t = row(jnp.tile(g_k[l], ATTN_HEADS))
    head_of_lane = jnp.arange(D_MODEL, dtype=jnp.int32) // HEAD_DIM
    seg = (head_of_lane[:, None] == jnp.arange(LANES, dtype=jnp.int32)[None, :]).astype(BF16)
    segt2 = jnp.concatenate([seg.T, seg.T], axis=0)
    ws_tril = jnp.tril(w_gmlp_s[l])
    ws_p = ws_tril.astype(BF16)
    bs_p = jnp.repeat(b_gmlp_s[l].T, GMLP_GROUP_DIM, axis=1).astype(F32)
    eye = jnp.eye(dec_batch, dtype=F32)
    ws_s = jnp.einsum('ab,gts->gatbs', eye, ws_tril[:, :dec_seq, :dec_seq]).reshape(
        GMLP_GROUPS, t_s, t_s)
    bs_s = jnp.tile(jnp.repeat(b_gmlp_s[l][:, :dec_seq].T, GMLP_GROUP_DIM, axis=1), (dec_batch, 1)).astype(F32)
    wb_b = w_branch[l].astype(BF16)
    wout_b = w_out[l].astype(BF16)
    wrg = _pad_lanes(w_router_group[l], 0.0)
    brg = _pad_lanes(row(b_router_group[l]), NEG)
    wre = _pad_lanes(w_router_expert[l], 0.0)
    bre = _pad_lanes(row(b_router_expert[l]), 0.0)
    wg_b = w_exp_gate[l].astype(BF16)
    wu_b = w_exp_up[l].astype(BF16)
    wd_b = w_exp_down[l].astype(BF16)
    wpg_b = w_ple_gate[l].astype(BF16)
    wpe_b = w_ple_proj[l].astype(BF16)
    bias_prompt = _prompt_bias_period(rel_bias[l] * LOG2_E)
    bias_c, bias_n = _sample_bias_tables(rel_bias[l], dec_seq, lc, PAST_LEN)

    wrgt = jnp.pad(w_router_group[l].T, ((0, ROUTE_ROWS - N_GROUPS), (0, 0))).astype(BF16)
    brgt = jnp.pad(b_router_group[l].astype(F32), (0, ROUTE_ROWS - N_GROUPS), constant_values=NEG).reshape(ROUTE_ROWS, 1)
    upper = jnp.asarray(np.triu(np.ones((tm, tm), np.float32), 1), BF16)

    xp = x_prompt.reshape(batch * seq, D_MODEL)
    u, vn, q, k, v, sga, sgb, kst, vst = _inproj(
        xp, row(g_mix[l]), w_in[l], row(g_gmlp_v[l]), gq_t, gk_t, seg, segt2,
        tm=tm, tiles_per_batch=seq // tm, emit_vn_state=False, precise=False, q_scale=SCORE_SCALE * LOG2_E)
    ob = _attn_prompt(q, k, v, bias_prompt, batch=batch, seq=seq)
    x1, gidx, rank = _merge(xp, u, vn, sga, sgb, ob, ws_p, bs_p, wb_b, wout_b,
                            (row(g_ffn[l]), wrgt, brgt, upper), tm=tm, precise=False)
    n_route_tiles = batch * seq // ROUTE_TILE + N_GROUPS
    pos, src_row, tile_group, n_valid = _route_plan(gidx.reshape(-1), rank.reshape(-1), n_tiles=n_route_tiles)
    w_route = _pad_lanes(jnp.concatenate([w_router_expert[l], w_router_group[l]], axis=1), 0.0).astype(BF16)
    b_route = _pad_lanes(row(jnp.concatenate([b_router_expert[l], b_router_group[l]])), 0.0)
    y_sorted = _experts(src_row, tile_group, n_valid, x1, row(g_ffn[l]), w_route, b_route,
                        wg_b, wu_b, wd_b, n_tiles=n_route_tiles)
    yp = _combine(pos, x1, p_prompt[l].reshape(batch * seq, PLE_DIM), row(g_ple[l]), wpg_b, wpe_b, y_sorted, tm=tm)

    xs = x_sample.reshape(t_s, D_MODEL)
    u, vn, q, k, v, sga, sgb, kss, vss, gvs = _inproj(
        xs, row(g_mix[l]), w_in[l], row(g_gmlp_v[l]), gq_t, gk_t, seg, segt2,
        tm=t_s, tiles_per_batch=1, emit_vn_state=True, precise=True, q_scale=SCORE_SCALE)
    ob = _attn_sample(q, k, v, cache_attn_k[l].reshape(dec_batch, lc, D_MODEL),
                      cache_attn_v[l].reshape(dec_batch, lc, D_MODEL), bias_c, bias_n,
                      batch=dec_batch, rows=dec_seq)
    x1, = _merge(xs, u, vn, sga, sgb, ob, ws_s, bs_s, w_branch[l], w_out[l], tm=t_s, precise=True)
    ys = _channel(x1, p_sample[l].reshape(t_s, PLE_DIM), row(g_ffn[l]), wrg, brg, wre, bre, wg_b, wu_b, wd_b,
                  row(g_ple[l]), wpg_b, wpe_b, tm=t_s, precise_router=True)

    return (yp.reshape(batch, seq, D_MODEL),
            ys.reshape(dec_batch, dec_seq, D_MODEL),
            kst.reshape(1, batch, keep, ATTN_HEADS, HEAD_DIM),
            vst.reshape(1, batch, keep, ATTN_HEADS, HEAD_DIM),
            kss.reshape(1, dec_batch, dec_seq, ATTN_HEADS, HEAD_DIM),
            vss.reshape(1, dec_batch, dec_seq, ATTN_HEADS, HEAD_DIM),
            gvs.reshape(1, dec_batch, dec_seq, D_MODEL))
```

```python
import functools

import jax
import jax.numpy as jnp
import numpy as np
from jax import lax
from jax.experimental import pallas as pl
from jax.experimental.pallas import tpu as pltpu

F32 = jnp.float32
BF16 = jnp.bfloat16

D_MODEL = 1024
CHUNK = 64
N_PAST_CHUNKS = 8
HEAD_DIM = 64
ATTN_HEADS = D_MODEL // HEAD_DIM
MAX_REL = 128
GMLP_CHUNK = 128
GMLP_GROUPS = 4
GMLP_GROUP_DIM = D_MODEL // GMLP_GROUPS
N_SECTIONS = 7
N_GROUPS = 4
EXPERTS_PER_GROUP = 8
N_EXPERTS = N_GROUPS * EXPERTS_PER_GROUP
D_EXPERT = D_MODEL // 4
PLE_DIM = 256
EPS = 1e-6
NEG = -1e30
PAST_LEN = 1024
SCORE_SCALE = HEAD_DIM ** -0.5
LOG2_E = 1.4426950408889634

LANES = 128
Q_BLOCK = 2 * CHUNK
K_WINDOW = Q_BLOCK + N_PAST_CHUNKS * CHUNK
N_SHIFT = N_PAST_CHUNKS * CHUNK // Q_BLOCK + 1
BIAS_SPAN = K_WINDOW + (N_SHIFT - 1) * Q_BLOCK
BIAS_PERIOD = Q_BLOCK + BIAS_SPAN
W_STAGE_COLS = 512
ROUTE_ROWS = 16
ROUTE_TILE = 512
DMA_UNROLL = 8
ATTN_UNROLL = 6
VMEM_LIMIT = 56 * 1024 * 1024


def _rms(x, gain):
    return x * lax.rsqrt(jnp.mean(x * x, axis=-1, keepdims=True) + EPS) * gain


def _act_dtype(precise):
    return F32 if precise else BF16


def _mm(a, b, precise, dims=None):
    dims = dims or (((a.ndim - 1,), (0,)), ((), ()))
    if precise:
        return lax.dot_general(a.astype(F32), b.astype(F32), dims, precision=lax.Precision.HIGHEST,
                               preferred_element_type=F32)
    return lax.dot_general(a.astype(BF16), b.astype(BF16), dims, preferred_element_type=F32)


_NT = (((1,), (1,)), ((), ()))


def _head_sumsq(z, seg, precise):
    return _mm(z * z, seg, precise)


def _head_inv_rms(ssum, segt2, precise):
    inv = lax.rsqrt(ssum * (1.0 / HEAD_DIM) + EPS)
    if precise:
        return _mm(inv, segt2[:LANES], True)
    hi = inv.astype(BF16)
    lo = (inv - hi.astype(F32)).astype(BF16)
    return _mm(jnp.concatenate([hi, lo], axis=1), segt2, False)


def _inproj_kernel(x_ref, gmix_ref, w_ref, gv_ref, gq_ref, gk_ref, seg_ref, segt2_ref,
                   u_ref, vn_ref, q_ref, k_ref, v_ref, sga_ref, sgb_ref,
                   kst_ref, vst_ref, *rest, precise, q_scale):
    act = _act_dtype(precise)
    if precise:
        maybe_vnst_ref = rest
    else:
        *maybe_vnst_ref, w_bf_ref, stage_ref, sems = rest
        w_hbm, w_ref = w_ref, w_bf_ref

        @pl.when(pl.program_id(0) == 0)
        def _():
            width = stage_ref.shape[2]
            n_chunks = w_bf_ref.shape[1] // width

            def chunk_copy(c):
                return pltpu.make_async_copy(w_hbm.at[:, pl.ds(c * width, width)], stage_ref.at[c % 2], sems.at[c % 2])
            chunk_copy(0).start()
            for c in range(n_chunks):
                if c + 1 < n_chunks:
                    chunk_copy(c + 1).start()
                chunk_copy(c).wait()
                w_bf_ref[:, c * width:(c + 1) * width] = stage_ref[c % 2].astype(BF16)

    h = _rms(x_ref[...], gmix_ref[...]).astype(act)

    def section(s):
        return _mm(h, w_ref[:, s * D_MODEL:(s + 1) * D_MODEL], precise)

    z_u = section(0)
    z_v = section(1)
    u_ref[...] = jax.nn.gelu(z_u).astype(act)
    z_q = section(2)
    vn = _rms(jax.nn.gelu(z_v), gv_ref[...])
    vn_ref[...] = vn.astype(act)
    if maybe_vnst_ref:
        maybe_vnst_ref[0][...] = vn
    z_k = section(3)
    ss_q = _head_sumsq(z_q, seg_ref[...], precise)
    v = section(4)
    ss_k = _head_sumsq(z_k, seg_ref[...], precise)
    z_ga = section(5)
    inv_q = _head_inv_rms(ss_q, segt2_ref[...], precise)
    z_gb = section(6)
    inv_k = _head_inv_rms(ss_k, segt2_ref[...], precise)

    v_ref[...] = v.astype(act)
    vst_ref[...] = v
    sga_ref[...] = jax.nn.sigmoid(z_ga).astype(act)
    q_ref[...] = (z_q * inv_q * gq_ref[...] * q_scale).astype(act)
    sgb_ref[...] = jax.nn.sigmoid(z_gb).astype(act)
    kn = z_k * inv_k * gk_ref[...]
    k_ref[...] = kn.astype(act)
    kst_ref[...] = kn


def _inproj(x, gmix, w_in, gv, gq_t, gk_t, seg, segt2, *, tm, tiles_per_batch, emit_vn_state, precise, q_scale):
    t = x.shape[0]
    n_tiles = t // tm
    n_state = n_tiles // tiles_per_batch
    row = lambda i: (i, 0)
    const = lambda i: (0, 0)
    state = lambda i: (i // tiles_per_batch, 0)
    act = jax.ShapeDtypeStruct((t, D_MODEL), _act_dtype(precise))
    st = jax.ShapeDtypeStruct((n_state * tm, D_MODEL), F32)
    out_shape = [act] * 7 + [st, st] + ([st] if emit_vn_state else [])
    out_specs = [pl.BlockSpec((tm, D_MODEL), row)] * 7 + [pl.BlockSpec((tm, D_MODEL), state)] * (
        3 if emit_vn_state else 2)
    if precise:
        w_spec = pl.BlockSpec((D_MODEL, N_SECTIONS * D_MODEL), const, pipeline_mode=pl.Buffered(1))
        scratch = []
    else:
        w_spec = pl.BlockSpec(memory_space=pl.ANY)
        scratch = [pltpu.VMEM((D_MODEL, N_SECTIONS * D_MODEL), BF16), pltpu.VMEM((2, D_MODEL, W_STAGE_COLS), F32),
                   pltpu.SemaphoreType.DMA((2,))]
    return pl.pallas_call(
        functools.partial(_inproj_kernel, precise=precise, q_scale=q_scale),
        grid=(n_tiles,),
        scratch_shapes=scratch,
        in_specs=[
            pl.BlockSpec((tm, D_MODEL), row),
            pl.BlockSpec((1, D_MODEL), const),
            w_spec,
            pl.BlockSpec((1, D_MODEL), const),
            pl.BlockSpec((1, D_MODEL), const),
            pl.BlockSpec((1, D_MODEL), const),
            pl.BlockSpec((D_MODEL, LANES), const),
            pl.BlockSpec((2 * LANES, D_MODEL), const),
        ],
        out_specs=out_specs,
        out_shape=out_shape,
        compiler_params=pltpu.CompilerParams(
            dimension_semantics=("arbitrary",), vmem_limit_bytes=VMEM_LIMIT),
        name="inproj",
    )(x, gmix, w_in, gv, gq_t, gk_t, seg, segt2)


def _attn_prompt_kernel(q_ref, k_ref, v_ref, period_ref, o_ref, bias_ref, *, seq):
    lane = lax.broadcasted_iota(jnp.int32, (1, LANES), 1)
    low_head = lane < HEAD_DIM
    zero = jnp.zeros((), BF16)

    @pl.when(pl.program_id(1) == 0)
    def _():
        row = lax.broadcasted_iota(jnp.int32, (Q_BLOCK, BIAS_SPAN), 0)
        col = lax.broadcasted_iota(jnp.int32, (Q_BLOCK, BIAS_SPAN), 1)
        chunk_gap = col // CHUNK - row // CHUNK
        in_band = (chunk_gap >= 0) & (chunk_gap <= N_PAST_CHUNKS)
        for e in range(2):
            period = jnp.broadcast_to(period_ref[e:e + 1, :], (Q_BLOCK, BIAS_PERIOD))
            toeplitz = pltpu.roll(period, 0, 1, stride=1, stride_axis=0)
            bias_ref[e * Q_BLOCK:(e + 1) * Q_BLOCK, :] = jnp.where(in_band, toeplitz[:, :BIAS_SPAN], NEG)

    def scores(qb, shift):
        q_start = pl.multiple_of(qb * Q_BLOCK, Q_BLOCK)
        k_start = pl.multiple_of((qb + shift) * Q_BLOCK - (K_WINDOW - Q_BLOCK), Q_BLOCK)
        q = q_ref[pl.ds(q_start, Q_BLOCK), :]
        k = k_ref[pl.ds(k_start, K_WINDOW), :]
        q2 = jnp.concatenate([jnp.where(low_head, q, zero), jnp.where(low_head, zero, q)], axis=0)
        bias = bias_ref[:, shift * Q_BLOCK:shift * Q_BLOCK + K_WINDOW]
        return lax.dot_general(q2, k, _NT, preferred_element_type=F32) + bias, q_start, k_start

    ones = jnp.ones((K_WINDOW, LANES), BF16)

    def attend(s, q_start, k_start):
        v = v_ref[pl.ds(k_start, K_WINDOW), :]
        m = jnp.max(s, axis=-1, keepdims=True)
        p = jnp.exp2((s - m).astype(BF16))
        ol = jnp.dot(p, jnp.concatenate([v, ones], axis=1), preferred_element_type=F32)
        o = ol[:, :LANES] * (1.0 / ol[:, LANES:])
        o_ref[pl.ds(q_start, Q_BLOCK), :] = jnp.where(low_head, o[:Q_BLOCK], o[Q_BLOCK:]).astype(BF16)

    def blocks(specs):
        for args in [scores(qb, shift) for qb, shift in specs]:
            attend(*args)

    blocks([(qb, N_SHIFT - 1 - qb) for qb in range(N_SHIFT - 1)])

    def steady(j, carry):
        first = N_SHIFT - 1 + j * ATTN_UNROLL
        blocks([(first + u, 0) for u in range(ATTN_UNROLL)])
        return carry

    lax.fori_loop(0, (seq // Q_BLOCK - (N_SHIFT - 1)) // ATTN_UNROLL, steady, 0)


def _attn_prompt(q, k, v, bias_tab, *, batch, seq):
    q3, k3, v3 = (a.reshape(batch, seq, D_MODEL) for a in (q, k, v))
    head_pair = lambda hp, b: (b, 0, hp)
    spec = pl.BlockSpec((None, seq, LANES), head_pair)
    out = pl.pallas_call(
        functools.partial(_attn_prompt_kernel, seq=seq),
        grid=(ATTN_HEADS // 2, batch),
        in_specs=[spec, spec, spec,
                  pl.BlockSpec((None, 2, BIAS_PERIOD), lambda hp, b: (hp, 0, 0))],
        out_specs=spec,
        out_shape=jax.ShapeDtypeStruct((batch, seq, D_MODEL), BF16),
        scratch_shapes=[pltpu.VMEM((2 * Q_BLOCK, BIAS_SPAN), F32)],
        compiler_params=pltpu.CompilerParams(
            dimension_semantics=("arbitrary", "arbitrary"), vmem_limit_bytes=VMEM_LIMIT),
        name="attn_prompt",
    )(q3, k3, v3, bias_tab)
    return out.reshape(batch * seq, D_MODEL)


def _rel_bias_period(rel_bias, q0, k0, rows, ncols):
    dmax = q0 - k0 + rows - 1
    total = rows + ncols - 1
    dmin = dmax - total + 1
    n_lo = min(max(-MAX_REL - dmin, 0), total)
    n_hi = min(max(dmax - MAX_REL, 0), total)
    mid0 = dmin + n_lo + MAX_REL
    asc = jnp.concatenate([jnp.repeat(rel_bias[:, :1], n_lo, axis=1),
                           rel_bias[:, mid0:mid0 + total - n_lo - n_hi],
                           jnp.repeat(rel_bias[:, -1:], n_hi, axis=1)], axis=1).astype(F32)
    desc = asc[:, ::-1]
    heads = desc.shape[0]
    return jnp.concatenate([desc[:, rows - 1:], jnp.zeros((heads, 1), F32), desc[:, :rows - 1]], axis=1)


def _rel_bias_rows(rel_bias, q0, k0, rows, ncols):
    period = _rel_bias_period(rel_bias, q0, k0, rows, ncols)
    total = rows + ncols - 1
    flat = jnp.tile(period, (1, rows))[:, :rows * total]
    return flat.reshape(period.shape[0], rows, total)[:, :, :ncols]


def _band_ok(q_pos, k_pos):
    q_pos, k_pos = q_pos[:, None], k_pos[None, :]
    return (k_pos >= 0) & (k_pos // CHUNK <= q_pos // CHUNK) & (k_pos // CHUNK >= q_pos // CHUNK - N_PAST_CHUNKS)


def _prompt_bias_period(rel_bias):
    period = _rel_bias_period(rel_bias, N_PAST_CHUNKS * CHUNK, 0, Q_BLOCK, BIAS_SPAN)
    return period.reshape(ATTN_HEADS // 2, 2, BIAS_PERIOD)


def _attn_sample_kernel(q_ref, k_ref, v_ref, ck_ref, cv_ref, bc_ref, bn_ref, o_ref):
    rows = q_ref.shape[0]
    lane = lax.broadcasted_iota(jnp.int32, (1, LANES), 1)
    low_head = lane < HEAD_DIM
    scored = []
    for hp in range(ATTN_HEADS // 2):
        cols = slice(hp * LANES, (hp + 1) * LANES)
        q = q_ref[:, cols]
        q2 = jnp.concatenate([jnp.where(low_head, q, 0.0), jnp.where(low_head, 0.0, q)], axis=0)
        s_c = _mm(q2, ck_ref[:, cols], True, _NT) + bc_ref[hp]
        s_n = _mm(q2, k_ref[:, cols], True, _NT) + bn_ref[hp]
        scored.append((cols, s_c, s_n))
    for cols, s_c, s_n in scored:
        m = jnp.maximum(jnp.max(s_c, axis=-1, keepdims=True), jnp.max(s_n, axis=-1, keepdims=True))
        p_c = jnp.exp(s_c - m)
        p_n = jnp.exp(s_n - m)
        l = jnp.sum(p_c, axis=-1, keepdims=True) + jnp.sum(p_n, axis=-1, keepdims=True)
        o = (_mm(p_c, cv_ref[:, cols], True) + _mm(p_n, v_ref[:, cols], True)) * (1.0 / l)
        o_ref[:, cols] = jnp.where(low_head, o[:rows], o[rows:])


def _attn_sample(q, k, v, cache_k, cache_v, bias_c, bias_n, *, batch, rows):
    lc = cache_k.shape[1]
    new = pl.BlockSpec((rows, D_MODEL), lambda b: (b, 0))
    cache = pl.BlockSpec((None, lc, D_MODEL), lambda b: (b, 0, 0))
    return pl.pallas_call(
        _attn_sample_kernel,
        grid=(batch,),
        in_specs=[new, new, new, cache, cache,
                  pl.BlockSpec((ATTN_HEADS // 2, 2 * rows, lc), lambda b: (0, 0, 0)),
                  pl.BlockSpec((ATTN_HEADS // 2, 2 * rows, rows), lambda b: (0, 0, 0))],
        out_specs=new,
        out_shape=jax.ShapeDtypeStruct((batch * rows, D_MODEL), F32),
        compiler_params=pltpu.CompilerParams(
            dimension_semantics=("arbitrary",), vmem_limit_bytes=VMEM_LIMIT),
        name="attn_sample",
    )(q, k, v, cache_k, cache_v, bias_c, bias_n)


def _sample_bias_tables(rel_bias, rows, lc, past_len):
    q_pos = past_len + np.arange(rows)
    c_pos = past_len - lc + np.arange(lc)
    bias_c = jnp.where(_band_ok(q_pos, c_pos)[None], _rel_bias_rows(rel_bias, past_len, past_len - lc, rows, lc), NEG)
    bias_n = jnp.where(_band_ok(q_pos, q_pos)[None], _rel_bias_rows(rel_bias, past_len, past_len, rows, rows), NEG)
    pair = lambda tab: tab.reshape(ATTN_HEADS // 2, 2 * rows, tab.shape[-1])
    return pair(bias_c), pair(bias_n)


def _route_rows(x1, gffn_ref, wrgt_ref, brgt_ref, upper_ref, gidx_ref, rank_ref, base_ref):
    tm = x1.shape[0]
    h = _rms(x1, gffn_ref[...]).astype(BF16)
    lgt = lax.dot_general(wrgt_ref[...], h, _NT, preferred_element_type=F32) + brgt_ref[...]
    sub = lax.broadcasted_iota(jnp.int32, (ROUTE_ROWS, tm), 0).astype(F32)
    top = jnp.max(lgt, axis=0, keepdims=True)
    gi = jnp.min(jnp.where(lgt == top, sub, float(ROUTE_ROWS)), axis=0, keepdims=True)
    onehot = jnp.where(sub == gi, 1.0, 0.0)
    before = jnp.dot(onehot.astype(BF16), upper_ref[...], preferred_element_type=F32)
    base = base_ref[...]
    rank = jnp.sum(onehot * (before + base[:, :1]), axis=0, keepdims=True)
    base_ref[...] = base + jnp.sum(onehot, axis=1, keepdims=True)
    gidx_ref[...] = gi.astype(jnp.int32)
    rank_ref[...] = rank.astype(jnp.int32)


def _merge_kernel(x_ref, u_ref, vn_ref, sga_ref, sgb_ref, ob_ref, ws_ref, bs_ref, wb_ref, wout_ref,
                  *rest, tm, precise, route):
    if route:
        gffn_ref, wrgt_ref, brgt_ref, upper_ref, x1_ref, gidx_ref, rank_ref, oa_ref, base_ref = rest
    else:
        x1_ref, oa_ref = rest
    act = _act_dtype(precise)
    if route:
        @pl.when(pl.program_id(0) == 0)
        def _():
            base_ref[...] = jnp.zeros_like(base_ref)

    b_b = _mm(ob_ref[...], wb_ref[1], precise)
    for c in range(tm // GMLP_CHUNK):
        rows = slice(c * GMLP_CHUNK, (c + 1) * GMLP_CHUNK)
        for g in range(GMLP_GROUPS):
            cols = slice(g * GMLP_GROUP_DIM, (g + 1) * GMLP_GROUP_DIM)
            mix = _mm(ws_ref[g], vn_ref[rows, cols], precise) + bs_ref[:, cols]
            oa_ref[rows, cols] = (u_ref[rows, cols].astype(F32) * mix).astype(act)
    b_a = _mm(oa_ref[...], wb_ref[0], precise)
    m = sga_ref[...].astype(F32) * b_a + sgb_ref[...].astype(F32) * b_b
    x1 = x_ref[...] + _mm(m, wout_ref[...], precise)
    x1_ref[...] = x1
    if route:
        _route_rows(x1, gffn_ref, wrgt_ref, brgt_ref, upper_ref, gidx_ref, rank_ref, base_ref)


def _merge(x, u, vn, sga, sgb, ob, ws, bs_full, wb, wout, route_params=None, *, tm, precise):
    t = x.shape[0]
    n_tiles = t // tm
    route = route_params is not None
    row = pl.BlockSpec((tm, D_MODEL), lambda i: (i, 0))
    const = lambda i: (0, 0)
    in_specs = [row, row, row, row, row, row,
                pl.BlockSpec((GMLP_GROUPS, GMLP_CHUNK, GMLP_CHUNK), lambda i: (0, 0, 0)),
                pl.BlockSpec((GMLP_CHUNK, D_MODEL), const),
                pl.BlockSpec((2, D_MODEL, D_MODEL), lambda i: (0, 0, 0)),
                pl.BlockSpec((D_MODEL, D_MODEL), const)]
    out_specs = [row]
    out_shape = [jax.ShapeDtypeStruct((t, D_MODEL), F32)]
    scratch = [pltpu.VMEM((tm, D_MODEL), _act_dtype(precise))]
    args = [x, u, vn, sga, sgb, ob, ws, bs_full, wb, wout]
    if route:
        in_specs += [pl.BlockSpec((1, D_MODEL), const), pl.BlockSpec((ROUTE_ROWS, D_MODEL), const),
                     pl.BlockSpec((ROUTE_ROWS, 1), const), pl.BlockSpec((tm, tm), const)]
        tok_row = pl.BlockSpec((None, 1, tm), lambda i: (i, 0, 0))
        out_specs += [tok_row, tok_row]
        out_shape += [jax.ShapeDtypeStruct((n_tiles, 1, tm), jnp.int32)] * 2
        scratch += [pltpu.VMEM((ROUTE_ROWS, LANES), F32)]
        args += list(route_params)
    return pl.pallas_call(
        functools.partial(_merge_kernel, tm=tm, precise=precise, route=route),
        grid=(n_tiles,),
        in_specs=in_specs,
        out_specs=out_specs,
        out_shape=out_shape,
        scratch_shapes=scratch,
        compiler_params=pltpu.CompilerParams(
            dimension_semantics=("arbitrary",), vmem_limit_bytes=VMEM_LIMIT),
        name="merge",
    )(*args)


def _first_index_of_max(vals, lane):
    top = jnp.max(vals, axis=-1, keepdims=True)
    idx = jnp.min(jnp.where(vals == top, lane, float(LANES)), axis=-1, keepdims=True)
    return top, idx


def _top2_weights(le_in_group, lane, pg_top):
    t1, i1 = _first_index_of_max(le_in_group, lane)
    t2, i2 = _first_index_of_max(jnp.where(lane == i1, -jnp.inf, le_in_group), lane)
    e2 = jnp.exp(t2 - t1)
    den = 1.0 + e2
    return jnp.where(lane == i1, pg_top * (1.0 / den), jnp.where(lane == i2, pg_top * (e2 / den), 0.0))


def _group_experts(h, comb, lane_i, g, wg_ref, wu_ref, wd_ref):
    hids = []
    for j in range(EXPERTS_PER_GROUP):
        w_tok = jnp.sum(jnp.where(lane_i == g * EXPERTS_PER_GROUP + j, comb, 0.0), axis=-1, keepdims=True)
        a = jnp.dot(h, wg_ref[j], preferred_element_type=F32)
        b = jnp.dot(h, wu_ref[j], preferred_element_type=F32)
        hids.append((jax.nn.silu(a) * b * w_tok).astype(BF16))
    hid = jnp.concatenate(hids, axis=1)
    return jnp.dot(hid, wd_ref[...].reshape(EXPERTS_PER_GROUP * D_EXPERT, D_MODEL), preferred_element_type=F32)


def _embedding_gate(x2, p_ref, gple_ref, wpg_ref, wpe_ref):
    pe = jnp.dot(p_ref[...].astype(BF16), wpe_ref[...], preferred_element_type=F32)
    gate = jax.nn.sigmoid(jnp.dot(_rms(x2, gple_ref[...]).astype(BF16), wpg_ref[...], preferred_element_type=F32))
    return x2 + gate * pe


def _gather_rows(src_hbm, index_ref, first, buf, slot, sem):
    if isinstance(slot, int):
        for r in range(buf.shape[1]):
            pltpu.make_async_copy(src_hbm.at[index_ref[first + r]], buf.at[slot, r], sem.at[slot]).start(
                priority=r % 2)
        return

    def issue(r, carry):
        pltpu.make_async_copy(src_hbm.at[index_ref[first + r]], buf.at[slot, r], sem.at[slot]).start()
        return carry
    lax.fori_loop(0, buf.shape[1], issue, 0, unroll=DMA_UNROLL)


def _gather_next_tile(src_hbm, index_ref, i, n_tiles, buf, sem):
    for parity in range(2):
        @pl.when((i + 1 < n_tiles) & (i % 2 == parity))
        def _():
            _gather_rows(src_hbm, index_ref, (i + 1) * buf.shape[1], buf, 1 - parity, sem)


def _wait_rows(src_hbm, buf, slot, sem):
    pltpu.make_async_copy(src_hbm.at[pl.ds(0, buf.shape[1])], buf.at[slot], sem.at[slot]).wait()


def _experts_kernel(src_row_ref, tile_group_ref, n_valid_ref, x1_hbm, gffn_ref, wroute_ref, broute_ref,
                    wg_ref, wu_ref, wd_ref, ys_ref, xbuf, sems):
    i = pl.program_id(0)
    n_valid = n_valid_ref[0]

    @pl.when(i == 0)
    def _():
        _gather_rows(x1_hbm, src_row_ref, 0, xbuf, 0, sems)

    _gather_next_tile(x1_hbm, src_row_ref, i, n_valid, xbuf, sems)

    @pl.when(i < n_valid)
    def _():
        slot = i % 2
        _wait_rows(x1_hbm, xbuf, slot, sems)
        g = tile_group_ref[i]
        lane_i = lax.broadcasted_iota(jnp.int32, (1, LANES), 1)
        lane = lane_i.astype(F32)
        h = _rms(xbuf[slot], gffn_ref[...]).astype(BF16)
        logits = _mm(h, wroute_ref[...], False) + broute_ref[...]
        is_group = (lane_i >= N_EXPERTS) & (lane_i < N_EXPERTS + N_GROUPS)
        lg = jnp.where(is_group, logits, NEG)
        eg = jnp.exp(lg - jnp.max(lg, axis=-1, keepdims=True))
        pg = (jnp.sum(jnp.where(lane_i == N_EXPERTS + g, eg, 0.0), axis=-1, keepdims=True)
              / jnp.sum(eg, axis=-1, keepdims=True))
        comb = _top2_weights(jnp.where(lane_i // EXPERTS_PER_GROUP == g, logits, -jnp.inf), lane, pg)
        ys_ref[...] = _group_experts(h, comb, lane_i, g, wg_ref, wu_ref, wd_ref)

    @pl.when(i >= n_valid)
    def _():
        ys_ref[...] = jnp.zeros_like(ys_ref)


def _experts(src_row, tile_group, n_valid, x1, gffn, wroute, broute, wg, wu, wd, *, n_tiles):
    const = lambda i, sr, tg, nv: (0, 0)
    group = lambda i, sr, tg, nv: (tg[i], 0, 0)
    return pl.pallas_call(
        _experts_kernel,
        grid_spec=pltpu.PrefetchScalarGridSpec(
            num_scalar_prefetch=3,
            grid=(n_tiles,),
            in_specs=[
                pl.BlockSpec(memory_space=pl.ANY),
                pl.BlockSpec((1, D_MODEL), const),
                pl.BlockSpec((D_MODEL, LANES), const),
                pl.BlockSpec((1, LANES), const),
                pl.BlockSpec((EXPERTS_PER_GROUP, D_MODEL, D_EXPERT), group),
                pl.BlockSpec((EXPERTS_PER_GROUP, D_MODEL, D_EXPERT), group),
                pl.BlockSpec((EXPERTS_PER_GROUP, D_EXPERT, D_MODEL), group),
            ],
            out_specs=pl.BlockSpec((ROUTE_TILE, D_MODEL), lambda i, sr, tg, nv: (i, 0)),
            scratch_shapes=[pltpu.VMEM((2, ROUTE_TILE, D_MODEL), F32), pltpu.SemaphoreType.DMA((2,))]),
        out_shape=jax.ShapeDtypeStruct((n_tiles * ROUTE_TILE, D_MODEL), F32),
        compiler_params=pltpu.CompilerParams(
            dimension_semantics=("arbitrary",), vmem_limit_bytes=VMEM_LIMIT,
            allow_input_fusion=[False] * 7 + [True] * 3),
        name="experts",
    )(src_row, tile_group, n_valid, x1, gffn, wroute, broute, wg, wu, wd)


def _combine_kernel(pos_ref, x1_ref, p_ref, gple_ref, wpg_ref, wpe_ref, ys_hbm, out_ref, ybuf, sems, *, tm):
    i = pl.program_id(0)

    @pl.when(i == 0)
    def _():
        _gather_rows(ys_hbm, pos_ref, 0, ybuf, 0, sems)

    _gather_next_tile(ys_hbm, pos_ref, i, pl.num_programs(0), ybuf, sems)

    slot = i % 2
    _wait_rows(ys_hbm, ybuf, slot, sems)
    out_ref[...] = _embedding_gate(x1_ref[...] + ybuf[slot], p_ref, gple_ref, wpg_ref, wpe_ref)


def _combine(pos, x1, p, gple, wpg, wpe, ys, *, tm):
    t = x1.shape[0]
    row = lambda i, pos: (i, 0)
    const = lambda i, pos: (0, 0)
    return pl.pallas_call(
        functools.partial(_combine_kernel, tm=tm),
        grid_spec=pltpu.PrefetchScalarGridSpec(
            num_scalar_prefetch=1,
            grid=(t // tm,),
            in_specs=[pl.BlockSpec((tm, D_MODEL), row),
                      pl.BlockSpec((tm, PLE_DIM), row),
                      pl.BlockSpec((1, D_MODEL), const),
                      pl.BlockSpec((D_MODEL, D_MODEL), const),
                      pl.BlockSpec((PLE_DIM, D_MODEL), const),
                      pl.BlockSpec(memory_space=pl.ANY)],
            out_specs=pl.BlockSpec((tm, D_MODEL), row),
            scratch_shapes=[pltpu.VMEM((2, tm, D_MODEL), F32), pltpu.SemaphoreType.DMA((2,))]),
        out_shape=jax.ShapeDtypeStruct((t, D_MODEL), F32),
        compiler_params=pltpu.CompilerParams(
            dimension_semantics=("arbitrary",), vmem_limit_bytes=VMEM_LIMIT,
            allow_input_fusion=[False] * 4 + [True, True, False]),
        name="combine",
    )(pos, x1, p, gple, wpg, wpe, ys)


def _route_plan(gidx, rank, *, n_tiles):
    t = gidx.shape[0]
    groups = jnp.arange(N_GROUPS, dtype=jnp.int32)
    member = gidx[None, :] == groups[:, None]
    counts = jnp.sum(member, axis=1, dtype=jnp.int32)
    padded = (counts + ROUTE_TILE - 1) // ROUTE_TILE * ROUTE_TILE
    ends = jnp.cumsum(padded)
    offsets = ends - padded
    pos = rank + jnp.sum(jnp.where(member, offsets[:, None], 0), axis=0, dtype=jnp.int32)
    src_row = jnp.zeros((n_tiles * ROUTE_TILE,), jnp.int32).at[pos].set(
        jnp.arange(t, dtype=jnp.int32), unique_indices=True, indices_are_sorted=False)
    n_valid = jnp.maximum(ends[-1] // ROUTE_TILE, 1)
    tile_start = jnp.minimum(jnp.arange(n_tiles, dtype=jnp.int32), n_valid - 1) * ROUTE_TILE
    tile_group = jnp.minimum(jnp.sum(tile_start[:, None] >= ends[None, :], axis=1, dtype=jnp.int32), N_GROUPS - 1)
    return pos, src_row, tile_group, n_valid.reshape(1).astype(jnp.int32)


def _channel_kernel(x1_ref, p_ref, gffn_ref, wrg_ref, brg_ref, wre_ref, bre_ref,
                    wg_ref, wu_ref, wd_ref, gple_ref, wpg_ref, wpe_ref,
                    out_ref, h_ref, comb_ref, acc_ref, *, precise_router):
    g = pl.program_id(1)
    lane_i = lax.broadcasted_iota(jnp.int32, (1, LANES), 1)
    lane = lane_i.astype(F32)

    @pl.when(g == 0)
    def _():
        h = _rms(x1_ref[...], gffn_ref[...])
        h_ref[...] = h.astype(BF16)
        lg = _mm(h, wrg_ref[...], precise_router) + brg_ref[...]
        top_g, g_idx = _first_index_of_max(lg, lane)
        pg_top = 1.0 / jnp.sum(jnp.exp(lg - top_g), axis=-1, keepdims=True)
        le = _mm(h, wre_ref[...], precise_router) + bre_ref[...]
        group_of_lane = (lane_i // EXPERTS_PER_GROUP).astype(F32)
        comb_ref[...] = _top2_weights(jnp.where(group_of_lane == g_idx, le, -jnp.inf), lane, pg_top)

    y = _group_experts(h_ref[...], comb_ref[...], lane_i, g, wg_ref, wu_ref, wd_ref)

    @pl.when(g == 0)
    def _():
        acc_ref[...] = y

    @pl.when(g > 0)
    def _():
        acc_ref[...] += y

    @pl.when(g == N_GROUPS - 1)
    def _():
        out_ref[...] = _embedding_gate(x1_ref[...] + acc_ref[...], p_ref, gple_ref, wpg_ref, wpe_ref)


def _channel(x1, p, gffn, wrg, brg, wre, bre, wg, wu, wd, gple, wpg, wpe, *, tm, precise_router):
    t = x1.shape[0]
    row = lambda i, g: (i, 0)
    const2 = lambda i, g: (0, 0)
    group = lambda i, g: (g, 0, 0)
    return pl.pallas_call(
        functools.partial(_channel_kernel, precise_router=precise_router),
        grid=(t // tm, N_GROUPS),
        in_specs=[
            pl.BlockSpec((tm, D_MODEL), row),
            pl.BlockSpec((tm, PLE_DIM), row),
            pl.BlockSpec((1, D_MODEL), const2),
            pl.BlockSpec((D_MODEL, LANES), const2),
            pl.BlockSpec((1, LANES), const2),
            pl.BlockSpec((D_MODEL, LANES), const2),
            pl.BlockSpec((1, LANES), const2),
            pl.BlockSpec((EXPERTS_PER_GROUP, D_MODEL, D_EXPERT), group),
            pl.BlockSpec((EXPERTS_PER_GROUP, D_MODEL, D_EXPERT), group),
            pl.BlockSpec((EXPERTS_PER_GROUP, D_EXPERT, D_MODEL), group),
            pl.BlockSpec((1, D_MODEL), const2),
            pl.BlockSpec((D_MODEL, D_MODEL), const2),
            pl.BlockSpec((PLE_DIM, D_MODEL), const2),
        ],
        out_specs=pl.BlockSpec((tm, D_MODEL), row),
        out_shape=jax.ShapeDtypeStruct((t, D_MODEL), F32),
        scratch_shapes=[pltpu.VMEM((tm, D_MODEL), BF16),
                        pltpu.VMEM((tm, LANES), F32),
                        pltpu.VMEM((tm, D_MODEL), F32)],
        compiler_params=pltpu.CompilerParams(
            dimension_semantics=("arbitrary", "arbitrary"), vmem_limit_bytes=VMEM_LIMIT,
            allow_input_fusion=[False] * 7 + [True] * 3 + [False, True, True]),
        name="channel",
    )(x1, p, gffn, wrg, brg, wre, bre, wg, wu, wd, gple, wpg, wpe)


def _pad_lanes(a, fill):
    return jnp.pad(a, ((0, 0), (0, LANES - a.shape[1])), constant_values=fill)


def kernel(x_prompt, x_sample, cache_attn_k, cache_attn_v, p_prompt, p_sample, g_mix, w_in, g_gmlp_v, w_gmlp_s, b_gmlp_s, g_q, g_k, rel_bias, w_branch, w_out, g_ffn, w_router_group, b_router_group, w_router_expert, b_router_expert, w_exp_gate, w_exp_up, w_exp_down, g_ple, w_ple_gate, w_ple_proj):
    depth = g_mix.shape[0]
    batch, seq, _ = x_prompt.shape
    dec_batch, dec_seq, _ = x_sample.shape
    lc = cache_attn_k.shape[2]
    keep = min(N_PAST_CHUNKS * CHUNK, seq)
    tm = 512
    assert depth == 1 and keep == tm and seq % tm == 0 and seq % Q_BLOCK == 0
    assert dec_seq <= GMLP_CHUNK and GMLP_CHUNK % dec_seq == 0 and dec_batch * dec_seq == GMLP_CHUNK
    l = 0
    t_s = dec_batch * dec_seq

    row = lambda a: a.reshape(1, -1).astype(F32)
    gq_t = row(jnp.tile(g_q[l], ATTN_HEADS))
    gk_t = row(jnp.tile(g_k[l], ATTN_HEADS))
    head_of_lane = jnp.arange(D_MODEL, dtype=jnp.int32) // HEAD_DIM
    seg = (head_of_lane[:, None] == jnp.arange(LANES, dtype=jnp.int32)[None, :]).astype(BF16)
    segt2 = jnp.concatenate([seg.T, seg.T], axis=0)
    ws_tril = jnp.tril(w_gmlp_s[l])
    ws_p = ws_tril.astype(BF16)
    bs_p = jnp.repeat(b_gmlp_s[l].T, GMLP_GROUP_DIM, axis=1).astype(F32)
    eye = jnp.eye(dec_batch, dtype=F32)
    ws_s = jnp.einsum('ab,gts->gatbs', eye, ws_tril[:, :dec_seq, :dec_seq]).reshape(
        GMLP_GROUPS, t_s, t_s)
    bs_s = jnp.tile(jnp.repeat(b_gmlp_s[l][:, :dec_seq].T, GMLP_GROUP_DIM, axis=1), (dec_batch, 1)).astype(F32)
    wb_b = w_branch[l].astype(BF16)
    wout_b = w_out[l].astype(BF16)
    wrg = _pad_lanes(w_router_group[l], 0.0)
    brg = _pad_lanes(row(b_router_group[l]), NEG)
    wre = _pad_lanes(w_router_expert[l], 0.0)
    bre = _pad_lanes(row(b_router_expert[l]), 0.0)
    wg_b = w_exp_gate[l].astype(BF16)
    wu_b = w_exp_up[l].astype(BF16)
    wd_b = w_exp_down[l].astype(BF16)
    wpg_b = w_ple_gate[l].astype(BF16)
    wpe_b = w_ple_proj[l].astype(BF16)
    bias_prompt = _prompt_bias_period(rel_bias[l] * LOG2_E)
    bias_c, bias_n = _sample_bias_tables(rel_bias[l], dec_seq, lc, PAST_LEN)

    wrgt = jnp.pad(w_router_group[l].T, ((0, ROUTE_ROWS - N_GROUPS), (0, 0))).astype(BF16)
    brgt = jnp.pad(b_router_group[l].astype(F32), (0, ROUTE_ROWS - N_GROUPS), constant_values=NEG).reshape(ROUTE_ROWS, 1)
    upper = jnp.asarray(np.triu(np.ones((tm, tm), np.float32), 1), BF16)

    xp = x_prompt.reshape(batch * seq, D_MODEL)
    u, vn, q, k, v, sga, sgb, kst, vst = _inproj(
        xp, row(g_mix[l]), w_in[l], row(g_gmlp_v[l]), gq_t, gk_t, seg, segt2,
        tm=tm, tiles_per_batch=seq // tm, emit_vn_state=False, precise=False, q_scale=SCORE_SCALE * LOG2_E)
    ob = _attn_prompt(q, k, v, bias_prompt, batch=batch, seq=seq)
    x1, gidx, rank = _merge(xp, u, vn, sga, sgb, ob, ws_p, bs_p, wb_b, wout_b,
                            (row(g_ffn[l]), wrgt, brgt, upper), tm=tm, precise=False)
    n_route_tiles = batch * seq // ROUTE_TILE + N_GROUPS
    pos, src_row, tile_group, n_valid = _route_plan(gidx.reshape(-1), rank.reshape(-1), n_tiles=n_route_tiles)
    w_route = _pad_lanes(jnp.concatenate([w_router_expert[l], w_router_group[l]], axis=1), 0.0).astype(BF16)
    b_route = _pad_lanes(row(jnp.concatenate([b_router_expert[l], b_router_group[l]])), 0.0)
    y_sorted = _experts(src_row, tile_group, n_valid, x1, row(g_ffn[l]), w_route, b_route,
                        wg_b, wu_b, wd_b, n_tiles=n_route_tiles)
    yp = _combine(pos, x1, p_prompt[l].reshape(batch * seq, PLE_DIM), row(g_ple[l]), wpg_b, wpe_b, y_sorted, tm=tm)

    xs = x_sample.reshape(t_s, D_MODEL)
    u, vn, q, k, v, sga, sgb, kss, vss, gvs = _inproj(
        xs, row(g_mix[l]), w_in[l], row(g_gmlp_v[l]), gq_t, gk_t, seg, segt2,
        tm=t_s, tiles_per_batch=1, emit_vn_state=True, precise=True, q_scale=SCORE_SCALE)
    ob = _attn_sample(q, k, v, cache_attn_k[l].reshape(dec_batch, lc, D_MODEL),
                      cache_attn_v[l].reshape(dec_batch, lc, D_MODEL), bias_c, bias_n,
                      batch=dec_batch, rows=dec_seq)
    x1, = _merge(xs, u, vn, sga, sgb, ob, ws_s, bs_s, w_branch[l], w_out[l], tm=t_s, precise=True)
    ys = _channel(x1, p_sample[l].reshape(t_s, PLE_DIM), row(g_ffn[l]), wrg, brg, wre, bre, wg_b, wu_b, wd_b,
                  row(g_ple[l]), wpg_b, wpe_b, tm=t_s, precise_router=True)

    return (yp.reshape(batch, seq, D_MODEL),
            ys.reshape(dec_batch, dec_seq, D_MODEL),
            kst.reshape(1, batch, keep, ATTN_HEADS, HEAD_DIM),
            vst.reshape(1, batch, keep, ATTN_HEADS, HEAD_DIM),
            kss.reshape(1, dec_batch, dec_seq, ATTN_HEADS, HEAD_DIM),
            vss.reshape(1, dec_batch, dec_seq, ATTN_HEADS, HEAD_DIM),
            gvs.reshape(1, dec_batch, dec_seq, D_MODEL))
```
